```python
import math
import jax, jax.numpy as jnp
from jax import lax
import numpy as np

D_MODEL = 1024
BATCH = 4
SEQ = 8192
DEPTH = 4

N_SSM_LAYERS = DEPTH // 2
N_ATTN_LAYERS = DEPTH - N_SSM_LAYERS
SSM_GROUP = 16
SSM_GROUPS = D_MODEL // SSM_GROUP
SSM_STATE = 64
DT_MIN = 1e-3
DT_MAX = 1e-1
HEAD_DIM = 64
V_DIM = 2 * HEAD_DIM
N_HEADS = D_MODEL // (2 * HEAD_DIM)
QK_WIDTH = N_HEADS * 2 * HEAD_DIM
V_WIDTH = N_HEADS * V_DIM
D_FF = 4 * D_MODEL
NUM_BUCKETS = 32
MAX_DISTANCE = 128
Q_BLOCK = 128
NORM_EPS = 1e-6
NEG_INF = -1e30

kernel_name = "yoco_s5_diffattn_hybrid"


def _rms_norm(x, gain):
    xf = x.astype(jnp.float32)
    y = xf * lax.rsqrt(jnp.mean(xf * xf, axis=-1, keepdims=True) + NORM_EPS)
    return (y * gain.astype(jnp.float32)).astype(x.dtype)


def _sq_relu_mlp(h, w_up, w_down):
    a = jax.nn.relu(h @ w_up)
    return (a * a) @ w_down


def _complex_linear_combine(left, right):
    a_re1, a_im1, b_re1, b_im1 = left
    a_re2, a_im2, b_re2, b_im2 = right
    a_re = a_re1 * a_re2 - a_im1 * a_im2
    a_im = a_re1 * a_im2 + a_im1 * a_re2
    b_re = a_re2 * b_re1 - a_im2 * b_im1 + b_re2
    b_im = a_re2 * b_im1 + a_im2 * b_re1 + b_im2
    return (a_re, a_im, b_re, b_im)


def _s5_mixer(h, w_in, a_re, a_im, log_dt, b_re, b_im, c_re, c_im, d_skip, w_glu):
    bsz, seqlen, _ = h.shape
    u = (h @ w_in).astype(jnp.float32).reshape(bsz, seqlen, SSM_GROUPS, SSM_GROUP)
    dt = jnp.exp(log_dt.astype(jnp.float32))[:, None]
    lr = a_re.astype(jnp.float32)
    li = a_im.astype(jnp.float32)
    mag = jnp.exp(lr * dt)
    ab_re = mag * jnp.cos(li * dt)
    ab_im = mag * jnp.sin(li * dt)
    den = lr * lr + li * li
    coef_re = ((ab_re - 1.0) * lr + ab_im * li) / den
    coef_im = (ab_im * lr - (ab_re - 1.0) * li) / den
    br = b_re.astype(jnp.float32)
    bi = b_im.astype(jnp.float32)
    bb_re = coef_re[..., None] * br - coef_im[..., None] * bi
    bb_im = coef_re[..., None] * bi + coef_im[..., None] * br
    bu_re = jnp.einsum('blgc,gpc->blgp', u, bb_re)
    bu_im = jnp.einsum('blgc,gpc->blgp', u, bb_im)
    a_seq_re = jnp.broadcast_to(ab_re, (1, seqlen, SSM_GROUPS, SSM_STATE))
    a_seq_im = jnp.broadcast_to(ab_im, (1, seqlen, SSM_GROUPS, SSM_STATE))
    _, _, s_re, s_im = lax.associative_scan(
        _complex_linear_combine, (a_seq_re, a_seq_im, bu_re, bu_im), axis=1)
    y = (jnp.einsum('blgp,gcp->blgc', s_re, c_re.astype(jnp.float32))
         - jnp.einsum('blgp,gcp->blgc', s_im, c_im.astype(jnp.float32)))
    y = y + d_skip.astype(jnp.float32).reshape(SSM_GROUPS, SSM_GROUP) * u
    y = jax.nn.gelu(y.reshape(bsz, seqlen, D_MODEL)).astype(h.dtype)
    val, gate = jnp.split(y @ w_glu, 2, axis=-1)
    return val * jax.nn.sigmoid(gate)


def _shared_kv(x, kv_norm, w_kv):
    bsz, seqlen, _ = x.shape
    kv = _rms_norm(x, kv_norm) @ w_kv
    k = kv[..., :QK_WIDTH].reshape(bsz, seqlen, N_HEADS, 2, HEAD_DIM)
    v = kv[..., QK_WIDTH:].reshape(bsz, seqlen, N_HEADS, V_DIM)
    k1 = jnp.transpose(k[..., 0, :], (0, 2, 1, 3))
    k2 = jnp.transpose(k[..., 1, :], (0, 2, 1, 3))
    v = jnp.transpose(v, (0, 2, 1, 3))
    return k1, k2, v


def _t5_bucket(rel):
    n = jnp.maximum(rel, 0)
    max_exact = NUM_BUCKETS // 2
    large = max_exact + (jnp.log(jnp.maximum(n, max_exact).astype(jnp.float32) / max_exact)
                         / math.log(MAX_DISTANCE / max_exact)
                         * (NUM_BUCKETS - max_exact)).astype(jnp.int32)
    large = jnp.minimum(large, NUM_BUCKETS - 1)
    return jnp.where(n < max_exact, n, large)


def _block_probs(q, k, bias, causal, scale):
    s = jnp.einsum('bhqd,bhkd->bhqk', q, k).astype(jnp.float32) * scale + bias
    return jax.nn.softmax(jnp.where(causal, s, NEG_INF), axis=-1)


def _diff_attention_mixer(h, k1, k2, v, w_q, lq1, lk1, lq2, lk2, head_norm, w_o,
                          rel_bias, lam_init):
    bsz, seqlen, _ = h.shape
    q = (h @ w_q).reshape(bsz, seqlen, N_HEADS, 2, HEAD_DIM)
    q1 = jnp.transpose(q[..., 0, :], (0, 2, 1, 3))
    q2 = jnp.transpose(q[..., 1, :], (0, 2, 1, 3))
    lam = (jnp.exp(jnp.sum(lq1.astype(jnp.float32) * lk1.astype(jnp.float32)))
           - jnp.exp(jnp.sum(lq2.astype(jnp.float32) * lk2.astype(jnp.float32)))
           + lam_init)
    scale = HEAD_DIM ** -0.5
    table = rel_bias.astype(jnp.float32)
    blocks = []
    for start in range(0, seqlen, Q_BLOCK):
        end = start + Q_BLOCK
        rel = jnp.arange(start, end)[:, None] - jnp.arange(end)[None, :]
        causal = rel >= 0
        bias = jnp.transpose(table[_t5_bucket(rel)], (2, 0, 1))
        p1 = _block_probs(q1[:, :, start:end], k1[:, :, :end], bias, causal, scale)
        p2 = _block_probs(q2[:, :, start:end], k2[:, :, :end], bias, causal, scale)
        w = p1 - lam * p2
        blocks.append(jnp.einsum('bhqk,bhkv->bhqv', w, v[:, :, :end].astype(jnp.float32)))
    o = jnp.concatenate(blocks, axis=2)
    o = o * lax.rsqrt(jnp.mean(o * o, axis=-1, keepdims=True) + NORM_EPS)
    o = o * head_norm.astype(jnp.float32) * (1.0 - lam_init)
    o = jnp.transpose(o, (0, 2, 1, 3)).reshape(bsz, seqlen, V_WIDTH).astype(h.dtype)
    return o @ w_o


def setup_inputs(seed: int = 0) -> dict:
    key = jax.random.key(seed)
    ks = jax.random.split(key, 32)
    f32 = jnp.float32

    def nrm(k, shape, scale):
        return jax.random.normal(k, shape, f32) * scale

    def gain(k, shape):
        return 1.0 + nrm(k, shape, 0.05)

    na, nb = N_SSM_LAYERS, N_ATTN_LAYERS
    x = nrm(ks[0], (BATCH, SEQ, D_MODEL), 1.0)
    norm_mixer_pre = gain(ks[1], (DEPTH, D_MODEL))
    norm_mixer_post = gain(ks[2], (DEPTH, D_MODEL))
    norm_mlp_pre = gain(ks[3], (DEPTH, D_MODEL))
    norm_mlp_post = gain(ks[4], (DEPTH, D_MODEL))
    mlp_w_up = nrm(ks[5], (DEPTH, D_MODEL, D_FF), D_MODEL ** -0.5)
    mlp_w_down = nrm(ks[6], (DEPTH, D_FF, D_MODEL), D_FF ** -0.5)
    ssm_w_in = nrm(ks[7], (na, D_MODEL, D_MODEL), D_MODEL ** -0.5)
    ssm_a_re = -0.5 + nrm(ks[8], (na, SSM_GROUPS, SSM_STATE), 0.01)
    ssm_a_im = (math.pi * jnp.arange(SSM_STATE, dtype=f32))[None, None, :] \
        + nrm(ks[9], (na, SSM_GROUPS, SSM_STATE), 0.01)
    ssm_log_dt = jax.random.uniform(ks[10], (na, SSM_GROUPS), f32,
                                    minval=math.log(DT_MIN), maxval=math.log(DT_MAX))
    ssm_b_re = nrm(ks[11], (na, SSM_GROUPS, SSM_STATE, SSM_GROUP), SSM_GROUP ** -0.5)
    ssm_b_im = nrm(ks[12], (na, SSM_GROUPS, SSM_STATE, SSM_GROUP), SSM_GROUP ** -0.5)
    ssm_c_re = nrm(ks[13], (na, SSM_GROUPS, SSM_GROUP, SSM_STATE), SSM_STATE ** -0.5)
    ssm_c_im = nrm(ks[14], (na, SSM_GROUPS, SSM_GROUP, SSM_STATE), SSM_STATE ** -0.5)
    ssm_d = nrm(ks[15], (na, D_MODEL), 1.0)
    ssm_w_glu = nrm(ks[16], (na, D_MODEL, 2 * D_MODEL), D_MODEL ** -0.5)
    kv_norm = gain(ks[17], (D_MODEL,))
    w_kv = nrm(ks[18], (D_MODEL, QK_WIDTH + V_WIDTH), D_MODEL ** -0.5)
    attn_w_q = nrm(ks[19], (nb, D_MODEL, QK_WIDTH), D_MODEL ** -0.5)
    attn_lambda_q1 = nrm(ks[20], (nb, HEAD_DIM), 0.1)
    attn_lambda_k1 = nrm(ks[21], (nb, HEAD_DIM), 0.1)
    attn_lambda_q2 = nrm(ks[22], (nb, HEAD_DIM), 0.1)
    attn_lambda_k2 = nrm(ks[23], (nb, HEAD_DIM), 0.1)
    attn_head_norm = gain(ks[24], (nb, V_DIM))
    attn_w_o = nrm(ks[25], (nb, V_WIDTH, D_MODEL), V_WIDTH ** -0.5)
    rel_bias = nrm(ks[26], (NUM_BUCKETS, N_HEADS), 0.3)
    return {"x": x,
            "norm_mixer_pre": norm_mixer_pre, "norm_mixer_post": norm_mixer_post,
            "norm_mlp_pre": norm_mlp_pre, "norm_mlp_post": norm_mlp_post,
            "mlp_w_up": mlp_w_up, "mlp_w_down": mlp_w_down,
            "ssm_w_in": ssm_w_in, "ssm_a_re": ssm_a_re, "ssm_a_im": ssm_a_im,
            "ssm_log_dt": ssm_log_dt, "ssm_b_re": ssm_b_re, "ssm_b_im": ssm_b_im,
            "ssm_c_re": ssm_c_re, "ssm_c_im": ssm_c_im, "ssm_d": ssm_d,
            "ssm_w_glu": ssm_w_glu,
            "kv_norm": kv_norm, "w_kv": w_kv,
            "attn_w_q": attn_w_q, "attn_lambda_q1": attn_lambda_q1,
            "attn_lambda_k1": attn_lambda_k1, "attn_lambda_q2": attn_lambda_q2,
            "attn_lambda_k2": attn_lambda_k2, "attn_head_norm": attn_head_norm,
            "attn_w_o": attn_w_o, "rel_bias": rel_bias}


def reference(x, norm_mixer_pre, norm_mixer_post, norm_mlp_pre, norm_mlp_post,
              mlp_w_up, mlp_w_down,
              ssm_w_in, ssm_a_re, ssm_a_im, ssm_log_dt, ssm_b_re, ssm_b_im,
              ssm_c_re, ssm_c_im, ssm_d, ssm_w_glu,
              kv_norm, w_kv,
              attn_w_q, attn_lambda_q1, attn_lambda_k1, attn_lambda_q2, attn_lambda_k2,
              attn_head_norm, attn_w_o, rel_bias):
    k1 = k2 = v = None
    for layer in range(DEPTH):
        h = _rms_norm(x, norm_mixer_pre[layer])
        if layer < N_SSM_LAYERS:
            i = layer
            mix = _s5_mixer(h, ssm_w_in[i], ssm_a_re[i], ssm_a_im[i], ssm_log_dt[i],
                            ssm_b_re[i], ssm_b_im[i], ssm_c_re[i], ssm_c_im[i],
                            ssm_d[i], ssm_w_glu[i])
        else:
            if layer == N_SSM_LAYERS:
                k1, k2, v = _shared_kv(x, kv_norm, w_kv)
            j = layer - N_SSM_LAYERS
            lam_init = 0.8 - 0.6 * math.exp(-0.3 * layer)
            mix = _diff_attention_mixer(h, k1, k2, v, attn_w_q[j],
                                        attn_lambda_q1[j], attn_lambda_k1[j],
                                        attn_lambda_q2[j], attn_lambda_k2[j],
                                        attn_head_norm[j], attn_w_o[j], rel_bias, lam_init)
        x = x + _rms_norm(mix, norm_mixer_post[layer])
        h = _rms_norm(x, norm_mlp_pre[layer])
        x = x + _rms_norm(_sq_relu_mlp(h, mlp_w_up[layer], mlp_w_down[layer]),
                          norm_mlp_post[layer])
    return x
```

```python
import functools
import math

import numpy as np
import jax
import jax.numpy as jnp
from jax import lax
from jax.experimental import pallas as pl
from jax.experimental.pallas import tpu as pltpu

F32 = jnp.float32
BF16 = jnp.bfloat16

SSM_GROUP = 16
SSM_STATE = 64
HEAD_DIM = 64
V_DIM = 2 * HEAD_DIM
NUM_BUCKETS = 32
MAX_DISTANCE = 128
NORM_EPS = 1e-6
NEG_INF = -1e30

MXU_WIDTH = 256
ROW_TILE = 512
SCAN_CHUNK = 8
SCAN_ROWS = 512
ATTN_TILE = 512
FF_CHUNK = 1024
VMEM_LIMIT = 56 * 1024 * 1024


def _params(*sem):
    return pltpu.CompilerParams(dimension_semantics=sem, vmem_limit_bytes=VMEM_LIMIT)


def _rms(xf, gain):
    return xf * lax.rsqrt(jnp.mean(xf * xf, axis=-1, keepdims=True) + NORM_EPS) * gain


def _const_spec(shape):
    zeros = (0,) * len(shape)
    return pl.BlockSpec(shape, lambda *_: zeros)


def _proj_kernel(x_ref, g_ref, w_ref, *out_refs, mode, scale):
    hn = _rms(x_ref[...], g_ref[...]).astype(BF16)
    acc = jnp.dot(hn, w_ref[...], preferred_element_type=F32)
    rows = acc.shape[0]
    if mode == "colblock":
        (o_ref,) = out_refs
        for cb in range(o_ref.shape[0]):
            o_ref[cb] = acc[:, cb * MXU_WIDTH:(cb + 1) * MXU_WIDTH].astype(BF16)
    elif mode == "qT":
        (o_ref,) = out_refs
        o_ref[0] = (acc * scale).T.astype(BF16)
    else:
        k_ref, vt_ref = out_refs
        d = k_ref.shape[1]
        k_ref[...] = acc[:, :d].astype(BF16)
        vt = acc[:, d:].T.astype(BF16)
        vt_ref[0, :, 0] = vt.reshape(d // V_DIM, V_DIM, rows)


def _norm_proj(x, gain, w, *, mode, batch, seq, scale=1.0):
    n, d = x.shape
    nout = w.shape[1]
    tm = ROW_TILE
    per_b = seq // tm
    if mode == "colblock":
        ncb = nout // MXU_WIDTH
        out_shape = jax.ShapeDtypeStruct((ncb, n, MXU_WIDTH), BF16)
        out_specs = pl.BlockSpec((ncb, tm, MXU_WIDTH), lambda i: (0, i, 0))
    elif mode == "qT":
        out_shape = jax.ShapeDtypeStruct((batch, nout, seq), BF16)
        out_specs = pl.BlockSpec((1, nout, tm), lambda i: (i // per_b, 0, i % per_b))
    else:
        heads = d // V_DIM
        assert tm == ATTN_TILE
        out_shape = (jax.ShapeDtypeStruct((n, d), BF16),
                     jax.ShapeDtypeStruct((batch, heads, per_b, V_DIM, tm), BF16))
        out_specs = (pl.BlockSpec((tm, d), lambda i: (i, 0)),
                     pl.BlockSpec((1, heads, 1, V_DIM, tm),
                                  lambda i: (i // per_b, 0, i % per_b, 0, 0)))
    return pl.pallas_call(
        functools.partial(_proj_kernel, mode=mode, scale=scale),
        grid=(n // tm,),
        in_specs=[pl.BlockSpec((tm, d), lambda i: (i, 0)),
                  _const_spec((1, d)),
                  _const_spec((d, nout))],
        out_specs=out_specs,
        out_shape=out_shape,
        compiler_params=_params("parallel"),
        name="norm_proj_" + mode,
    )(x, gain.reshape(1, d), w)


def _out_kernel(y_ref, w_ref, x_ref, g_ref, o_ref, *, glu, chunked):
    if chunked:
        acc = jnp.dot(y_ref[0], w_ref[0], preferred_element_type=F32)
        for cb in range(1, y_ref.shape[0]):
            acc = acc + jnp.dot(y_ref[cb], w_ref[cb], preferred_element_type=F32)
    else:
        acc = jnp.dot(y_ref[...], w_ref[...], preferred_element_type=F32)
    if glu:
        d = o_ref.shape[1]
        acc = acc[:, :d] * jax.nn.sigmoid(acc[:, d:])
    o_ref[...] = x_ref[...] + _rms(acc, g_ref[...])


def _out_proj(y, w, x, gain, *, glu, chunked):
    n, d = x.shape
    tm = ROW_TILE
    if chunked:
        ncb = y.shape[0]
        w = w.reshape(ncb, MXU_WIDTH, w.shape[1])
        y_spec = pl.BlockSpec((ncb, tm, MXU_WIDTH), lambda i: (0, i, 0))
    else:
        y_spec = pl.BlockSpec((tm, y.shape[1]), lambda i: (i, 0))
    return pl.pallas_call(
        functools.partial(_out_kernel, glu=glu, chunked=chunked),
        grid=(n // tm,),
        in_specs=[y_spec, _const_spec(w.shape),
                  pl.BlockSpec((tm, d), lambda i: (i, 0)), _const_spec((1, d))],
        out_specs=pl.BlockSpec((tm, d), lambda i: (i, 0)),
        out_shape=jax.ShapeDtypeStruct((n, d), F32),
        compiler_params=_params("parallel"),
        name="out_proj_glu" if glu else "out_proj",
    )(y, w, x, gain.reshape(1, d))


def _mlp_kernel(x_ref, g1_ref, wu_ref, wd_ref, g2_ref, o_ref):
    x = x_ref[...]
    hn = _rms(x, g1_ref[...]).astype(BF16)
    acc = None
    for c in range(wu_ref.shape[1] // FF_CHUNK):
        a = jnp.dot(hn, wu_ref[:, c * FF_CHUNK:(c + 1) * FF_CHUNK], preferred_element_type=F32)
        a = jnp.maximum(a, 0.0)
        a = (a * a).astype(BF16)
        part = jnp.dot(a, wd_ref[c * FF_CHUNK:(c + 1) * FF_CHUNK, :], preferred_element_type=F32)
        acc = part if acc is None else acc + part
    o_ref[...] = x + _rms(acc, g2_ref[...])


def _mlp(x, g_pre, w_up, w_down, g_post):
    n, d = x.shape
    dff = w_up.shape[1]
    tm = ROW_TILE
    return pl.pallas_call(
        _mlp_kernel,
        grid=(n // tm,),
        in_specs=[pl.BlockSpec((tm, d), lambda i: (i, 0)), _const_spec((1, d)),
                  _const_spec((d, dff)), _const_spec((dff, d)), _const_spec((1, d))],
        out_specs=pl.BlockSpec((tm, d), lambda i: (i, 0)),
        out_shape=jax.ShapeDtypeStruct((n, d), F32),
        compiler_params=_params("parallel"),
        name="mlp",
    )(x, g_pre.reshape(1, d), w_up, w_down, g_post.reshape(1, d))


def _s5_kernel(u_ref, w_ref, k_ref, p_ref, apr_ref, api_ref, d_ref, o_ref, v_ref, sr_ref, si_ref):
    cw = MXU_WIDTH
    ns = v_ref.shape[1] // 2

    @pl.when(pl.program_id(2) == 0)
    def _():
        sr_ref[...] = jnp.zeros_like(sr_ref)
        si_ref[...] = jnp.zeros_like(si_ref)

    x = u_ref[0]
    v_ref[...] = jnp.dot(x, w_ref[0], preferred_element_type=F32)

    apr = apr_ref[0]
    api = api_ref[0]
    row = lax.broadcasted_iota(jnp.int32, (8, ns), 0)

    def shift(z, k, fill):
        return jnp.where(row >= k, pltpu.roll(z, k, 0), fill)

    def block(i, carry):
        sr, si = carry
        r0 = pl.multiple_of(i * 8, 8)
        zr = v_ref[pl.ds(r0, 8), 0:ns]
        zi = v_ref[pl.ds(r0, 8), ns:2 * ns]
        for k in (1, 2, 4):
            ar, ai = apr[k - 1:k], api[k - 1:k]
            hr, hi = shift(zr, k, 0.0), shift(zi, k, 0.0)
            zr, zi = zr + ar * hr - ai * hi, zi + ar * hi + ai * hr
        fr = zr + apr * sr - api * si
        fi = zi + apr * si + api * sr
        v_ref[pl.ds(r0, 8), 0:ns] = shift(fr, 1, sr)
        v_ref[pl.ds(r0, 8), ns:2 * ns] = shift(fi, 1, si)
        return fr[7:8], fi[7:8]

    sr, si = lax.fori_loop(0, v_ref.shape[0] // 8, block, (sr_ref[...], si_ref[...]))
    sr_ref[...] = sr
    si_ref[...] = si

    sp = v_ref[...].astype(BF16)
    steps = x.shape[1] // cw
    for t in range(steps):
        acc = jnp.dot(x[:, :(t + 1) * cw], k_ref[0, (steps - 1 - t) * cw:, :],
                      preferred_element_type=F32)
        acc = acc + jnp.dot(sp, p_ref[0, t], preferred_element_type=F32)
        y = acc + d_ref[0] * x[:, t * cw:(t + 1) * cw].astype(F32)
        o_ref[0, :, t * cw:(t + 1) * cw] = jax.nn.gelu(y).astype(BF16)


def _s5_core(u_cb, ops, d_skip, *, batch, seq):
    wstack, kstack, pstack, apr, api = ops
    ncb, n, cw = u_cb.shape
    L = SCAN_CHUNK
    rows_per_b = seq // L
    halves = rows_per_b // SCAN_ROWS
    u_v = u_cb.reshape(ncb, n // L, L * cw)
    ns2 = wstack.shape[2]
    blk = pl.BlockSpec((1, SCAN_ROWS, L * cw), lambda cb, b, h: (cb, b * halves + h, 0))
    out = pl.pallas_call(
        _s5_kernel,
        grid=(ncb, batch, halves),
        in_specs=[blk,
                  pl.BlockSpec((1, L * cw, ns2), lambda cb, b, h: (cb, 0, 0)),
                  pl.BlockSpec((1, L * cw, cw), lambda cb, b, h: (cb, 0, 0)),
                  pl.BlockSpec((1, L, ns2, cw), lambda cb, b, h: (cb, 0, 0, 0)),
                  pl.BlockSpec((1, 8, ns2 // 2), lambda cb, b, h: (cb, 0, 0)),
                  pl.BlockSpec((1, 8, ns2 // 2), lambda cb, b, h: (cb, 0, 0)),
                  pl.BlockSpec((1, 1, cw), lambda cb, b, h: (cb, 0, 0))],
        out_specs=blk,
        out_shape=jax.ShapeDtypeStruct(u_v.shape, BF16),
        scratch_shapes=[pltpu.VMEM((SCAN_ROWS, ns2), F32),
                        pltpu.VMEM((1, ns2 // 2), F32),
                        pltpu.VMEM((1, ns2 // 2), F32)],
        compiler_params=_params("arbitrary", "arbitrary", "arbitrary"),
        name="s5_core",
    )(u_v, wstack, kstack, pstack, apr, api, d_skip.reshape(ncb, 1, cw))
    return out.reshape(ncb, n, cw)


def _block_diag(t):
    g, a, b = t.shape[-3:]
    eye = jnp.eye(g, dtype=t.dtype)
    out = jnp.einsum("...gab,gh->...gahb", t, eye)
    return out.reshape(t.shape[:-3] + (g * a, g * b))


def _s5_operators(a_re, a_im, log_dt, b_re, b_im, c_re, c_im):
    L = SCAN_CHUNK
    groups, p = a_re.shape
    gl = MXU_WIDTH // SSM_GROUP
    ncb = groups // gl
    dt = jnp.exp(log_dt.astype(F32))[:, None]
    lr, li = a_re.astype(F32), a_im.astype(F32)
    mag = jnp.exp(lr * dt)
    ab_re = mag * jnp.cos(li * dt)
    ab_im = mag * jnp.sin(li * dt)
    den = lr * lr + li * li
    coef_re = ((ab_re - 1.0) * lr + ab_im * li) / den
    coef_im = (ab_im * lr - (ab_re - 1.0) * li) / den
    br, bi = b_re.astype(F32), b_im.astype(F32)
    bb_re = coef_re[..., None] * br - coef_im[..., None] * bi
    bb_im = coef_re[..., None] * bi + coef_im[..., None] * br
    cr, ci = c_re.astype(F32), c_im.astype(F32)

    def apow(m):
        m = jnp.asarray(m, F32)[:, None, None]
        mg = jnp.exp(m * (lr * dt))
        return mg * jnp.cos(m * (li * dt)), mg * jnp.sin(m * (li * dt))

    pr, pi = apow(np.arange(L))
    ca_re = cr[None] * pr[:, :, None, :] - ci[None] * pi[:, :, None, :]
    ca_im = cr[None] * pi[:, :, None, :] + ci[None] * pr[:, :, None, :]
    km = jnp.einsum("mgap,gpc->mgca", ca_re, bb_re) - jnp.einsum("mgap,gpc->mgca", ca_im, bb_im)
    km = km[::-1].reshape(L, ncb, gl, SSM_GROUP, SSM_GROUP)
    kstack = _block_diag(km).transpose(1, 0, 2, 3).reshape(ncb, L * MXU_WIDTH, MXU_WIDTH)

    pr, pi = apow(np.arange(L - 1, -1, -1))
    w_re = pr[..., None] * bb_re[None] - pi[..., None] * bb_im[None]
    w_im = pr[..., None] * bb_im[None] + pi[..., None] * bb_re[None]

    def to_w(w):
        w = w.transpose(0, 1, 3, 2).reshape(L, ncb, gl, SSM_GROUP, p)
        return _block_diag(w)

    wstack = jnp.concatenate([to_w(w_re), to_w(w_im)], axis=-1)
    wstack = wstack.transpose(1, 0, 2, 3).reshape(ncb, L * MXU_WIDTH, 2 * gl * p)

    pr, pi = apow(np.arange(1, L + 1))
    q_re = cr[None] * pr[:, :, None, :] - ci[None] * pi[:, :, None, :]
    q_im = cr[None] * pi[:, :, None, :] + ci[None] * pr[:, :, None, :]

    def to_p(q):
        q = q.transpose(0, 1, 3, 2).reshape(L, ncb, gl, p, SSM_GROUP)
        return _block_diag(q)

    pstack = jnp.concatenate([to_p(q_re), to_p(-q_im)], axis=-2).transpose(1, 0, 2, 3)

    pr, pi = apow(L * np.arange(1, 9))
    apr = pr.reshape(8, ncb, gl * p).transpose(1, 0, 2)
    api = pi.reshape(8, ncb, gl * p).transpose(1, 0, 2)
    return wstack.astype(BF16), kstack.astype(BF16), pstack.astype(BF16), apr, api


def _bucket_tiles(t):
    i = np.arange(t)[:, None]
    j = np.arange(t)[None, :]
    tiles = []
    for off in (0, t):
        rel = j + off - i
        n = np.maximum(rel, 0)
        max_exact = NUM_BUCKETS // 2
        large = max_exact + (np.log(np.maximum(n, max_exact).astype(np.float32) / max_exact)
                             / math.log(MAX_DISTANCE / max_exact)
                             * (NUM_BUCKETS - max_exact)).astype(np.int32)
        large = np.minimum(large, NUM_BUCKETS - 1)
        b = np.where(n < max_exact, n, large)
        tiles.append(np.where(rel >= 0, b, -1).astype(np.int32))
    return np.stack(tiles)


def _bias_kernel(bkt_ref, tab_ref, o_ref):
    h = pl.program_id(0)
    far = tab_ref[NUM_BUCKETS - 1, h]
    for typ in range(2):
        b = bkt_ref[typ]
        acc = jnp.zeros(b.shape, F32)
        for i in range(NUM_BUCKETS - 1):
            acc = jnp.where(b == i, tab_ref[i, h] - far, acc)
        o_ref[0, typ] = jnp.where(b < 0, NEG_INF, acc)


def _bias_tiles(rel_bias, t):
    heads = rel_bias.shape[1]
    return pl.pallas_call(
        _bias_kernel,
        grid=(heads,),
        in_specs=[_const_spec((2, t, t)), pl.BlockSpec(memory_space=pltpu.SMEM)],
        out_specs=pl.BlockSpec((1, 2, t, t), lambda h: (h, 0, 0, 0)),
        out_shape=jax.ShapeDtypeStruct((heads, 2, t, t), F32),
        compiler_params=_params("parallel"),
        name="rel_bias_tiles",
    )(jnp.asarray(_bucket_tiles(t)), rel_bias.astype(F32))


def _flash_kernel(q_ref, k_ref, vt_ref, bias_ref, lam_ref, hn_ref, o_ref, m_ref, l_ref, acc_ref,
                  *, lam_init):
    t = q_ref.shape[2]
    qi = pl.program_id(2)
    qt = q_ref[0]
    row = lax.broadcasted_iota(jnp.int32, qt.shape, 0)
    zero = jnp.zeros_like(qt)
    qz = (jnp.where(row < HEAD_DIM, qt, zero), jnp.where(row >= HEAD_DIM, qt, zero))

    m_ref[...] = jnp.full(m_ref.shape, NEG_INF, F32)
    l_ref[...] = jnp.zeros_like(l_ref)
    acc_ref[...] = jnp.zeros_like(acc_ref)

    def step(j, bias):
        kblk = k_ref[0, pl.ds(pl.multiple_of(j * t, t), t), :]
        vblk = vt_ref[0, 0, j]
        for br in range(2):
            s = jnp.dot(kblk, qz[br], preferred_element_type=F32)
            if bias is not None:
                s = s + bias
            m_old = m_ref[br]
            m_new = jnp.maximum(m_old, jnp.max(s, axis=0, keepdims=True))
            alpha = jnp.exp(m_old - m_new)
            p = jnp.exp(s - m_new)
            l_ref[br] = alpha * l_ref[br] + jnp.sum(p, axis=0, keepdims=True)
            acc_ref[br] = alpha * acc_ref[br] + jnp.dot(vblk, p.astype(BF16),
                                                        preferred_element_type=F32)
            m_ref[br] = m_new

    def far(j, carry):
        step(j, None)
        return carry

    lax.fori_loop(0, jnp.maximum(qi - 1, 0), far, 0)

    @pl.when(qi >= 1)
    def _():
        step(qi - 1, bias_ref[0, 1])

    step(qi, bias_ref[0, 0])

    lv = lam_ref[...]
    lam = (jnp.exp(jnp.sum(lv[0:1] * lv[1:2], axis=1, keepdims=True))
           - jnp.exp(jnp.sum(lv[2:3] * lv[3:4], axis=1, keepdims=True)) + lam_init)
    o1 = acc_ref[0] * (1.0 / l_ref[0])
    o2 = acc_ref[1] * (1.0 / l_ref[1])
    o = o1 - lam * o2
    o = o * lax.rsqrt(jnp.mean(o * o, axis=0, keepdims=True) + NORM_EPS)
    o = o * hn_ref[...] * (1.0 - lam_init)
    o_ref[0] = o.T.astype(BF16)


def _diff_attention(qt, k, vt, bias, lam_vecs, head_norm, *, lam_init):
    batch, d, seq = qt.shape
    heads = d // V_DIM
    t = ATTN_TILE
    nblk = seq // t
    lam_pad = jnp.zeros((8, V_DIM), F32).at[:4, :HEAD_DIM].set(lam_vecs.astype(F32))
    return pl.pallas_call(
        functools.partial(_flash_kernel, lam_init=lam_init),
        grid=(batch, heads, nblk),
        in_specs=[pl.BlockSpec((1, V_DIM, t), lambda b, h, i: (b, h, i)),
                  pl.BlockSpec((1, seq, V_DIM), lambda b, h, i: (b, 0, h)),
                  pl.BlockSpec((1, 1, nblk, V_DIM, t), lambda b, h, i: (b, h, 0, 0, 0)),
                  pl.BlockSpec((1, 2, t, t), lambda b, h, i: (h, 0, 0, 0)),
                  _const_spec((8, V_DIM)),
                  _const_spec((V_DIM, 1))],
        out_specs=pl.BlockSpec((1, t, V_DIM), lambda b, h, i: (b, i, h)),
        out_shape=jax.ShapeDtypeStruct((batch, seq, d), BF16),
        scratch_shapes=[pltpu.VMEM((2, 1, t), F32), pltpu.VMEM((2, 1, t), F32),
                        pltpu.VMEM((2, V_DIM, t), F32)],
        compiler_params=_params("parallel", "parallel", "arbitrary"),
        name="diff_attention",
    )(qt, k.reshape(batch, seq, d), vt, bias, lam_pad, head_norm.astype(F32).reshape(V_DIM, 1))


def kernel(x, norm_mixer_pre, norm_mixer_post, norm_mlp_pre, norm_mlp_post, mlp_w_up, mlp_w_down,
           ssm_w_in, ssm_a_re, ssm_a_im, ssm_log_dt, ssm_b_re, ssm_b_im, ssm_c_re, ssm_c_im, ssm_d,
           ssm_w_glu, kv_norm, w_kv, attn_w_q, attn_lambda_q1, attn_lambda_k1, attn_lambda_q2,
           attn_lambda_k2, attn_head_norm, attn_w_o, rel_bias):
    batch, seq, d = x.shape
    depth = norm_mixer_pre.shape[0]
    n_ssm = ssm_w_in.shape[0]
    xs = x.reshape(batch * seq, d).astype(F32)
    bias = k = vt = None
    for layer in range(depth):
        if layer < n_ssm:
            i = layer
            ops = _s5_operators(ssm_a_re[i], ssm_a_im[i], ssm_log_dt[i], ssm_b_re[i], ssm_b_im[i],
                                ssm_c_re[i], ssm_c_im[i])
            u = _norm_proj(xs, norm_mixer_pre[layer], ssm_w_in[i].astype(BF16),
                           mode="colblock", batch=batch, seq=seq)
            yg = _s5_core(u, ops, ssm_d[i].astype(F32), batch=batch, seq=seq)
            xs = _out_proj(yg, ssm_w_glu[i].astype(BF16), xs, norm_mixer_post[layer],
                           glu=True, chunked=True)
        else:
            if layer == n_ssm:
                k, vt = _norm_proj(xs, kv_norm, w_kv.astype(BF16), mode="kv", batch=batch, seq=seq)
                bias = _bias_tiles(rel_bias, ATTN_TILE)
            j = layer - n_ssm
            lam_init = 0.8 - 0.6 * math.exp(-0.3 * layer)
            qt = _norm_proj(xs, norm_mixer_pre[layer], attn_w_q[j].astype(BF16), mode="qT",
                            batch=batch, seq=seq, scale=HEAD_DIM ** -0.5)
            lam_vecs = jnp.stack([attn_lambda_q1[j], attn_lambda_k1[j],
                                  attn_lambda_q2[j], attn_lambda_k2[j]])
            o = _diff_attention(qt, k, vt, bias, lam_vecs, attn_head_norm[j], lam_init=lam_init)
            xs = _out_proj(o.reshape(batch * seq, d), attn_w_o[j].astype(BF16), xs,
                           norm_mixer_post[layer], glu=False, chunked=False)
        xs = _mlp(xs, norm_mlp_pre[layer], mlp_w_up[layer].astype(BF16),
                  mlp_w_down[layer].astype(BF16), norm_mlp_post[layer])
    return xs.reshape(batch, seq, d).astype(x.dtype)
```

```python
import functools
import math

import numpy as np
import jax
import jax.numpy as jnp
from jax import lax
from jax.experimental import pallas as pl
from jax.experimental.pallas import tpu as pltpu

F32 = jnp.float32
BF16 = jnp.bfloat16

SSM_GROUP = 16
SSM_STATE = 64
HEAD_DIM = 64
V_DIM = 2 * HEAD_DIM
NUM_BUCKETS = 32
MAX_DISTANCE = 128
NORM_EPS = 1e-6
NEG_INF = -1e30
LOG2E = math.log2(math.e)

MXU_WIDTH = 256
ROW_TILE = 512
SCAN_CHUNK = 8
SCAN_ROWS = 512
ATTN_TILE = 512
FF_CHUNK = 1024
VMEM_LIMIT = 56 * 1024 * 1024


def _params(*sem):
    return pltpu.CompilerParams(dimension_semantics=sem, vmem_limit_bytes=VMEM_LIMIT)


def _rms(xf, gain):
    return xf * lax.rsqrt(jnp.mean(xf * xf, axis=-1, keepdims=True) + NORM_EPS) * gain


def _const_spec(shape):
    zeros = (0,) * len(shape)
    return pl.BlockSpec(shape, lambda *_: zeros)


def _proj_kernel(x_ref, g_ref, w_ref, *out_refs, mode, scale):
    hn = _rms(x_ref[...], g_ref[...]).astype(BF16)
    acc = jnp.dot(hn, w_ref[...], preferred_element_type=F32)
    rows = acc.shape[0]
    if mode == "colblock":
        (o_ref,) = out_refs
        for cb in range(o_ref.shape[0]):
            o_ref[cb] = acc[:, cb * MXU_WIDTH:(cb + 1) * MXU_WIDTH].astype(BF16)
    elif mode == "qT":
        (o_ref,) = out_refs
        o_ref[0] = (acc * scale).T.astype(BF16)
    else:
        k_ref, vt_ref = out_refs
        d = k_ref.shape[1]
        k_ref[...] = acc[:, :d].astype(BF16)
        vt = acc[:, d:].T.astype(BF16)
        vt_ref[0, :, 0] = vt.reshape(d // V_DIM, V_DIM, rows)


def _norm_proj(x, gain, w, *, mode, batch, seq, scale=1.0):
    n, d = x.shape
    nout = w.shape[1]
    tm = ROW_TILE
    per_b = seq // tm
    if mode == "colblock":
        ncb = nout // MXU_WIDTH
        out_shape = jax.ShapeDtypeStruct((ncb, n, MXU_WIDTH), BF16)
        out_specs = pl.BlockSpec((ncb, tm, MXU_WIDTH), lambda i: (0, i, 0))
    elif mode == "qT":
        out_shape = jax.ShapeDtypeStruct((batch, nout, seq), BF16)
        out_specs = pl.BlockSpec((1, nout, tm), lambda i: (i // per_b, 0, i % per_b))
    else:
        heads = d // V_DIM
        assert tm == ATTN_TILE
        out_shape = (jax.ShapeDtypeStruct((n, d), BF16),
                     jax.ShapeDtypeStruct((batch, heads, per_b, V_DIM, tm), BF16))
        out_specs = (pl.BlockSpec((tm, d), lambda i: (i, 0)),
                     pl.BlockSpec((1, heads, 1, V_DIM, tm),
                                  lambda i: (i // per_b, 0, i % per_b, 0, 0)))
    return pl.pallas_call(
        functools.partial(_proj_kernel, mode=mode, scale=scale),
        grid=(n // tm,),
        in_specs=[pl.BlockSpec((tm, d), lambda i: (i, 0)),
                  _const_spec((1, d)),
                  _const_spec((d, nout))],
        out_specs=out_specs,
        out_shape=out_shape,
        compiler_params=_params("parallel"),
        name="norm_proj_" + mode,
    )(x, gain.reshape(1, d), w)


def _out_kernel(y_ref, w_ref, x_ref, g_ref, o_ref, *, glu, chunked):
    if chunked:
        acc = jnp.dot(y_ref[0], w_ref[0], preferred_element_type=F32)
        for cb in range(1, y_ref.shape[0]):
            acc = acc + jnp.dot(y_ref[cb], w_ref[cb], preferred_element_type=F32)
    else:
        acc = jnp.dot(y_ref[...], w_ref[...], preferred_element_type=F32)
    if glu:
        d = o_ref.shape[1]
        acc = acc[:, :d] * jax.nn.sigmoid(acc[:, d:])
    o_ref[...] = x_ref[...] + _rms(acc, g_ref[...])


def _out_proj(y, w, x, gain, *, glu, chunked):
    n, d = x.shape
    tm = ROW_TILE
    if chunked:
        ncb = y.shape[0]
        w = w.reshape(ncb, MXU_WIDTH, w.shape[1])
        y_spec = pl.BlockSpec((ncb, tm, MXU_WIDTH), lambda i: (0, i, 0))
    else:
        y_spec = pl.BlockSpec((tm, y.shape[1]), lambda i: (i, 0))
    return pl.pallas_call(
        functools.partial(_out_kernel, glu=glu, chunked=chunked),
        grid=(n // tm,),
        in_specs=[y_spec, _const_spec(w.shape),
                  pl.BlockSpec((tm, d), lambda i: (i, 0)), _const_spec((1, d))],
        out_specs=pl.BlockSpec((tm, d), lambda i: (i, 0)),
        out_shape=jax.ShapeDtypeStruct((n, d), F32),
        compiler_params=_params("parallel"),
        name="out_proj_glu" if glu else "out_proj",
    )(y, w, x, gain.reshape(1, d))


def _mlp_kernel(x_ref, g1_ref, wu_ref, wd_ref, g2_ref, o_ref):
    x = x_ref[...]
    hn = _rms(x, g1_ref[...]).astype(BF16)
    acc = None
    for c in range(wu_ref.shape[1] // FF_CHUNK):
        a = jnp.dot(hn, wu_ref[:, c * FF_CHUNK:(c + 1) * FF_CHUNK], preferred_element_type=F32)
        a = jnp.maximum(a, 0.0)
        a = (a * a).astype(BF16)
        part = jnp.dot(a, wd_ref[c * FF_CHUNK:(c + 1) * FF_CHUNK, :], preferred_element_type=F32)
        acc = part if acc is None else acc + part
    o_ref[...] = x + _rms(acc, g2_ref[...])


def _mlp(x, g_pre, w_up, w_down, g_post):
    n, d = x.shape
    dff = w_up.shape[1]
    tm = ROW_TILE
    return pl.pallas_call(
        _mlp_kernel,
        grid=(n // tm,),
        in_specs=[pl.BlockSpec((tm, d), lambda i: (i, 0)), _const_spec((1, d)),
                  _const_spec((d, dff)), _const_spec((dff, d)), _const_spec((1, d))],
        out_specs=pl.BlockSpec((tm, d), lambda i: (i, 0)),
        out_shape=jax.ShapeDtypeStruct((n, d), F32),
        compiler_params=_params("parallel"),
        name="mlp",
    )(x, g_pre.reshape(1, d), w_up, w_down, g_post.reshape(1, d))


def _s5_kernel(u_ref, w_ref, k_ref, p_ref, apr_ref, api_ref, d_ref, o_ref, v_ref, sr_ref, si_ref):
    cw = MXU_WIDTH
    ns = v_ref.shape[1] // 2

    @pl.when(pl.program_id(2) == 0)
    def _():
        sr_ref[...] = jnp.zeros_like(sr_ref)
        si_ref[...] = jnp.zeros_like(si_ref)

    x = u_ref[0]
    v_ref[...] = jnp.dot(x, w_ref[0], preferred_element_type=F32)

    apr = apr_ref[0]
    api = api_ref[0]
    row = lax.broadcasted_iota(jnp.int32, (8, ns), 0)

    def shift(z, k, fill):
        return jnp.where(row >= k, pltpu.roll(z, k, 0), fill)

    def block(i, carry):
        sr, si = carry
        r0 = pl.multiple_of(i * 8, 8)
        zr = v_ref[pl.ds(r0, 8), 0:ns]
        zi = v_ref[pl.ds(r0, 8), ns:2 * ns]
        for k in (1, 2, 4):
            ar, ai = apr[k - 1:k], api[k - 1:k]
            hr, hi = shift(zr, k, 0.0), shift(zi, k, 0.0)
            zr, zi = zr + ar * hr - ai * hi, zi + ar * hi + ai * hr
        fr = zr + apr * sr - api * si
        fi = zi + apr * si + api * sr
        v_ref[pl.ds(r0, 8), 0:ns] = shift(fr, 1, sr)
        v_ref[pl.ds(r0, 8), ns:2 * ns] = shift(fi, 1, si)
        return fr[7:8], fi[7:8]

    sr, si = lax.fori_loop(0, v_ref.shape[0] // 8, block, (sr_ref[...], si_ref[...]))
    sr_ref[...] = sr
    si_ref[...] = si

    sp = v_ref[...].astype(BF16)
    steps = x.shape[1] // cw
    for t in range(steps):
        acc = jnp.dot(x[:, :(t + 1) * cw], k_ref[0, (steps - 1 - t) * cw:, :],
                      preferred_element_type=F32)
        acc = acc + jnp.dot(sp, p_ref[0, t], preferred_element_type=F32)
        y = acc + d_ref[0] * x[:, t * cw:(t + 1) * cw].astype(F32)
        o_ref[0, :, t * cw:(t + 1) * cw] = jax.nn.gelu(y).astype(BF16)


def _s5_core(u_cb, ops, d_skip, *, batch, seq):
    wstack, kstack, pstack, apr, api = ops
    ncb, n, cw = u_cb.shape
    L = SCAN_CHUNK
    rows_per_b = seq // L
    halves = rows_per_b // SCAN_ROWS
    u_v = u_cb.reshape(ncb, n // L, L * cw)
    ns2 = wstack.shape[2]
    blk = pl.BlockSpec((1, SCAN_ROWS, L * cw), lambda cb, b, h: (cb, b * halves + h, 0))
    out = pl.pallas_call(
        _s5_kernel,
        grid=(ncb, batch, halves),
        in_specs=[blk,
                  pl.BlockSpec((1, L * cw, ns2), lambda cb, b, h: (cb, 0, 0)),
                  pl.BlockSpec((1, L * cw, cw), lambda cb, b, h: (cb, 0, 0)),
                  pl.BlockSpec((1, L, ns2, cw), lambda cb, b, h: (cb, 0, 0, 0)),
                  pl.BlockSpec((1, 8, ns2 // 2), lambda cb, b, h: (cb, 0, 0)),
                  pl.BlockSpec((1, 8, ns2 // 2), lambda cb, b, h: (cb, 0, 0)),
                  pl.BlockSpec((1, 1, cw), lambda cb, b, h: (cb, 0, 0))],
        out_specs=blk,
        out_shape=jax.ShapeDtypeStruct(u_v.shape, BF16),
        scratch_shapes=[pltpu.VMEM((SCAN_ROWS, ns2), F32),
                        pltpu.VMEM((1, ns2 // 2), F32),
                        pltpu.VMEM((1, ns2 // 2), F32)],
        compiler_params=_params("arbitrary", "arbitrary", "arbitrary"),
        name="s5_core",
    )(u_v, wstack, kstack, pstack, apr, api, d_skip.reshape(ncb, 1, cw))
    return out.reshape(ncb, n, cw)


def _block_diag(t):
    g, a, b = t.shape[-3:]
    eye = jnp.eye(g, dtype=t.dtype)
    out = jnp.einsum("...gab,gh->...gahb", t, eye)
    return out.reshape(t.shape[:-3] + (g * a, g * b))


def _s5_operators(a_re, a_im, log_dt, b_re, b_im, c_re, c_im):
    L = SCAN_CHUNK
    groups, p = a_re.shape
    gl = MXU_WIDTH // SSM_GROUP
    ncb = groups // gl
    dt = jnp.exp(log_dt.astype(F32))[:, None]
    lr, li = a_re.astype(F32), a_im.astype(F32)
    mag = jnp.exp(lr * dt)
    ab_re = mag * jnp.cos(li * dt)
    ab_im = mag * jnp.sin(li * dt)
    den = lr * lr + li * li
    coef_re = ((ab_re - 1.0) * lr + ab_im * li) / den
    coef_im = (ab_im * lr - (ab_re - 1.0) * li) / den
    br, bi = b_re.astype(F32), b_im.astype(F32)
    bb_re = coef_re[..., None] * br - coef_im[..., None] * bi
    bb_im = coef_re[..., None] * bi + coef_im[..., None] * br
    cr, ci = c_re.astype(F32), c_im.astype(F32)

    def apow(m):
        m = jnp.asarray(m, F32)[:, None, None]
        mg = jnp.exp(m * (lr * dt))
        return mg * jnp.cos(m * (li * dt)), mg * jnp.sin(m * (li * dt))

    pr, pi = apow(np.arange(L))
    ca_re = cr[None] * pr[:, :, None, :] - ci[None] * pi[:, :, None, :]
    ca_im = cr[None] * pi[:, :, None, :] + ci[None] * pr[:, :, None, :]
    km = jnp.einsum("mgap,gpc->mgca", ca_re, bb_re) - jnp.einsum("mgap,gpc->mgca", ca_im, bb_im)
    km = km[::-1].reshape(L, ncb, gl, SSM_GROUP, SSM_GROUP)
    kstack = _block_diag(km).transpose(1, 0, 2, 3).reshape(ncb, L * MXU_WIDTH, MXU_WIDTH)

    pr, pi = apow(np.arange(L - 1, -1, -1))
    w_re = pr[..., None] * bb_re[None] - pi[..., None] * bb_im[None]
    w_im = pr[..., None] * bb_im[None] + pi[..., None] * bb_re[None]

    def to_w(w):
        w = w.transpose(0, 1, 3, 2).reshape(L, ncb, gl, SSM_GROUP, p)
        return _block_diag(w)

    wstack = jnp.concatenate([to_w(w_re), to_w(w_im)], axis=-1)
    wstack = wstack.transpose(1, 0, 2, 3).reshape(ncb, L * MXU_WIDTH, 2 * gl * p)

    pr, pi = apow(np.arange(1, L + 1))
    q_re = cr[None] * pr[:, :, None, :] - ci[None] * pi[:, :, None, :]
    q_im = cr[None] * pi[:, :, None, :] + ci[None] * pr[:, :, None, :]

    def to_p(q):
        q = q.transpose(0, 1, 3, 2).reshape(L, ncb, gl, p, SSM_GROUP)
        return _block_diag(q)

    pstack = jnp.concatenate([to_p(q_re), to_p(-q_im)], axis=-2).transpose(1, 0, 2, 3)

    pr, pi = apow(L * np.arange(1, 9))
    apr = pr.reshape(8, ncb, gl * p).transpose(1, 0, 2)
    api = pi.reshape(8, ncb, gl * p).transpose(1, 0, 2)
    return wstack.astype(BF16), kstack.astype(BF16), pstack.astype(BF16), apr, api


def _bucket_tiles(t):
    i = np.arange(t)[:, None]
    j = np.arange(t)[None, :]
    tiles = []
    for off in (0, t):
        rel = j + off - i
        n = np.maximum(rel, 0)
        max_exact = NUM_BUCKETS // 2
        large = max_exact + (np.log(np.maximum(n, max_exact).astype(np.float32) / max_exact)
                             / math.log(MAX_DISTANCE / max_exact)
                             * (NUM_BUCKETS - max_exact)).astype(np.int32)
        large = np.minimum(large, NUM_BUCKETS - 1)
        b = np.where(n < max_exact, n, large)
        tiles.append(np.where(rel >= 0, b, -1).astype(np.int32))
    return np.stack(tiles)


def _bias_kernel(bkt_ref, tab_ref, o_ref):
    h = pl.program_id(0)
    far = tab_ref[NUM_BUCKETS - 1, h]
    for typ in range(2):
        b = bkt_ref[typ]
        acc = jnp.zeros(b.shape, F32)
        for i in range(NUM_BUCKETS - 1):
            acc = jnp.where(b == i, (tab_ref[i, h] - far) * LOG2E, acc)
        o_ref[0, typ] = jnp.where(b < 0, NEG_INF, acc)


def _bias_tiles(rel_bias, t):
    heads = rel_bias.shape[1]
    return pl.pallas_call(
        _bias_kernel,
        grid=(heads,),
        in_specs=[_const_spec((2, t, t)), pl.BlockSpec(memory_space=pltpu.SMEM)],
        out_specs=pl.BlockSpec((1, 2, t, t), lambda h: (h, 0, 0, 0)),
        out_shape=jax.ShapeDtypeStruct((heads, 2, t, t), F32),
        compiler_params=_params("parallel"),
        name="rel_bias_tiles",
    )(jnp.asarray(_bucket_tiles(t)), rel_bias.astype(F32))


def _flash_kernel(q_ref, k_ref, vt_ref, bias_ref, lam_ref, hn_ref, o_ref,
                  qcat_ref, s_ref, m_ref, l_ref, acc_ref, *, lam_init):
    t = q_ref.shape[2]
    cw = MXU_WIDTH
    per_branch = t // cw
    ncol = 2 * per_branch
    qi = pl.program_id(2)
    qt = q_ref[0]
    row = lax.broadcasted_iota(jnp.int32, qt.shape, 0)
    zero = jnp.zeros_like(qt)
    qcat_ref[:, :t] = jnp.where(row < HEAD_DIM, qt, zero)
    qcat_ref[:, t:] = jnp.where(row >= HEAD_DIM, qt, zero)

    m_ref[...] = jnp.full(m_ref.shape, NEG_INF, F32)
    l_ref[...] = jnp.zeros_like(l_ref)
    acc_ref[...] = jnp.zeros_like(acc_ref)

    def step(j, bias_idx):
        kblk = k_ref[0, pl.ds(pl.multiple_of(j * t, t), t), :]
        vblk = vt_ref[0, 0, j]
        cmax = []
        for c in range(ncol):
            s = jnp.dot(kblk, qcat_ref[:, c * cw:(c + 1) * cw], preferred_element_type=F32)
            if bias_idx is not None:
                lo = (c % per_branch) * cw
                s = s + bias_ref[0, bias_idx, :, lo:lo + cw]
            s_ref[c] = s
            cmax.append(jnp.max(s, axis=0, keepdims=True))
        for c in range(ncol):
            m_old = m_ref[c]
            m_new = jnp.maximum(m_old, cmax[c])
            alpha = jnp.exp2(m_old - m_new)
            p = jnp.exp2(s_ref[c] - m_new)
            l_ref[c] = alpha * l_ref[c] + jnp.sum(p, axis=0, keepdims=True)
            m_ref[c] = m_new
            acc_ref[c] = alpha * acc_ref[c] + jnp.dot(vblk, p.astype(BF16),
                                                      preferred_element_type=F32)

    def far(j, carry):
        step(j, None)
        return carry

    lax.fori_loop(0, jnp.maximum(qi - 1, 0), far, 0)

    @pl.when(qi >= 1)
    def _():
        step(qi - 1, 1)

    step(qi, 0)

    lv = lam_ref[...]
    lam = (jnp.exp(jnp.sum(lv[0:1] * lv[1:2], axis=1, keepdims=True))
           - jnp.exp(jnp.sum(lv[2:3] * lv[3:4], axis=1, keepdims=True)) + lam_init)
    for c in range(per_branch):
        o1 = acc_ref[c] * (1.0 / l_ref[c])
        o2 = acc_ref[per_branch + c] * (1.0 / l_ref[per_branch + c])
        o = o1 - lam * o2
        o = o * lax.rsqrt(jnp.mean(o * o, axis=0, keepdims=True) + NORM_EPS)
        o = o * hn_ref[...] * (1.0 - lam_init)
        o_ref[0, c * cw:(c + 1) * cw, :] = o.T.astype(BF16)


def _diff_attention(qt, k, vt, bias, lam_vecs, head_norm, *, lam_init):
    batch, d, seq = qt.shape
    heads = d // V_DIM
    t = ATTN_TILE
    nblk = seq // t
    ncol = 2 * t // MXU_WIDTH
    lam_pad = jnp.zeros((8, V_DIM), F32).at[:4, :HEAD_DIM].set(lam_vecs.astype(F32))
    return pl.pallas_call(
        functools.partial(_flash_kernel, lam_init=lam_init),
        grid=(batch, heads, nblk),
        in_specs=[pl.BlockSpec((1, V_DIM, t), lambda b, h, i: (b, h, i)),
                  pl.BlockSpec((1, seq, V_DIM), lambda b, h, i: (b, 0, h)),
                  pl.BlockSpec((1, 1, nblk, V_DIM, t), lambda b, h, i: (b, h, 0, 0, 0)),
                  pl.BlockSpec((1, 2, t, t), lambda b, h, i: (h, 0, 0, 0)),
                  _const_spec((8, V_DIM)),
                  _const_spec((V_DIM, 1))],
        out_specs=pl.BlockSpec((1, t, V_DIM), lambda b, h, i: (b, i, h)),
        out_shape=jax.ShapeDtypeStruct((batch, seq, d), BF16),
        scratch_shapes=[pltpu.VMEM((V_DIM, 2 * t), BF16),
                        pltpu.VMEM((ncol, t, MXU_WIDTH), F32),
                        pltpu.VMEM((ncol, 1, MXU_WIDTH), F32),
                        pltpu.VMEM((ncol, 1, MXU_WIDTH), F32),
                        pltpu.VMEM((ncol, V_DIM, MXU_WIDTH), F32)],
        compiler_params=_params("parallel", "parallel", "arbitrary"),
        name="diff_attention",
    )(qt, k.reshape(batch, seq, d), vt, bias, lam_pad, head_norm.astype(F32).reshape(V_DIM, 1))


def kernel(x, norm_mixer_pre, norm_mixer_post, norm_mlp_pre, norm_mlp_post, mlp_w_up, mlp_w_down,
           ssm_w_in, ssm_a_re, ssm_a_im, ssm_log_dt, ssm_b_re, ssm_b_im, ssm_c_re, ssm_c_im, ssm_d,
           ssm_w_glu, kv_norm, w_kv, attn_w_q, attn_lambda_q1, attn_lambda_k1, attn_lambda_q2,
           attn_lambda_k2, attn_head_norm, attn_w_o, rel_bias):
    batch, seq, d = x.shape
    depth = norm_mixer_pre.shape[0]
    n_ssm = ssm_w_in.shape[0]
    xs = x.reshape(batch * seq, d).astype(F32)
    bias = k = vt = None
    for layer in range(depth):
        if layer < n_ssm:
            i = layer
            ops = _s5_operators(ssm_a_re[i], ssm_a_im[i], ssm_log_dt[i], ssm_b_re[i], ssm_b_im[i],
                                ssm_c_re[i], ssm_c_im[i])
            u = _norm_proj(xs, norm_mixer_pre[layer], ssm_w_in[i].astype(BF16),
                           mode="colblock", batch=batch, seq=seq)
            yg = _s5_core(u, ops, ssm_d[i].astype(F32), batch=batch, seq=seq)
            xs = _out_proj(yg, ssm_w_glu[i].astype(BF16), xs, norm_mixer_post[layer],
                           glu=True, chunked=True)
        else:
            if layer == n_ssm:
                k, vt = _norm_proj(xs, kv_norm, w_kv.astype(BF16), mode="kv", batch=batch, seq=seq)
                bias = _bias_tiles(rel_bias, ATTN_TILE)
            j = layer - n_ssm
            lam_init = 0.8 - 0.6 * math.exp(-0.3 * layer)
            qt = _norm_proj(xs, norm_mixer_pre[layer], attn_w_q[j].astype(BF16), mode="qT",
                            batch=batch, seq=seq, scale=HEAD_DIM ** -0.5 * LOG2E)
            lam_vecs = jnp.stack([attn_lambda_q1[j], attn_lambda_k1[j],
                                  attn_lambda_q2[j], attn_lambda_k2[j]])
            o = _diff_attention(qt, k, vt, bias, lam_vecs, attn_head_norm[j], lam_init=lam_init)
            xs = _out_proj(o.reshape(batch * seq, d), attn_w_o[j].astype(BF16), xs,
                           norm_mixer_post[layer], glu=False, chunked=False)
        xs = _mlp(xs, norm_mlp_pre[layer], mlp_w_up[layer].astype(BF16),
                  mlp_w_down[layer].astype(BF16), norm_mlp_post[layer])
    return xs.reshape(batch, seq, d).astype(x.dtype)
```

```python
import functools
import math

import numpy as np
import jax
import jax.numpy as jnp
from jax import lax
from jax.experimental import pallas as pl
from jax.experimental.pallas import tpu as pltpu

F32 = jnp.float32
BF16 = jnp.bfloat16

SSM_GROUP = 16
SSM_STATE = 64
HEAD_DIM = 64
V_DIM = 2 * HEAD_DIM
NUM_BUCKETS = 32
MAX_DISTANCE = 128
NORM_EPS = 1e-6
NEG_INF = -1e30
LOG2E = math.log2(math.e)

MXU_WIDTH = 256
ROW_TILE = 512
SCAN_CHUNK = 8
SCAN_ROWS = 512
ATTN_TILE = 512
FF_CHUNK = 1024
VMEM_LIMIT = 56 * 1024 * 1024


def _params(*sem):
    return pltpu.CompilerParams(dimension_semantics=sem, vmem_limit_bytes=VMEM_LIMIT)


def _rms(xf, gain):
    return xf * lax.rsqrt(jnp.mean(xf * xf, axis=-1, keepdims=True) + NORM_EPS) * gain


def _const_spec(shape):
    zeros = (0,) * len(shape)
    return pl.BlockSpec(shape, lambda *_: zeros)


def _proj_kernel(x_ref, g_ref, w_ref, *out_refs, mode, scale):
    hn = _rms(x_ref[...], g_ref[...]).astype(BF16)
    acc = jnp.dot(hn, w_ref[...], preferred_element_type=F32)
    rows = acc.shape[0]
    if mode == "colblock":
        (o_ref,) = out_refs
        for cb in range(o_ref.shape[0]):
            o_ref[cb] = acc[:, cb * MXU_WIDTH:(cb + 1) * MXU_WIDTH].astype(BF16)
    elif mode == "qT":
        (o_ref,) = out_refs
        o_ref[0] = (acc * scale).T.astype(BF16)
    else:
        k_ref, vt_ref = out_refs
        d = k_ref.shape[1]
        k_ref[...] = acc[:, :d].astype(BF16)
        vt = acc[:, d:].T.astype(BF16)
        vt_ref[0, :, 0] = vt.reshape(d // V_DIM, V_DIM, rows)


def _norm_proj(x, gain, w, *, mode, batch, seq, scale=1.0):
    n, d = x.shape
    nout = w.shape[1]
    tm = ROW_TILE
    per_b = seq // tm
    if mode == "colblock":
        ncb = nout // MXU_WIDTH
        out_shape = jax.ShapeDtypeStruct((ncb, n, MXU_WIDTH), BF16)
        out_specs = pl.BlockSpec((ncb, tm, MXU_WIDTH), lambda i: (0, i, 0))
    elif mode == "qT":
        out_shape = jax.ShapeDtypeStruct((batch, nout, seq), BF16)
        out_specs = pl.BlockSpec((1, nout, tm), lambda i: (i // per_b, 0, i % per_b))
    else:
        heads = d // V_DIM
        assert tm == ATTN_TILE
        out_shape = (jax.ShapeDtypeStruct((n, d), BF16),
                     jax.ShapeDtypeStruct((batch, heads, per_b, V_DIM, tm), BF16))
        out_specs = (pl.BlockSpec((tm, d), lambda i: (i, 0)),
                     pl.BlockSpec((1, heads, 1, V_DIM, tm),
                                  lambda i: (i // per_b, 0, i % per_b, 0, 0)))
    return pl.pallas_call(
        functools.partial(_proj_kernel, mode=mode, scale=scale),
        grid=(n // tm,),
        in_specs=[pl.BlockSpec((tm, d), lambda i: (i, 0)),
                  _const_spec((1, d)),
                  _const_spec((d, nout))],
        out_specs=out_specs,
        out_shape=out_shape,
        compiler_params=_params("parallel"),
        name="norm_proj_" + mode,
    )(x, gain.reshape(1, d), w)


def _out_kernel(y_ref, w_ref, x_ref, g_ref, o_ref, *, glu, chunked):
    if chunked:
        acc = jnp.dot(y_ref[0], w_ref[0], preferred_element_type=F32)
        for cb in range(1, y_ref.shape[0]):
            acc = acc + jnp.dot(y_ref[cb], w_ref[cb], preferred_element_type=F32)
    else:
        acc = jnp.dot(y_ref[...], w_ref[...], preferred_element_type=F32)
    if glu:
        d = o_ref.shape[1]
        acc = acc[:, :d] * jax.nn.sigmoid(acc[:, d:])
    o_ref[...] = x_ref[...] + _rms(acc, g_ref[...])


def _out_proj(y, w, x, gain, *, glu, chunked):
    n, d = x.shape
    tm = ROW_TILE
    if chunked:
        ncb = y.shape[0]
        w = w.reshape(ncb, MXU_WIDTH, w.shape[1])
        y_spec = pl.BlockSpec((ncb, tm, MXU_WIDTH), lambda i: (0, i, 0))
    else:
        y_spec = pl.BlockSpec((tm, y.shape[1]), lambda i: (i, 0))
    return pl.pallas_call(
        functools.partial(_out_kernel, glu=glu, chunked=chunked),
        grid=(n // tm,),
        in_specs=[y_spec, _const_spec(w.shape),
                  pl.BlockSpec((tm, d), lambda i: (i, 0)), _const_spec((1, d))],
        out_specs=pl.BlockSpec((tm, d), lambda i: (i, 0)),
        out_shape=jax.ShapeDtypeStruct((n, d), F32),
        compiler_params=_params("parallel"),
        name="out_proj_glu" if glu else "out_proj",
    )(y, w, x, gain.reshape(1, d))


def _mlp_kernel(x_ref, g1_ref, wu_ref, wd_ref, g2_ref, o_ref):
    x = x_ref[...]
    hn = _rms(x, g1_ref[...]).astype(BF16)
    acc = None
    for c in range(wu_ref.shape[1] // FF_CHUNK):
        a = jnp.dot(hn, wu_ref[:, c * FF_CHUNK:(c + 1) * FF_CHUNK], preferred_element_type=F32)
        a = jnp.maximum(a, 0.0)
        a = (a * a).astype(BF16)
        part = jnp.dot(a, wd_ref[c * FF_CHUNK:(c + 1) * FF_CHUNK, :], preferred_element_type=F32)
        acc = part if acc is None else acc + part
    o_ref[...] = x + _rms(acc, g2_ref[...])


def _mlp(x, g_pre, w_up, w_down, g_post):
    n, d = x.shape
    dff = w_up.shape[1]
    tm = ROW_TILE
    return pl.pallas_call(
        _mlp_kernel,
        grid=(n // tm,),
        in_specs=[pl.BlockSpec((tm, d), lambda i: (i, 0)), _const_spec((1, d)),
                  _const_spec((d, dff)), _const_spec((dff, d)), _const_spec((1, d))],
        out_specs=pl.BlockSpec((tm, d), lambda i: (i, 0)),
        out_shape=jax.ShapeDtypeStruct((n, d), F32),
        compiler_params=_params("parallel"),
        name="mlp",
    )(x, g_pre.reshape(1, d), w_up, w_down, g_post.reshape(1, d))


def _s5_kernel(u_ref, w_ref, k_ref, p_ref, apr_ref, api_ref, d_ref, o_ref, v_ref, sr_ref, si_ref):
    cw = MXU_WIDTH
    ns = v_ref.shape[1] // 2

    @pl.when(pl.program_id(2) == 0)
    def _():
        sr_ref[...] = jnp.zeros_like(sr_ref)
        si_ref[...] = jnp.zeros_like(si_ref)

    x = u_ref[0]
    v_ref[...] = jnp.dot(x, w_ref[0], preferred_element_type=F32)

    apr = apr_ref[0]
    api = api_ref[0]
    row = lax.broadcasted_iota(jnp.int32, (8, ns), 0)

    def shift(z, k, fill):
        return jnp.where(row >= k, pltpu.roll(z, k, 0), fill)

    def block(i, carry):
        sr, si = carry
        r0 = pl.multiple_of(i * 8, 8)
        zr = v_ref[pl.ds(r0, 8), 0:ns]
        zi = v_ref[pl.ds(r0, 8), ns:2 * ns]
        for k in (1, 2, 4):
            ar, ai = apr[k - 1:k], api[k - 1:k]
            hr, hi = shift(zr, k, 0.0), shift(zi, k, 0.0)
            zr, zi = zr + ar * hr - ai * hi, zi + ar * hi + ai * hr
        fr = zr + apr * sr - api * si
        fi = zi + apr * si + api * sr
        v_ref[pl.ds(r0, 8), 0:ns] = shift(fr, 1, sr)
        v_ref[pl.ds(r0, 8), ns:2 * ns] = shift(fi, 1, si)
        return fr[7:8], fi[7:8]

    sr, si = lax.fori_loop(0, v_ref.shape[0] // 8, block, (sr_ref[...], si_ref[...]))
    sr_ref[...] = sr
    si_ref[...] = si

    sp = v_ref[...].astype(BF16)
    steps = x.shape[1] // cw
    for t in range(steps):
        acc = jnp.dot(x[:, :(t + 1) * cw], k_ref[0, (steps - 1 - t) * cw:, :],
                      preferred_element_type=F32)
        acc = acc + jnp.dot(sp, p_ref[0, t], preferred_element_type=F32)
        y = acc + d_ref[0] * x[:, t * cw:(t + 1) * cw].astype(F32)
        o_ref[0, :, t * cw:(t + 1) * cw] = jax.nn.gelu(y).astype(BF16)


def _s5_core(u_cb, ops, d_skip, *, batch, seq):
    wstack, kstack, pstack, apr, api = ops
    ncb, n, cw = u_cb.shape
    L = SCAN_CHUNK
    rows_per_b = seq // L
    halves = rows_per_b // SCAN_ROWS
    u_v = u_cb.reshape(ncb, n // L, L * cw)
    ns2 = wstack.shape[2]
    blk = pl.BlockSpec((1, SCAN_ROWS, L * cw), lambda cb, b, h: (cb, b * halves + h, 0))
    out = pl.pallas_call(
        _s5_kernel,
        grid=(ncb, batch, halves),
        in_specs=[blk,
                  pl.BlockSpec((1, L * cw, ns2), lambda cb, b, h: (cb, 0, 0)),
                  pl.BlockSpec((1, L * cw, cw), lambda cb, b, h: (cb, 0, 0)),
                  pl.BlockSpec((1, L, ns2, cw), lambda cb, b, h: (cb, 0, 0, 0)),
                  pl.BlockSpec((1, 8, ns2 // 2), lambda cb, b, h: (cb, 0, 0)),
                  pl.BlockSpec((1, 8, ns2 // 2), lambda cb, b, h: (cb, 0, 0)),
                  pl.BlockSpec((1, 1, cw), lambda cb, b, h: (cb, 0, 0))],
        out_specs=blk,
        out_shape=jax.ShapeDtypeStruct(u_v.shape, BF16),
        scratch_shapes=[pltpu.VMEM((SCAN_ROWS, ns2), F32),
                        pltpu.VMEM((1, ns2 // 2), F32),
                        pltpu.VMEM((1, ns2 // 2), F32)],
        compiler_params=_params("arbitrary", "arbitrary", "arbitrary"),
        name="s5_core",
    )(u_v, wstack, kstack, pstack, apr, api, d_skip.reshape(ncb, 1, cw))
    return out.reshape(ncb, n, cw)


def _block_diag(t):
    g, a, b = t.shape[-3:]
    eye = jnp.eye(g, dtype=t.dtype)
    out = jnp.einsum("...gab,gh->...gahb", t, eye)
    return out.reshape(t.shape[:-3] + (g * a, g * b))


def _s5_operators(a_re, a_im, log_dt, b_re, b_im, c_re, c_im):
    L = SCAN_CHUNK
    groups, p = a_re.shape
    gl = MXU_WIDTH // SSM_GROUP
    ncb = groups // gl
    dt = jnp.exp(log_dt.astype(F32))[:, None]
    lr, li = a_re.astype(F32), a_im.astype(F32)
    mag = jnp.exp(lr * dt)
    ab_re = mag * jnp.cos(li * dt)
    ab_im = mag * jnp.sin(li * dt)
    den = lr * lr + li * li
    coef_re = ((ab_re - 1.0) * lr + ab_im * li) / den
    coef_im = (ab_im * lr - (ab_re - 1.0) * li) / den
    br, bi = b_re.astype(F32), b_im.astype(F32)
    bb_re = coef_re[..., None] * br - coef_im[..., None] * bi
    bb_im = coef_re[..., None] * bi + coef_im[..., None] * br
    cr, ci = c_re.astype(F32), c_im.astype(F32)

    def apow(m):
        m = jnp.asarray(m, F32)[:, None, None]
        mg = jnp.exp(m * (lr * dt))
        return mg * jnp.cos(m * (li * dt)), mg * jnp.sin(m * (li * dt))

    pr, pi = apow(np.arange(L))
    ca_re = cr[None] * pr[:, :, None, :] - ci[None] * pi[:, :, None, :]
    ca_im = cr[None] * pi[:, :, None, :] + ci[None] * pr[:, :, None, :]
    km = jnp.einsum("mgap,gpc->mgca", ca_re, bb_re) - jnp.einsum("mgap,gpc->mgca", ca_im, bb_im)
    km = km[::-1].reshape(L, ncb, gl, SSM_GROUP, SSM_GROUP)
    kstack = _block_diag(km).transpose(1, 0, 2, 3).reshape(ncb, L * MXU_WIDTH, MXU_WIDTH)

    pr, pi = apow(np.arange(L - 1, -1, -1))
    w_re = pr[..., None] * bb_re[None] - pi[..., None] * bb_im[None]
    w_im = pr[..., None] * bb_im[None] + pi[..., None] * bb_re[None]

    def to_w(w):
        w = w.transpose(0, 1, 3, 2).reshape(L, ncb, gl, SSM_GROUP, p)
        return _block_diag(w)

    wstack = jnp.concatenate([to_w(w_re), to_w(w_im)], axis=-1)
    wstack = wstack.transpose(1, 0, 2, 3).reshape(ncb, L * MXU_WIDTH, 2 * gl * p)

    pr, pi = apow(np.arange(1, L + 1))
    q_re = cr[None] * pr[:, :, None, :] - ci[None] * pi[:, :, None, :]
    q_im = cr[None] * pi[:, :, None, :] + ci[None] * pr[:, :, None, :]

    def to_p(q):
        q = q.transpose(0, 1, 3, 2).reshape(L, ncb, gl, p, SSM_GROUP)
        return _block_diag(q)

    pstack = jnp.concatenate([to_p(q_re), to_p(-q_im)], axis=-2).transpose(1, 0, 2, 3)

    pr, pi = apow(L * np.arange(1, 9))
    apr = pr.reshape(8, ncb, gl * p).transpose(1, 0, 2)
    api = pi.reshape(8, ncb, gl * p).transpose(1, 0, 2)
    return wstack.astype(BF16), kstack.astype(BF16), pstack.astype(BF16), apr, api


def _bucket_tiles(t):
    i = np.arange(t)[:, None]
    j = np.arange(t)[None, :]
    tiles = []
    for off in (0, t):
        rel = j + off - i
        n = np.maximum(rel, 0)
        max_exact = NUM_BUCKETS // 2
        large = max_exact + (np.log(np.maximum(n, max_exact).astype(np.float32) / max_exact)
                             / math.log(MAX_DISTANCE / max_exact)
                             * (NUM_BUCKETS - max_exact)).astype(np.int32)
        large = np.minimum(large, NUM_BUCKETS - 1)
        b = np.where(n < max_exact, n, large)
        tiles.append(np.where(rel >= 0, b, -1).astype(np.int32))
    return np.stack(tiles)


def _bias_kernel(bkt_ref, tab_ref, o_ref):
    h = pl.program_id(0)
    far = tab_ref[NUM_BUCKETS - 1, h]
    for typ in range(2):
        b = bkt_ref[typ]
        acc = jnp.zeros(b.shape, F32)
        for i in range(NUM_BUCKETS - 1):
            acc = jnp.where(b == i, (tab_ref[i, h] - far) * LOG2E, acc)
        o_ref[0, typ] = jnp.where(b < 0, NEG_INF, acc)


def _bias_tiles(rel_bias, t):
    heads = rel_bias.shape[1]
    return pl.pallas_call(
        _bias_kernel,
        grid=(heads,),
        in_specs=[_const_spec((2, t, t)), pl.BlockSpec(memory_space=pltpu.SMEM)],
        out_specs=pl.BlockSpec((1, 2, t, t), lambda h: (h, 0, 0, 0)),
        out_shape=jax.ShapeDtypeStruct((heads, 2, t, t), F32),
        compiler_params=_params("parallel"),
        name="rel_bias_tiles",
    )(jnp.asarray(_bucket_tiles(t)), rel_bias.astype(F32))


def _flash_kernel(q_ref, k_ref, vt_ref, bias_ref, lam_ref, hn_ref, o_ref,
                  qcat_ref, sa_ref, sb_ref, m_ref, l_ref, acc_ref, *, lam_init):
    t = q_ref.shape[2]
    cw = MXU_WIDTH
    per_branch = t // cw
    ncol = 2 * per_branch
    qi = pl.program_id(2)
    qt = q_ref[0]
    row = lax.broadcasted_iota(jnp.int32, qt.shape, 0)
    zero = jnp.zeros_like(qt)
    qcat_ref[:, :t] = jnp.where(row < HEAD_DIM, qt, zero)
    qcat_ref[:, t:] = jnp.where(row >= HEAD_DIM, qt, zero)

    m_ref[...] = jnp.full(m_ref.shape, NEG_INF, F32)
    l_ref[...] = jnp.zeros_like(l_ref)
    acc_ref[...] = jnp.zeros_like(acc_ref)

    def scores(j, dst_ref, want_max):
        kblk = k_ref[0, pl.ds(pl.multiple_of(j * t, t), t), :]
        cmax = []
        for c in range(ncol):
            s = jnp.dot(kblk, qcat_ref[:, c * cw:(c + 1) * cw], preferred_element_type=F32)
            dst_ref[c] = s
            if want_max:
                cmax.append(jnp.max(s, axis=0, keepdims=True))
        return tuple(cmax)

    def softmax_pv(j, src_ref, cmax, bias_idx):
        vblk = vt_ref[0, 0, j]
        for c in range(ncol):
            if bias_idx is None:
                s, cm = src_ref[c], cmax[c]
            else:
                lo = (c % per_branch) * cw
                s = src_ref[c] + bias_ref[0, bias_idx, :, lo:lo + cw]
                cm = jnp.max(s, axis=0, keepdims=True)
            m_old = m_ref[c]
            m_new = jnp.maximum(m_old, cm)
            alpha = jnp.exp2(m_old - m_new)
            p = jnp.exp2(s - m_new)
            l_ref[c] = alpha * l_ref[c] + jnp.sum(p, axis=0, keepdims=True)
            m_ref[c] = m_new
            acc_ref[c] = alpha * acc_ref[c] + jnp.dot(vblk, p.astype(BF16),
                                                      preferred_element_type=F32)

    n_far = jnp.maximum(qi - 1, 0)
    n_pairs = n_far // 2
    cm_a = scores(0, sa_ref, True)

    def pair(i, cm_a):
        j = 2 * i
        cm_b = scores(j + 1, sb_ref, True)
        softmax_pv(j, sa_ref, cm_a, None)
        cm_next = scores(j + 2, sa_ref, True)
        softmax_pv(j + 1, sb_ref, cm_b, None)
        return cm_next

    cm_a = lax.fori_loop(0, n_pairs, pair, cm_a)
    odd_far = n_far - 2 * n_pairs

    @pl.when(qi == 0)
    def _():
        softmax_pv(0, sa_ref, None, 0)

    @pl.when(jnp.logical_and(qi >= 1, odd_far == 0))
    def _():
        scores(qi, sb_ref, False)
        softmax_pv(qi - 1, sa_ref, None, 1)
        softmax_pv(qi, sb_ref, None, 0)

    @pl.when(jnp.logical_and(qi >= 1, odd_far == 1))
    def _():
        scores(qi - 1, sb_ref, False)
        softmax_pv(qi - 2, sa_ref, cm_a, None)
        scores(qi, sa_ref, False)
        softmax_pv(qi - 1, sb_ref, None, 1)
        softmax_pv(qi, sa_ref, None, 0)

    lv = lam_ref[...]
    lam = (jnp.exp(jnp.sum(lv[0:1] * lv[1:2], axis=1, keepdims=True))
           - jnp.exp(jnp.sum(lv[2:3] * lv[3:4], axis=1, keepdims=True)) + lam_init)
    for c in range(per_branch):
        o1 = acc_ref[c] * (1.0 / l_ref[c])
        o2 = acc_ref[per_branch + c] * (1.0 / l_ref[per_branch + c])
        o = o1 - lam * o2
        o = o * lax.rsqrt(jnp.mean(o * o, axis=0, keepdims=True) + NORM_EPS)
        o = o * hn_ref[...] * (1.0 - lam_init)
        o_ref[0, c * cw:(c + 1) * cw, :] = o.T.astype(BF16)


def _diff_attention(qt, k, vt, bias, lam_vecs, head_norm, *, lam_init):
    batch, d, seq = qt.shape
    heads = d // V_DIM
    t = ATTN_TILE
    nblk = seq // t
    ncol = 2 * t // MXU_WIDTH
    lam_pad = jnp.zeros((8, V_DIM), F32).at[:4, :HEAD_DIM].set(lam_vecs.astype(F32))
    return pl.pallas_call(
        functools.partial(_flash_kernel, lam_init=lam_init),
        grid=(batch, heads, nblk),
        in_specs=[pl.BlockSpec((1, V_DIM, t), lambda b, h, i: (b, h, i)),
                  pl.BlockSpec((1, seq, V_DIM), lambda b, h, i: (b, 0, h)),
                  pl.BlockSpec((1, 1, nblk, V_DIM, t), lambda b, h, i: (b, h, 0, 0, 0)),
                  pl.BlockSpec((1, 2, t, t), lambda b, h, i: (h, 0, 0, 0)),
                  _const_spec((8, V_DIM)),
                  _const_spec((V_DIM, 1))],
        out_specs=pl.BlockSpec((1, t, V_DIM), lambda b, h, i: (b, i, h)),
        out_shape=jax.ShapeDtypeStruct((batch, seq, d), BF16),
        scratch_shapes=[pltpu.VMEM((V_DIM, 2 * t), BF16),
                        pltpu.VMEM((ncol, t, MXU_WIDTH), F32),
                        pltpu.VMEM((ncol, t, MXU_WIDTH), F32),
                        pltpu.VMEM((ncol, 1, MXU_WIDTH), F32),
                        pltpu.VMEM((ncol, 1, MXU_WIDTH), F32),
                        pltpu.VMEM((ncol, V_DIM, MXU_WIDTH), F32)],
        compiler_params=_params("parallel", "parallel", "arbitrary"),
        name="diff_attention",
    )(qt, k.reshape(batch, seq, d), vt, bias, lam_pad, head_norm.astype(F32).reshape(V_DIM, 1))


def kernel(x, norm_mixer_pre, norm_mixer_post, norm_mlp_pre, norm_mlp_post, mlp_w_up, mlp_w_down,
           ssm_w_in, ssm_a_re, ssm_a_im, ssm_log_dt, ssm_b_re, ssm_b_im, ssm_c_re, ssm_c_im, ssm_d,
           ssm_w_glu, kv_norm, w_kv, attn_w_q, attn_lambda_q1, attn_lambda_k1, attn_lambda_q2,
           attn_lambda_k2, attn_head_norm, attn_w_o, rel_bias):
    batch, seq, d = x.shape
    depth = norm_mixer_pre.shape[0]
    n_ssm = ssm_w_in.shape[0]
    xs = x.reshape(batch * seq, d).astype(F32)
    bias = k = vt = None
    for layer in range(depth):
        if layer < n_ssm:
            i = layer
            ops = _s5_operators(ssm_a_re[i], ssm_a_im[i], ssm_log_dt[i], ssm_b_re[i], ssm_b_im[i],
                                ssm_c_re[i], ssm_c_im[i])
            u = _norm_proj(xs, norm_mixer_pre[layer], ssm_w_in[i].astype(BF16),
                           mode="colblock", batch=batch, seq=seq)
            yg = _s5_core(u, ops, ssm_d[i].astype(F32), batch=batch, seq=seq)
            xs = _out_proj(yg, ssm_w_glu[i].astype(BF16), xs, norm_mixer_post[layer],
                           glu=True, chunked=True)
        else:
            if layer == n_ssm:
                k, vt = _norm_proj(xs, kv_norm, w_kv.astype(BF16), mode="kv", batch=batch, seq=seq)
                bias = _bias_tiles(rel_bias, ATTN_TILE)
            j = layer - n_ssm
            lam_init = 0.8 - 0.6 * math.exp(-0.3 * layer)
            qt = _norm_proj(xs, norm_mixer_pre[layer], attn_w_q[j].astype(BF16), mode="qT",
                            batch=batch, seq=seq, scale=HEAD_DIM ** -0.5 * LOG2E)
            lam_vecs = jnp.stack([attn_lambda_q1[j], attn_lambda_k1[j],
                                  attn_lambda_q2[j], attn_lambda_k2[j]])
            o = _diff_attention(qt, k, vt, bias, lam_vecs, attn_head_norm[j], lam_init=lam_init)
            xs = _out_proj(o.reshape(batch * seq, d), attn_w_o[j].astype(BF16), xs,
                           norm_mixer_post[layer], glu=False, chunked=False)
        xs = _mlp(xs, norm_mlp_pre[layer], mlp_w_up[layer].astype(BF16),
                  mlp_w_down[layer].astype(BF16), norm_mlp_post[layer])
    return xs.reshape(batch, seq, d).astype(x.dtype)
```

```python
import functools
import math

import numpy as np
import jax
import jax.numpy as jnp
from jax import lax
from jax.experimental import pallas as pl
from jax.experimental.pallas import tpu as pltpu

F32 = jnp.float32
BF16 = jnp.bfloat16

SSM_GROUP = 16
SSM_STATE = 64
HEAD_DIM = 64
V_DIM = 2 * HEAD_DIM
NUM_BUCKETS = 32
MAX_DISTANCE = 128
NORM_EPS = 1e-6
NEG_INF = -1e30
LOG2E = math.log2(math.e)

MXU_WIDTH = 256
LANES = 128
ROW_TILE = 512
SCAN_CHUNK = 8
SCAN_ROWS = 512
ATTN_TILE = 512
FF_CHUNK = 1024
VMEM_LIMIT = 56 * 1024 * 1024


def _params(*sem, flags=None):
    return pltpu.CompilerParams(dimension_semantics=sem, vmem_limit_bytes=VMEM_LIMIT, flags=flags)


def _rms(xf, gain):
    return xf * lax.rsqrt(jnp.mean(xf * xf, axis=-1, keepdims=True) + NORM_EPS) * gain


def _const_spec(shape):
    zeros = (0,) * len(shape)
    return pl.BlockSpec(shape, lambda *_: zeros)


def _proj_kernel(x_ref, g_ref, w_ref, *out_refs, mode, scale):
    hn = _rms(x_ref[...], g_ref[...]).astype(BF16)
    acc = jnp.dot(hn, w_ref[...], preferred_element_type=F32)
    rows = acc.shape[0]
    if mode == "colblock":
        o_ref, slab_ref = out_refs
        L = SCAN_CHUNK
        for sl in range(slab_ref.shape[0]):
            slab_ref[sl] = acc[:, sl * LANES:(sl + 1) * LANES]
        for cb in range(o_ref.shape[0]):
            for t in range(L):
                for hf in range(MXU_WIDTH // LANES):
                    lo = t * MXU_WIDTH + hf * LANES
                    piece = slab_ref[cb * (MXU_WIDTH // LANES) + hf, pl.ds(t, rows // L, stride=L), :]
                    o_ref[cb, :, lo:lo + LANES] = piece.astype(BF16)
    elif mode == "qT":
        (o_ref,) = out_refs
        o_ref[0] = (acc * scale).T.astype(BF16)
    else:
        k_ref, vt_ref = out_refs
        d = k_ref.shape[1]
        k_ref[...] = acc[:, :d].astype(BF16)
        vt = acc[:, d:].T.astype(BF16)
        vt_ref[0, :, 0] = vt.reshape(d // V_DIM, V_DIM, rows)


def _norm_proj(x, gain, w, *, mode, batch, seq, scale=1.0):
    n, d = x.shape
    nout = w.shape[1]
    tm = ROW_TILE
    per_b = seq // tm
    scratch = []
    if mode == "colblock":
        ncb = nout // MXU_WIDTH
        L = SCAN_CHUNK
        out_shape = jax.ShapeDtypeStruct((ncb, n // L, L * MXU_WIDTH), BF16)
        out_specs = pl.BlockSpec((ncb, tm // L, L * MXU_WIDTH), lambda i: (0, i, 0))
        scratch = [pltpu.VMEM((nout // LANES, tm, LANES), F32)]
    elif mode == "qT":
        out_shape = jax.ShapeDtypeStruct((batch, nout, seq), BF16)
        out_specs = pl.BlockSpec((1, nout, tm), lambda i: (i // per_b, 0, i % per_b))
    else:
        heads = d // V_DIM
        assert tm == ATTN_TILE
        out_shape = (jax.ShapeDtypeStruct((n, d), BF16),
                     jax.ShapeDtypeStruct((batch, heads, per_b, V_DIM, tm), BF16))
        out_specs = (pl.BlockSpec((tm, d), lambda i: (i, 0)),
                     pl.BlockSpec((1, heads, 1, V_DIM, tm),
                                  lambda i: (i // per_b, 0, i % per_b, 0, 0)))
    return pl.pallas_call(
        functools.partial(_proj_kernel, mode=mode, scale=scale),
        grid=(n // tm,),
        in_specs=[pl.BlockSpec((tm, d), lambda i: (i, 0)),
                  _const_spec((1, d)),
                  _const_spec((d, nout))],
        out_specs=out_specs,
        out_shape=out_shape,
        scratch_shapes=scratch,
        compiler_params=_params("parallel"),
        name="norm_proj_" + mode,
    )(x, gain.reshape(1, d), w)


def _out_kernel(y_ref, w_ref, x_ref, g_ref, o_ref, *scratch, glu, chunked):
    if chunked:
        slab_ref, tok_ref = scratch
        L = SCAN_CHUNK
        nchunk = y_ref.shape[1]
        for cb in range(y_ref.shape[0]):
            for t in range(L):
                for hf in range(MXU_WIDTH // LANES):
                    lo = t * MXU_WIDTH + hf * LANES
                    slab_ref[cb * (MXU_WIDTH // LANES) + hf, pl.ds(t, nchunk, stride=L), :] = (
                        y_ref[cb, :, lo:lo + LANES].astype(F32))
        for sl in range(slab_ref.shape[0]):
            tok_ref[:, sl * LANES:(sl + 1) * LANES] = slab_ref[sl].astype(BF16)
        acc = jnp.dot(tok_ref[...], w_ref[...], preferred_element_type=F32)
    else:
        acc = jnp.dot(y_ref[...], w_ref[...], preferred_element_type=F32)
    if glu:
        d = o_ref.shape[1]
        acc = acc[:, :d] * jax.nn.sigmoid(acc[:, d:])
    o_ref[...] = x_ref[...] + _rms(acc, g_ref[...])


def _out_proj(y, w, x, gain, *, glu, chunked):
    n, d = x.shape
    tm = ROW_TILE
    scratch = []
    if chunked:
        ncb, _, width = y.shape
        y_spec = pl.BlockSpec((ncb, tm // SCAN_CHUNK, width), lambda i: (0, i, 0))
        scratch = [pltpu.VMEM((d // LANES, tm, LANES), F32), pltpu.VMEM((tm, d), BF16)]
    else:
        y_spec = pl.BlockSpec((tm, y.shape[1]), lambda i: (i, 0))
    return pl.pallas_call(
        functools.partial(_out_kernel, glu=glu, chunked=chunked),
        grid=(n // tm,),
        in_specs=[y_spec, _const_spec(w.shape),
                  pl.BlockSpec((tm, d), lambda i: (i, 0)), _const_spec((1, d))],
        out_specs=pl.BlockSpec((tm, d), lambda i: (i, 0)),
        out_shape=jax.ShapeDtypeStruct((n, d), F32),
        scratch_shapes=scratch,
        compiler_params=_params("parallel"),
        name="out_proj_glu" if glu else "out_proj",
    )(y, w, x, gain.reshape(1, d))


def _mlp_kernel(x_ref, g1_ref, wu_ref, wd_ref, g2_ref, o_ref):
    x = x_ref[...]
    hn = _rms(x, g1_ref[...]).astype(BF16)
    acc = None
    for c in range(wu_ref.shape[1] // FF_CHUNK):
        a = jnp.dot(hn, wu_ref[:, c * FF_CHUNK:(c + 1) * FF_CHUNK], preferred_element_type=F32)
        a = jnp.maximum(a, 0.0)
        a = (a * a).astype(BF16)
        part = jnp.dot(a, wd_ref[c * FF_CHUNK:(c + 1) * FF_CHUNK, :], preferred_element_type=F32)
        acc = part if acc is None else acc + part
    o_ref[...] = x + _rms(acc, g2_ref[...])


def _mlp(x, g_pre, w_up, w_down, g_post):
    n, d = x.shape
    dff = w_up.shape[1]
    tm = ROW_TILE
    return pl.pallas_call(
        _mlp_kernel,
        grid=(n // tm,),
        in_specs=[pl.BlockSpec((tm, d), lambda i: (i, 0)), _const_spec((1, d)),
                  _const_spec((d, dff)), _const_spec((dff, d)), _const_spec((1, d))],
        out_specs=pl.BlockSpec((tm, d), lambda i: (i, 0)),
        out_shape=jax.ShapeDtypeStruct((n, d), F32),
        compiler_params=_params("parallel"),
        name="mlp",
    )(x, g_pre.reshape(1, d), w_up, w_down, g_post.reshape(1, d))


def _s5_kernel(u_ref, wc_ref, kc_ref, pc_ref, apr_ref, api_ref, d_ref, o_ref,
               w_ref, k_ref, p_ref, v_ref, sr_ref, si_ref):
    cw = MXU_WIDTH
    ns = v_ref.shape[1] // 2
    L = u_ref.shape[2] // cw
    grp_shift = SSM_GROUP.bit_length() - 1
    st_shift = SSM_STATE.bit_length() - 1

    @pl.when(jnp.logical_and(pl.program_id(1) == 0, pl.program_id(2) == 0))
    def _():
        def iota(shape, dim):
            return lax.broadcasted_iota(jnp.int32, shape, dim)

        wmask = (iota((cw, ns), 0) >> grp_shift) == (iota((cw, ns), 1) >> st_shift)
        pmask = (iota((ns, cw), 0) >> st_shift) == (iota((ns, cw), 1) >> grp_shift)
        kmask = (iota((cw, cw), 0) >> grp_shift) == (iota((cw, cw), 1) >> grp_shift)
        for t in range(L):
            for h in range(2):
                piece = wc_ref[0, t, h].astype(F32)
                tiled = jnp.concatenate([piece] * (ns // LANES), axis=1)
                w_ref[t * cw:(t + 1) * cw, h * ns:(h + 1) * ns] = (
                    jnp.where(wmask, tiled, 0.0).astype(BF16))
                piece = pc_ref[0, t, h * SSM_STATE:(h + 1) * SSM_STATE, :].astype(F32)
                tiled = jnp.concatenate([piece] * (ns // SSM_STATE), axis=0)
                p_ref[t, h * ns:(h + 1) * ns, :] = jnp.where(pmask, tiled, 0.0).astype(BF16)
            piece = kc_ref[0, t].astype(F32)
            tiled = jnp.concatenate([piece] * (cw // SSM_GROUP), axis=0)
            k_ref[t * cw:(t + 1) * cw, :] = jnp.where(kmask, tiled, 0.0).astype(BF16)

    @pl.when(pl.program_id(2) == 0)
    def _():
        sr_ref[...] = jnp.zeros_like(sr_ref)
        si_ref[...] = jnp.zeros_like(si_ref)

    x = u_ref[0]
    v_ref[...] = jnp.dot(x, w_ref[...], preferred_element_type=F32)

    apr = apr_ref[0]
    api = api_ref[0]
    row = lax.broadcasted_iota(jnp.int32, (8, ns), 0)

    def shift(z, k, fill):
        return jnp.where(row >= k, pltpu.roll(z, k, 0), fill)

    def block(i, carry):
        sr, si = carry
        r0 = pl.multiple_of(i * 8, 8)
        zr = v_ref[pl.ds(r0, 8), 0:ns]
        zi = v_ref[pl.ds(r0, 8), ns:2 * ns]
        for k in (1, 2, 4):
            ar, ai = apr[k - 1:k], api[k - 1:k]
            hr, hi = shift(zr, k, 0.0), shift(zi, k, 0.0)
            zr, zi = zr + ar * hr - ai * hi, zi + ar * hi + ai * hr
        fr = zr + apr * sr - api * si
        fi = zi + apr * si + api * sr
        v_ref[pl.ds(r0, 8), 0:ns] = shift(fr, 1, sr)
        v_ref[pl.ds(r0, 8), ns:2 * ns] = shift(fi, 1, si)
        return fr[7:8], fi[7:8]

    sr, si = lax.fori_loop(0, v_ref.shape[0] // 8, block, (sr_ref[...], si_ref[...]))
    sr_ref[...] = sr
    si_ref[...] = si

    sp = v_ref[...].astype(BF16)
    steps = x.shape[1] // cw
    for t in range(steps):
        acc = jnp.dot(x[:, :(t + 1) * cw], k_ref[(steps - 1 - t) * cw:, :],
                      preferred_element_type=F32)
        acc = acc + jnp.dot(sp, p_ref[t], preferred_element_type=F32)
        y = acc + d_ref[0] * x[:, t * cw:(t + 1) * cw].astype(F32)
        o_ref[0, :, t * cw:(t + 1) * cw] = jax.nn.gelu(y).astype(BF16)


def _s5_core(u_v, ops, d_skip, *, batch, seq):
    wc, kc, pc, apr, api = ops
    L = SCAN_CHUNK
    cw = MXU_WIDTH
    ncb = u_v.shape[0]
    rows_per_b = seq // L
    halves = rows_per_b // SCAN_ROWS
    ns = apr.shape[2]
    blk = pl.BlockSpec((1, SCAN_ROWS, L * cw), lambda cb, b, h: (cb, b * halves + h, 0))
    return pl.pallas_call(
        _s5_kernel,
        grid=(ncb, batch, halves),
        in_specs=[blk,
                  pl.BlockSpec((1,) + wc.shape[1:], lambda cb, b, h: (cb, 0, 0, 0, 0)),
                  pl.BlockSpec((1,) + kc.shape[1:], lambda cb, b, h: (cb, 0, 0, 0)),
                  pl.BlockSpec((1,) + pc.shape[1:], lambda cb, b, h: (cb, 0, 0, 0)),
                  pl.BlockSpec((1, 8, ns), lambda cb, b, h: (cb, 0, 0)),
                  pl.BlockSpec((1, 8, ns), lambda cb, b, h: (cb, 0, 0)),
                  pl.BlockSpec((1, 1, cw), lambda cb, b, h: (cb, 0, 0))],
        out_specs=blk,
        out_shape=jax.ShapeDtypeStruct(u_v.shape, BF16),
        scratch_shapes=[pltpu.VMEM((L * cw, 2 * ns), BF16),
                        pltpu.VMEM((L * cw, cw), BF16),
                        pltpu.VMEM((L, 2 * ns, cw), BF16),
                        pltpu.VMEM((SCAN_ROWS, 2 * ns), F32),
                        pltpu.VMEM((1, ns), F32),
                        pltpu.VMEM((1, ns), F32)],
        compiler_params=_params("arbitrary", "arbitrary", "arbitrary"),
        name="s5_core",
    )(u_v, wc, kc, pc, apr, api, d_skip.reshape(ncb, 1, cw))


def _s5_operators(a_re, a_im, log_dt, b_re, b_im, c_re, c_im):
    L = SCAN_CHUNK
    groups, p = a_re.shape
    gl = MXU_WIDTH // SSM_GROUP
    ncb = groups // gl
    dt = jnp.exp(log_dt.astype(F32))[:, None]
    lr, li = a_re.astype(F32), a_im.astype(F32)
    mag = jnp.exp(lr * dt)
    ab_re = mag * jnp.cos(li * dt)
    ab_im = mag * jnp.sin(li * dt)
    den = lr * lr + li * li
    coef_re = ((ab_re - 1.0) * lr + ab_im * li) / den
    coef_im = (ab_im * lr - (ab_re - 1.0) * li) / den
    br, bi = b_re.astype(F32), b_im.astype(F32)
    bb_re = coef_re[..., None] * br - coef_im[..., None] * bi
    bb_im = coef_re[..., None] * bi + coef_im[..., None] * br
    cr, ci = c_re.astype(F32), c_im.astype(F32)

    def apow(m):
        m = jnp.asarray(m, F32)[:, None, None]
        mg = jnp.exp(m * (lr * dt))
        return mg * jnp.cos(m * (li * dt)), mg * jnp.sin(m * (li * dt))

    pr, pi = apow(np.arange(L))
    ca_re = cr[None] * pr[:, :, None, :] - ci[None] * pi[:, :, None, :]
    ca_im = cr[None] * pi[:, :, None, :] + ci[None] * pr[:, :, None, :]
    km = jnp.einsum("mgap,gpc->mgca", ca_re, bb_re) - jnp.einsum("mgap,gpc->mgca", ca_im, bb_im)
    km = km[::-1].reshape(L, ncb, gl, SSM_GROUP, SSM_GROUP)
    kc = km.transpose(1, 0, 3, 2, 4).reshape(ncb, L, SSM_GROUP, MXU_WIDTH)

    pr, pi = apow(np.arange(L - 1, -1, -1))
    w_re = pr[..., None] * bb_re[None] - pi[..., None] * bb_im[None]
    w_im = pr[..., None] * bb_im[None] + pi[..., None] * bb_re[None]
    wc = jnp.stack([w_re, w_im], axis=1).transpose(0, 1, 2, 4, 3)
    wc = wc.reshape(L, 2, ncb, MXU_WIDTH, p).transpose(2, 0, 1, 3, 4)
    wc = jnp.concatenate([wc] * (LANES // p), axis=-1)

    pr, pi = apow(np.arange(1, L + 1))
    q_re = cr[None] * pr[:, :, None, :] - ci[None] * pi[:, :, None, :]
    q_im = cr[None] * pi[:, :, None, :] + ci[None] * pr[:, :, None, :]
    pc = jnp.stack([q_re, -q_im], axis=1).reshape(L, 2, ncb, gl, SSM_GROUP, p)
    pc = pc.transpose(2, 0, 1, 5, 3, 4).reshape(ncb, L, 2 * p, MXU_WIDTH)

    pr, pi = apow(L * np.arange(1, 9))
    apr = pr.reshape(8, ncb, gl * p).transpose(1, 0, 2)
    api = pi.reshape(8, ncb, gl * p).transpose(1, 0, 2)
    return wc.astype(BF16), kc.astype(BF16), pc.astype(BF16), apr, api


def _bucket_tiles(t):
    i = np.arange(t)[:, None]
    j = np.arange(t)[None, :]
    tiles = []
    for off in (0, t):
        rel = j + off - i
        n = np.maximum(rel, 0)
        max_exact = NUM_BUCKETS // 2
        large = max_exact + (np.log(np.maximum(n, max_exact).astype(np.float32) / max_exact)
                             / math.log(MAX_DISTANCE / max_exact)
                             * (NUM_BUCKETS - max_exact)).astype(np.int32)
        large = np.minimum(large, NUM_BUCKETS - 1)
        b = np.where(n < max_exact, n, large)
        tiles.append(np.where(rel >= 0, b, -1).astype(np.int32))
    return np.stack(tiles)


def _bias_kernel(bkt_ref, tab_ref, o_ref):
    h = pl.program_id(0)
    far = tab_ref[NUM_BUCKETS - 1, h]
    for typ in range(2):
        b = bkt_ref[typ]
        acc = jnp.zeros(b.shape, F32)
        for i in range(NUM_BUCKETS - 1):
            acc = jnp.where(b == i, (tab_ref[i, h] - far) * LOG2E, acc)
        o_ref[0, typ] = jnp.where(b < 0, NEG_INF, acc)


def _bias_tiles(rel_bias, t):
    heads = rel_bias.shape[1]
    return pl.pallas_call(
        _bias_kernel,
        grid=(heads,),
        in_specs=[_const_spec((2, t, t)), pl.BlockSpec(memory_space=pltpu.SMEM)],
        out_specs=pl.BlockSpec((1, 2, t, t), lambda h: (h, 0, 0, 0)),
        out_shape=jax.ShapeDtypeStruct((heads, 2, t, t), F32),
        compiler_params=_params("parallel"),
        name="rel_bias_tiles",
    )(jnp.asarray(_bucket_tiles(t)), rel_bias.astype(F32))


def _flash_kernel(q_ref, k_ref, vt_ref, bias_ref, lam_ref, hn_ref, o_ref,
                  qcat_ref, sa_ref, sb_ref, m_ref, l_ref, acc_ref, *, lam_init):
    t = q_ref.shape[2]
    cw = MXU_WIDTH
    per_branch = t // cw
    ncol = 2 * per_branch
    qi = pl.program_id(2)
    qt = q_ref[0]
    row = lax.broadcasted_iota(jnp.int32, qt.shape, 0)
    zero = jnp.zeros_like(qt)
    qcat_ref[:, :t] = jnp.where(row < HEAD_DIM, qt, zero)
    qcat_ref[:, t:] = jnp.where(row >= HEAD_DIM, qt, zero)

    m_ref[...] = jnp.full(m_ref.shape, NEG_INF, F32)
    l_ref[...] = jnp.zeros_like(l_ref)
    acc_ref[...] = jnp.zeros_like(acc_ref)

    def scores(j, dst_ref, want_max):
        kblk = k_ref[0, pl.ds(pl.multiple_of(j * t, t), t), :]
        cmax = []
        for c in range(ncol):
            s = jnp.dot(kblk, qcat_ref[:, c * cw:(c + 1) * cw], preferred_element_type=F32)
            dst_ref[c] = s
            if want_max:
                cmax.append(jnp.max(s, axis=0, keepdims=True))
        return tuple(cmax)

    def softmax_pv(j, src_ref, cmax, bias_idx):
        vblk = vt_ref[0, 0, j]
        for c in range(ncol):
            if bias_idx is None:
                s, cm = src_ref[c], cmax[c]
            else:
                lo = (c % per_branch) * cw
                s = src_ref[c] + bias_ref[0, bias_idx, :, lo:lo + cw]
                cm = jnp.max(s, axis=0, keepdims=True)
            m_old = m_ref[c]
            m_new = jnp.maximum(m_old, cm)
            alpha = jnp.exp2(m_old - m_new)
            p = jnp.exp2(s - m_new)
            l_ref[c] = alpha * l_ref[c] + jnp.sum(p, axis=0, keepdims=True)
            m_ref[c] = m_new
            acc_ref[c] = alpha * acc_ref[c] + jnp.dot(vblk, p.astype(BF16),
                                                      preferred_element_type=F32)

    n_far = jnp.maximum(qi - 1, 0)
    n_pairs = n_far // 2
    cm_a = scores(0, sa_ref, True)

    def pair(i, cm_a):
        j = 2 * i
        cm_b = scores(j + 1, sb_ref, True)
        softmax_pv(j, sa_ref, cm_a, None)
        cm_next = scores(j + 2, sa_ref, True)
        softmax_pv(j + 1, sb_ref, cm_b, None)
        return cm_next

    cm_a = lax.fori_loop(0, n_pairs, pair, cm_a)
    odd_far = n_far - 2 * n_pairs

    @pl.when(qi == 0)
    def _():
        softmax_pv(0, sa_ref, None, 0)

    @pl.when(jnp.logical_and(qi >= 1, odd_far == 0))
    def _():
        scores(qi, sb_ref, False)
        softmax_pv(qi - 1, sa_ref, None, 1)
        softmax_pv(qi, sb_ref, None, 0)

    @pl.when(jnp.logical_and(qi >= 1, odd_far == 1))
    def _():
        scores(qi - 1, sb_ref, False)
        softmax_pv(qi - 2, sa_ref, cm_a, None)
        scores(qi, sa_ref, False)
        softmax_pv(qi - 1, sb_ref, None, 1)
        softmax_pv(qi, sa_ref, None, 0)

    lv = lam_ref[...]
    lam = (jnp.exp(jnp.sum(lv[0:1] * lv[1:2], axis=1, keepdims=True))
           - jnp.exp(jnp.sum(lv[2:3] * lv[3:4], axis=1, keepdims=True)) + lam_init)
    for c in range(per_branch):
        o1 = acc_ref[c] * (1.0 / l_ref[c])
        o2 = acc_ref[per_branch + c] * (1.0 / l_ref[per_branch + c])
        o = o1 - lam * o2
        o = o * lax.rsqrt(jnp.mean(o * o, axis=0, keepdims=True) + NORM_EPS)
        o = o * hn_ref[...] * (1.0 - lam_init)
        o_ref[0, c * cw:(c + 1) * cw, :] = o.T.astype(BF16)


def _diff_attention(qt, k, vt, bias, lam_vecs, head_norm, *, lam_init):
    batch, d, seq = qt.shape
    heads = d // V_DIM
    t = ATTN_TILE
    nblk = seq // t
    ncol = 2 * t // MXU_WIDTH
    lam_pad = jnp.zeros((8, V_DIM), F32).at[:4, :HEAD_DIM].set(lam_vecs.astype(F32))
    return pl.pallas_call(
        functools.partial(_flash_kernel, lam_init=lam_init),
        grid=(batch, heads, nblk),
        in_specs=[pl.BlockSpec((1, V_DIM, t), lambda b, h, i: (b, h, i)),
                  pl.BlockSpec((1, seq, V_DIM), lambda b, h, i: (b, 0, h)),
                  pl.BlockSpec((1, 1, nblk, V_DIM, t), lambda b, h, i: (b, h, 0, 0, 0)),
                  pl.BlockSpec((1, 2, t, t), lambda b, h, i: (h, 0, 0, 0)),
                  _const_spec((8, V_DIM)),
                  _const_spec((V_DIM, 1))],
        out_specs=pl.BlockSpec((1, t, V_DIM), lambda b, h, i: (b, i, h)),
        out_shape=jax.ShapeDtypeStruct((batch, seq, d), BF16),
        scratch_shapes=[pltpu.VMEM((V_DIM, 2 * t), BF16),
                        pltpu.VMEM((ncol, t, MXU_WIDTH), F32),
                        pltpu.VMEM((ncol, t, MXU_WIDTH), F32),
                        pltpu.VMEM((ncol, 1, MXU_WIDTH), F32),
                        pltpu.VMEM((ncol, 1, MXU_WIDTH), F32),
                        pltpu.VMEM((ncol, V_DIM, MXU_WIDTH), F32)],
        compiler_params=_params("parallel", "parallel", "arbitrary"),
        name="diff_attention",
    )(qt, k.reshape(batch, seq, d), vt, bias, lam_pad, head_norm.astype(F32).reshape(V_DIM, 1))


def kernel(x, norm_mixer_pre, norm_mixer_post, norm_mlp_pre, norm_mlp_post, mlp_w_up, mlp_w_down,
           ssm_w_in, ssm_a_re, ssm_a_im, ssm_log_dt, ssm_b_re, ssm_b_im, ssm_c_re, ssm_c_im, ssm_d,
           ssm_w_glu, kv_norm, w_kv, attn_w_q, attn_lambda_q1, attn_lambda_k1, attn_lambda_q2,
           attn_lambda_k2, attn_head_norm, attn_w_o, rel_bias):
    batch, seq, d = x.shape
    depth = norm_mixer_pre.shape[0]
    n_ssm = ssm_w_in.shape[0]
    xs = x.reshape(batch * seq, d).astype(F32)
    bias = k = vt = None
    for layer in range(depth):
        if layer < n_ssm:
            i = layer
            ops = _s5_operators(ssm_a_re[i], ssm_a_im[i], ssm_log_dt[i], ssm_b_re[i], ssm_b_im[i],
                                ssm_c_re[i], ssm_c_im[i])
            u = _norm_proj(xs, norm_mixer_pre[layer], ssm_w_in[i].astype(BF16),
                           mode="colblock", batch=batch, seq=seq)
            yg = _s5_core(u, ops, ssm_d[i].astype(F32), batch=batch, seq=seq)
            xs = _out_proj(yg, ssm_w_glu[i].astype(BF16), xs, norm_mixer_post[layer],
                           glu=True, chunked=True)
        else:
            if layer == n_ssm:
                k, vt = _norm_proj(xs, kv_norm, w_kv.astype(BF16), mode="kv", batch=batch, seq=seq)
                bias = _bias_tiles(rel_bias, ATTN_TILE)
            j = layer - n_ssm
            lam_init = 0.8 - 0.6 * math.exp(-0.3 * layer)
            qt = _norm_proj(xs, norm_mixer_pre[layer], attn_w_q[j].astype(BF16), mode="qT",
                            batch=batch, seq=seq, scale=HEAD_DIM ** -0.5 * LOG2E)
            lam_vecs = jnp.stack([attn_lambda_q1[j], attn_lambda_k1[j],
                                  attn_lambda_q2[j], attn_lambda_k2[j]])
            o = _diff_attention(qt, k, vt, bias, lam_vecs, attn_head_norm[j], lam_init=lam_init)
            xs = _out_proj(o.reshape(batch * seq, d), attn_w_o[j].astype(BF16), xs,
                           norm_mixer_post[layer], glu=False, chunked=False)
        xs = _mlp(xs, norm_mlp_pre[layer], mlp_w_up[layer].astype(BF16),
                  mlp_w_down[layer].astype(BF16), norm_mlp_post[layer])
    return xs.reshape(batch, seq, d).astype(x.dtype)
```

```python
import functools
import math

import numpy as np
import jax
import jax.numpy as jnp
from jax import lax
from jax.experimental import pallas as pl
from jax.experimental.pallas import tpu as pltpu

F32 = jnp.float32
BF16 = jnp.bfloat16

SSM_GROUP = 16
SSM_STATE = 64
HEAD_DIM = 64
V_DIM = 2 * HEAD_DIM
NUM_BUCKETS = 32
MAX_DISTANCE = 128
NORM_EPS = 1e-6
NEG_INF = -1e30
LOG2E = math.log2(math.e)

MXU_WIDTH = 256
LANES = 128
ROW_TILE = 512
SCAN_CHUNK = 8
SCAN_ROWS = 512
ATTN_TILE = 512
ATTN_QTILE = 1024
FF_CHUNK = 1024
VMEM_LIMIT = 56 * 1024 * 1024


def _params(*sem, flags=None):
    return pltpu.CompilerParams(dimension_semantics=sem, vmem_limit_bytes=VMEM_LIMIT, flags=flags)


def _rms(xf, gain):
    return xf * lax.rsqrt(jnp.mean(xf * xf, axis=-1, keepdims=True) + NORM_EPS) * gain


def _const_spec(shape):
    zeros = (0,) * len(shape)
    return pl.BlockSpec(shape, lambda *_: zeros)


def _proj_kernel(x_ref, g_ref, w_ref, *out_refs, mode, scale):
    hn = _rms(x_ref[...], g_ref[...]).astype(BF16)
    acc = jnp.dot(hn, w_ref[...], preferred_element_type=F32)
    rows = acc.shape[0]
    if mode == "colblock":
        o_ref, slab_ref = out_refs
        L = SCAN_CHUNK
        for sl in range(slab_ref.shape[0]):
            slab_ref[sl] = acc[:, sl * LANES:(sl + 1) * LANES]
        for cb in range(o_ref.shape[0]):
            for t in range(L):
                for hf in range(MXU_WIDTH // LANES):
                    lo = t * MXU_WIDTH + hf * LANES
                    piece = slab_ref[cb * (MXU_WIDTH // LANES) + hf, pl.ds(t, rows // L, stride=L), :]
                    o_ref[cb, :, lo:lo + LANES] = piece.astype(BF16)
    elif mode == "qT":
        (o_ref,) = out_refs
        o_ref[0] = (acc * scale).T.astype(BF16)
    else:
        k_ref, vt_ref = out_refs
        d = k_ref.shape[1]
        k_ref[...] = acc[:, :d].astype(BF16)
        vt = acc[:, d:].T.astype(BF16)
        vt_ref[0, :, 0] = vt.reshape(d // V_DIM, V_DIM, rows)


def _norm_proj(x, gain, w, *, mode, batch, seq, scale=1.0):
    n, d = x.shape
    nout = w.shape[1]
    tm = ROW_TILE
    per_b = seq // tm
    scratch = []
    if mode == "colblock":
        ncb = nout // MXU_WIDTH
        L = SCAN_CHUNK
        out_shape = jax.ShapeDtypeStruct((ncb, n // L, L * MXU_WIDTH), BF16)
        out_specs = pl.BlockSpec((ncb, tm // L, L * MXU_WIDTH), lambda i: (0, i, 0))
        scratch = [pltpu.VMEM((nout // LANES, tm, LANES), F32)]
    elif mode == "qT":
        out_shape = jax.ShapeDtypeStruct((batch, nout, seq), BF16)
        out_specs = pl.BlockSpec((1, nout, tm), lambda i: (i // per_b, 0, i % per_b))
    else:
        heads = d // V_DIM
        assert tm == ATTN_TILE
        out_shape = (jax.ShapeDtypeStruct((n, d), BF16),
                     jax.ShapeDtypeStruct((batch, heads, per_b, V_DIM, tm), BF16))
        out_specs = (pl.BlockSpec((tm, d), lambda i: (i, 0)),
                     pl.BlockSpec((1, heads, 1, V_DIM, tm),
                                  lambda i: (i // per_b, 0, i % per_b, 0, 0)))
    return pl.pallas_call(
        functools.partial(_proj_kernel, mode=mode, scale=scale),
        grid=(n // tm,),
        in_specs=[pl.BlockSpec((tm, d), lambda i: (i, 0)),
                  _const_spec((1, d)),
                  _const_spec((d, nout))],
        out_specs=out_specs,
        out_shape=out_shape,
        scratch_shapes=scratch,
        compiler_params=_params("parallel"),
        name="norm_proj_" + mode,
    )(x, gain.reshape(1, d), w)


def _out_kernel(y_ref, w_ref, x_ref, g_ref, o_ref, *scratch, glu, chunked):
    if chunked:
        slab_ref, tok_ref = scratch
        L = SCAN_CHUNK
        nchunk = y_ref.shape[1]
        for cb in range(y_ref.shape[0]):
            for t in range(L):
                for hf in range(MXU_WIDTH // LANES):
                    lo = t * MXU_WIDTH + hf * LANES
                    slab_ref[cb * (MXU_WIDTH // LANES) + hf, pl.ds(t, nchunk, stride=L), :] = (
                        y_ref[cb, :, lo:lo + LANES].astype(F32))
        for sl in range(slab_ref.shape[0]):
            tok_ref[:, sl * LANES:(sl + 1) * LANES] = slab_ref[sl].astype(BF16)
        acc = jnp.dot(tok_ref[...], w_ref[...], preferred_element_type=F32)
    else:
        acc = jnp.dot(y_ref[...], w_ref[...], preferred_element_type=F32)
    if glu:
        d = o_ref.shape[1]
        acc = acc[:, :d] * jax.nn.sigmoid(acc[:, d:])
    o_ref[...] = x_ref[...] + _rms(acc, g_ref[...])


def _out_proj(y, w, x, gain, *, glu, chunked):
    n, d = x.shape
    tm = ROW_TILE
    scratch = []
    if chunked:
        ncb, _, width = y.shape
        y_spec = pl.BlockSpec((ncb, tm // SCAN_CHUNK, width), lambda i: (0, i, 0))
        scratch = [pltpu.VMEM((d // LANES, tm, LANES), F32), pltpu.VMEM((tm, d), BF16)]
    else:
        y_spec = pl.BlockSpec((tm, y.shape[1]), lambda i: (i, 0))
    return pl.pallas_call(
        functools.partial(_out_kernel, glu=glu, chunked=chunked),
        grid=(n // tm,),
        in_specs=[y_spec, _const_spec(w.shape),
                  pl.BlockSpec((tm, d), lambda i: (i, 0)), _const_spec((1, d))],
        out_specs=pl.BlockSpec((tm, d), lambda i: (i, 0)),
        out_shape=jax.ShapeDtypeStruct((n, d), F32),
        scratch_shapes=scratch,
        compiler_params=_params("parallel"),
        name="out_proj_glu" if glu else "out_proj",
    )(y, w, x, gain.reshape(1, d))


def _mlp_kernel(x_ref, g1_ref, wu_ref, wd_ref, g2_ref, o_ref):
    x = x_ref[...]
    hn = _rms(x, g1_ref[...]).astype(BF16)
    acc = None
    for c in range(wu_ref.shape[1] // FF_CHUNK):
        a = jnp.dot(hn, wu_ref[:, c * FF_CHUNK:(c + 1) * FF_CHUNK], preferred_element_type=F32)
        a = jnp.maximum(a, 0.0)
        a = (a * a).astype(BF16)
        part = jnp.dot(a, wd_ref[c * FF_CHUNK:(c + 1) * FF_CHUNK, :], preferred_element_type=F32)
        acc = part if acc is None else acc + part
    o_ref[...] = x + _rms(acc, g2_ref[...])


def _mlp(x, g_pre, w_up, w_down, g_post):
    n, d = x.shape
    dff = w_up.shape[1]
    tm = ROW_TILE
    return pl.pallas_call(
        _mlp_kernel,
        grid=(n // tm,),
        in_specs=[pl.BlockSpec((tm, d), lambda i: (i, 0)), _const_spec((1, d)),
                  _const_spec((d, dff)), _const_spec((dff, d)), _const_spec((1, d))],
        out_specs=pl.BlockSpec((tm, d), lambda i: (i, 0)),
        out_shape=jax.ShapeDtypeStruct((n, d), F32),
        compiler_params=_params("parallel"),
        name="mlp",
    )(x, g_pre.reshape(1, d), w_up, w_down, g_post.reshape(1, d))


def _s5_kernel(u_ref, wc_ref, kc_ref, pc_ref, apr_ref, api_ref, d_ref, o_ref,
               w_ref, k_ref, p_ref, v_ref, sr_ref, si_ref):
    cw = MXU_WIDTH
    ns = v_ref.shape[1] // 2
    L = u_ref.shape[2] // cw
    grp_shift = SSM_GROUP.bit_length() - 1
    st_shift = SSM_STATE.bit_length() - 1

    @pl.when(jnp.logical_and(pl.program_id(1) == 0, pl.program_id(2) == 0))
    def _():
        def iota(shape, dim):
            return lax.broadcasted_iota(jnp.int32, shape, dim)

        wmask = (iota((cw, ns), 0) >> grp_shift) == (iota((cw, ns), 1) >> st_shift)
        pmask = (iota((ns, cw), 0) >> st_shift) == (iota((ns, cw), 1) >> grp_shift)
        kmask = (iota((cw, cw), 0) >> grp_shift) == (iota((cw, cw), 1) >> grp_shift)
        for t in range(L):
            for h in range(2):
                piece = wc_ref[0, t, h].astype(F32)
                tiled = jnp.concatenate([piece] * (ns // LANES), axis=1)
                w_ref[t * cw:(t + 1) * cw, h * ns:(h + 1) * ns] = (
                    jnp.where(wmask, tiled, 0.0).astype(BF16))
                piece = pc_ref[0, t, h * SSM_STATE:(h + 1) * SSM_STATE, :].astype(F32)
                tiled = jnp.concatenate([piece] * (ns // SSM_STATE), axis=0)
                p_ref[t, h * ns:(h + 1) * ns, :] = jnp.where(pmask, tiled, 0.0).astype(BF16)
            piece = kc_ref[0, t].astype(F32)
            tiled = jnp.concatenate([piece] * (cw // SSM_GROUP), axis=0)
            k_ref[t * cw:(t + 1) * cw, :] = jnp.where(kmask, tiled, 0.0).astype(BF16)

    @pl.when(pl.program_id(2) == 0)
    def _():
        sr_ref[...] = jnp.zeros_like(sr_ref)
        si_ref[...] = jnp.zeros_like(si_ref)

    x = u_ref[0]
    v_ref[...] = jnp.dot(x, w_ref[...], preferred_element_type=F32)

    apr = apr_ref[0]
    api = api_ref[0]
    row = lax.broadcasted_iota(jnp.int32, (8, ns), 0)

    def shift(z, k, fill):
        return jnp.where(row >= k, pltpu.roll(z, k, 0), fill)

    def block(i, carry):
        sr, si = carry
        r0 = pl.multiple_of(i * 8, 8)
        zr = v_ref[pl.ds(r0, 8), 0:ns]
        zi = v_ref[pl.ds(r0, 8), ns:2 * ns]
        for k in (1, 2, 4):
            ar, ai = apr[k - 1:k], api[k - 1:k]
            hr, hi = shift(zr, k, 0.0), shift(zi, k, 0.0)
            zr, zi = zr + ar * hr - ai * hi, zi + ar * hi + ai * hr
        fr = zr + apr * sr - api * si
        fi = zi + apr * si + api * sr
        v_ref[pl.ds(r0, 8), 0:ns] = shift(fr, 1, sr)
        v_ref[pl.ds(r0, 8), ns:2 * ns] = shift(fi, 1, si)
        return fr[7:8], fi[7:8]

    sr, si = lax.fori_loop(0, v_ref.shape[0] // 8, block, (sr_ref[...], si_ref[...]))
    sr_ref[...] = sr
    si_ref[...] = si

    sp = v_ref[...].astype(BF16)
    steps = x.shape[1] // cw
    for t in range(steps):
        acc = jnp.dot(x[:, :(t + 1) * cw], k_ref[(steps - 1 - t) * cw:, :],
                      preferred_element_type=F32)
        acc = acc + jnp.dot(sp, p_ref[t], preferred_element_type=F32)
        y = acc + d_ref[0] * x[:, t * cw:(t + 1) * cw].astype(F32)
        o_ref[0, :, t * cw:(t + 1) * cw] = jax.nn.gelu(y).astype(BF16)


def _s5_core(u_v, ops, d_skip, *, batch, seq):
    wc, kc, pc, apr, api = ops
    L = SCAN_CHUNK
    cw = MXU_WIDTH
    ncb = u_v.shape[0]
    rows_per_b = seq // L
    halves = rows_per_b // SCAN_ROWS
    ns = apr.shape[2]
    blk = pl.BlockSpec((1, SCAN_ROWS, L * cw), lambda cb, b, h: (cb, b * halves + h, 0))
    return pl.pallas_call(
        _s5_kernel,
        grid=(ncb, batch, halves),
        in_specs=[blk,
                  pl.BlockSpec((1,) + wc.shape[1:], lambda cb, b, h: (cb, 0, 0, 0, 0)),
                  pl.BlockSpec((1,) + kc.shape[1:], lambda cb, b, h: (cb, 0, 0, 0)),
                  pl.BlockSpec((1,) + pc.shape[1:], lambda cb, b, h: (cb, 0, 0, 0)),
                  pl.BlockSpec((1, 8, ns), lambda cb, b, h: (cb, 0, 0)),
                  pl.BlockSpec((1, 8, ns), lambda cb, b, h: (cb, 0, 0)),
                  pl.BlockSpec((1, 1, cw), lambda cb, b, h: (cb, 0, 0))],
        out_specs=blk,
        out_shape=jax.ShapeDtypeStruct(u_v.shape, BF16),
        scratch_shapes=[pltpu.VMEM((L * cw, 2 * ns), BF16),
                        pltpu.VMEM((L * cw, cw), BF16),
                        pltpu.VMEM((L, 2 * ns, cw), BF16),
                        pltpu.VMEM((SCAN_ROWS, 2 * ns), F32),
                        pltpu.VMEM((1, ns), F32),
                        pltpu.VMEM((1, ns), F32)],
        compiler_params=_params("arbitrary", "arbitrary", "arbitrary"),
        name="s5_core",
    )(u_v, wc, kc, pc, apr, api, d_skip.reshape(ncb, 1, cw))


def _s5_operators(a_re, a_im, log_dt, b_re, b_im, c_re, c_im):
    L = SCAN_CHUNK
    groups, p = a_re.shape
    gl = MXU_WIDTH // SSM_GROUP
    ncb = groups // gl
    dt = jnp.exp(log_dt.astype(F32))[:, None]
    lr, li = a_re.astype(F32), a_im.astype(F32)
    mag = jnp.exp(lr * dt)
    ab_re = mag * jnp.cos(li * dt)
    ab_im = mag * jnp.sin(li * dt)
    den = lr * lr + li * li
    coef_re = ((ab_re - 1.0) * lr + ab_im * li) / den
    coef_im = (ab_im * lr - (ab_re - 1.0) * li) / den
    br, bi = b_re.astype(F32), b_im.astype(F32)
    bb_re = coef_re[..., None] * br - coef_im[..., None] * bi
    bb_im = coef_re[..., None] * bi + coef_im[..., None] * br
    cr, ci = c_re.astype(F32), c_im.astype(F32)

    def apow(m):
        m = jnp.asarray(m, F32)[:, None, None]
        mg = jnp.exp(m * (lr * dt))
        return mg * jnp.cos(m * (li * dt)), mg * jnp.sin(m * (li * dt))

    pr, pi = apow(np.arange(L))
    ca_re = cr[None] * pr[:, :, None, :] - ci[None] * pi[:, :, None, :]
    ca_im = cr[None] * pi[:, :, None, :] + ci[None] * pr[:, :, None, :]
    km = jnp.einsum("mgap,gpc->mgca", ca_re, bb_re) - jnp.einsum("mgap,gpc->mgca", ca_im, bb_im)
    km = km[::-1].reshape(L, ncb, gl, SSM_GROUP, SSM_GROUP)
    kc = km.transpose(1, 0, 3, 2, 4).reshape(ncb, L, SSM_GROUP, MXU_WIDTH)

    pr, pi = apow(np.arange(L - 1, -1, -1))
    w_re = pr[..., None] * bb_re[None] - pi[..., None] * bb_im[None]
    w_im = pr[..., None] * bb_im[None] + pi[..., None] * bb_re[None]
    wc = jnp.stack([w_re, w_im], axis=1).transpose(0, 1, 2, 4, 3)
    wc = wc.reshape(L, 2, ncb, MXU_WIDTH, p).transpose(2, 0, 1, 3, 4)
    wc = jnp.concatenate([wc] * (LANES // p), axis=-1)

    pr, pi = apow(np.arange(1, L + 1))
    q_re = cr[None] * pr[:, :, None, :] - ci[None] * pi[:, :, None, :]
    q_im = cr[None] * pi[:, :, None, :] + ci[None] * pr[:, :, None, :]
    pc = jnp.stack([q_re, -q_im], axis=1).reshape(L, 2, ncb, gl, SSM_GROUP, p)
    pc = pc.transpose(2, 0, 1, 5, 3, 4).reshape(ncb, L, 2 * p, MXU_WIDTH)

    pr, pi = apow(L * np.arange(1, 9))
    apr = pr.reshape(8, ncb, gl * p).transpose(1, 0, 2)
    api = pi.reshape(8, ncb, gl * p).transpose(1, 0, 2)
    return wc.astype(BF16), kc.astype(BF16), pc.astype(BF16), apr, api


def _bucket_tiles(t):
    i = np.arange(t)[:, None]
    j = np.arange(t)[None, :]
    tiles = []
    for off in (0, t):
        rel = j + off - i
        n = np.maximum(rel, 0)
        max_exact = NUM_BUCKETS // 2
        large = max_exact + (np.log(np.maximum(n, max_exact).astype(np.float32) / max_exact)
                             / math.log(MAX_DISTANCE / max_exact)
                             * (NUM_BUCKETS - max_exact)).astype(np.int32)
        large = np.minimum(large, NUM_BUCKETS - 1)
        b = np.where(n < max_exact, n, large)
        tiles.append(np.where(rel >= 0, b, -1).astype(np.int32))
    return np.stack(tiles)


def _bias_kernel(bkt_ref, tab_ref, o_ref):
    h = pl.program_id(0)
    far = tab_ref[NUM_BUCKETS - 1, h]
    for typ in range(2):
        b = bkt_ref[typ]
        acc = jnp.zeros(b.shape, F32)
        for i in range(NUM_BUCKETS - 1):
            acc = jnp.where(b == i, (tab_ref[i, h] - far) * LOG2E, acc)
        o_ref[0, typ] = jnp.where(b < 0, NEG_INF, acc)


def _bias_tiles(rel_bias, t):
    heads = rel_bias.shape[1]
    return pl.pallas_call(
        _bias_kernel,
        grid=(heads,),
        in_specs=[_const_spec((2, t, t)), pl.BlockSpec(memory_space=pltpu.SMEM)],
        out_specs=pl.BlockSpec((1, 2, t, t), lambda h: (h, 0, 0, 0)),
        out_shape=jax.ShapeDtypeStruct((heads, 2, t, t), F32),
        compiler_params=_params("parallel"),
        name="rel_bias_tiles",
    )(jnp.asarray(_bucket_tiles(t)), rel_bias.astype(F32))


def _flash_kernel(q_ref, k_ref, vt_ref, bias_ref, lam_ref, hn_ref, o_ref,
                  qcat_ref, sa_ref, sb_ref, m_ref, l_ref, acc_ref, *, lam_init):
    tq = q_ref.shape[2]
    tk = bias_ref.shape[2]
    cw = MXU_WIDTH
    per_branch = tq // cw
    ncol = 2 * per_branch
    per_key = tk // cw
    inner = tq // tk
    assert inner == 2, "the far-tile loop below relies on an odd number of far tiles"
    qi = pl.program_id(2)
    qt = q_ref[0]
    row = lax.broadcasted_iota(jnp.int32, qt.shape, 0)
    zero = jnp.zeros_like(qt)
    qcat_ref[:, :tq] = jnp.where(row < HEAD_DIM, qt, zero)
    qcat_ref[:, tq:] = jnp.where(row >= HEAD_DIM, qt, zero)

    m_ref[...] = jnp.full(m_ref.shape, NEG_INF, F32)
    l_ref[...] = jnp.zeros_like(l_ref)
    acc_ref[...] = jnp.zeros_like(acc_ref)

    def bias_of(kind, c):
        if kind is None:
            return None
        cq = c % per_branch
        dist = cq // per_key - kind
        if dist < 0:
            return "skip"
        if dist > 1:
            return None
        return dist, (cq % per_key) * cw

    def scores(j, dst_ref, kind):
        kblk = k_ref[0, pl.ds(pl.multiple_of(j * tk, tk), tk), :]
        cmax = []
        for c in range(ncol):
            if bias_of(kind, c) == "skip":
                continue
            s = jnp.dot(kblk, qcat_ref[:, c * cw:(c + 1) * cw], preferred_element_type=F32)
            dst_ref[c] = s
            if kind is None:
                cmax.append(jnp.max(s, axis=0, keepdims=True))
        return tuple(cmax)

    def softmax_pv(j, src_ref, cmax, kind):
        vblk = vt_ref[0, 0, j]
        for c in range(ncol):
            spec = bias_of(kind, c)
            if spec == "skip":
                continue
            if kind is None:
                s, cm = src_ref[c], cmax[c]
            else:
                s = src_ref[c]
                if spec is not None:
                    s = s + bias_ref[0, spec[0], :, spec[1]:spec[1] + cw]
                cm = jnp.max(s, axis=0, keepdims=True)
            m_old = m_ref[c]
            m_new = jnp.maximum(m_old, cm)
            alpha = jnp.exp2(m_old - m_new)
            p = jnp.exp2(s - m_new)
            l_ref[c] = alpha * l_ref[c] + jnp.sum(p, axis=0, keepdims=True)
            m_ref[c] = m_new
            acc_ref[c] = alpha * acc_ref[c] + jnp.dot(vblk, p.astype(BF16),
                                                      preferred_element_type=F32)

    def run(tiles, first_max=None):
        bufs = (sa_ref, sb_ref)
        cmax = first_max
        for n, (j, kind) in enumerate(tiles):
            nxt = None
            if n + 1 < len(tiles):
                nxt = scores(tiles[n + 1][0], bufs[(n + 1) % 2], tiles[n + 1][1])
            softmax_pv(j, bufs[n % 2], cmax, kind)
            cmax = nxt

    first_diag = qi * inner
    diag = [(first_diag + r, r) for r in range(inner)]

    @pl.when(qi == 0)
    def _():
        scores(0, sa_ref, 0)
        run(diag)

    @pl.when(qi >= 1)
    def _():
        n_far = first_diag - 1

        def pair(i, cm_a):
            j = 2 * i
            cm_b = scores(j + 1, sb_ref, None)
            softmax_pv(j, sa_ref, cm_a, None)
            cm_next = scores(j + 2, sa_ref, None)
            softmax_pv(j + 1, sb_ref, cm_b, None)
            return cm_next

        cm_a = lax.fori_loop(0, (n_far - 1) // 2, pair, scores(0, sa_ref, None))
        run([(n_far - 1, None), (n_far, -1)] + diag, cm_a)

    lv = lam_ref[...]
    lam = (jnp.exp(jnp.sum(lv[0:1] * lv[1:2], axis=1, keepdims=True))
           - jnp.exp(jnp.sum(lv[2:3] * lv[3:4], axis=1, keepdims=True)) + lam_init)
    for c in range(per_branch):
        o1 = acc_ref[c] * (1.0 / l_ref[c])
        o2 = acc_ref[per_branch + c] * (1.0 / l_ref[per_branch + c])
        o = o1 - lam * o2
        o = o * lax.rsqrt(jnp.mean(o * o, axis=0, keepdims=True) + NORM_EPS)
        o = o * hn_ref[...] * (1.0 - lam_init)
        o_ref[0, c * cw:(c + 1) * cw, :] = o.T.astype(BF16)


def _diff_attention(qt, k, vt, bias, lam_vecs, head_norm, *, lam_init):
    batch, d, seq = qt.shape
    heads = d // V_DIM
    t = ATTN_TILE
    tq = ATTN_QTILE
    nblk = seq // t
    ncol = 2 * tq // MXU_WIDTH
    lam_pad = jnp.zeros((8, V_DIM), F32).at[:4, :HEAD_DIM].set(lam_vecs.astype(F32))
    return pl.pallas_call(
        functools.partial(_flash_kernel, lam_init=lam_init),
        grid=(batch, heads, seq // tq),
        in_specs=[pl.BlockSpec((1, V_DIM, tq), lambda b, h, i: (b, h, i)),
                  pl.BlockSpec((1, seq, V_DIM), lambda b, h, i: (b, 0, h)),
                  pl.BlockSpec((1, 1, nblk, V_DIM, t), lambda b, h, i: (b, h, 0, 0, 0)),
                  pl.BlockSpec((1, 2, t, t), lambda b, h, i: (h, 0, 0, 0)),
                  _const_spec((8, V_DIM)),
                  _const_spec((V_DIM, 1))],
        out_specs=pl.BlockSpec((1, tq, V_DIM), lambda b, h, i: (b, i, h)),
        out_shape=jax.ShapeDtypeStruct((batch, seq, d), BF16),
        scratch_shapes=[pltpu.VMEM((V_DIM, 2 * tq), BF16),
                        pltpu.VMEM((ncol, t, MXU_WIDTH), F32),
                        pltpu.VMEM((ncol, t, MXU_WIDTH), F32),
                        pltpu.VMEM((ncol, 1, MXU_WIDTH), F32),
                        pltpu.VMEM((ncol, 1, MXU_WIDTH), F32),
                        pltpu.VMEM((ncol, V_DIM, MXU_WIDTH), F32)],
        compiler_params=_params("parallel", "parallel", "arbitrary"),
        name="diff_attention",
    )(qt, k.reshape(batch, seq, d), vt, bias, lam_pad, head_norm.astype(F32).reshape(V_DIM, 1))


def kernel(x, norm_mixer_pre, norm_mixer_post, norm_mlp_pre, norm_mlp_post, mlp_w_up, mlp_w_down,
           ssm_w_in, ssm_a_re, ssm_a_im, ssm_log_dt, ssm_b_re, ssm_b_im, ssm_c_re, ssm_c_im, ssm_d,
           ssm_w_glu, kv_norm, w_kv, attn_w_q, attn_lambda_q1, attn_lambda_k1, attn_lambda_q2,
           attn_lambda_k2, attn_head_norm, attn_w_o, rel_bias):
    batch, seq, d = x.shape
    depth = norm_mixer_pre.shape[0]
    n_ssm = ssm_w_in.shape[0]
    xs = x.reshape(batch * seq, d).astype(F32)
    bias = k = vt = None
    for layer in range(depth):
        if layer < n_ssm:
            i = layer
            ops = _s5_operators(ssm_a_re[i], ssm_a_im[i], ssm_log_dt[i], ssm_b_re[i], ssm_b_im[i],
                                ssm_c_re[i], ssm_c_im[i])
            u = _norm_proj(xs, norm_mixer_pre[layer], ssm_w_in[i].astype(BF16),
                           mode="colblock", batch=batch, seq=seq)
            yg = _s5_core(u, ops, ssm_d[i].astype(F32), batch=batch, seq=seq)
            xs = _out_proj(yg, ssm_w_glu[i].astype(BF16), xs, norm_mixer_post[layer],
                           glu=True, chunked=True)
        else:
            if layer == n_ssm:
                k, vt = _norm_proj(xs, kv_norm, w_kv.astype(BF16), mode="kv", batch=batch, seq=seq)
                bias = _bias_tiles(rel_bias, ATTN_TILE)
            j = layer - n_ssm
            lam_init = 0.8 - 0.6 * math.exp(-0.3 * layer)
            qt = _norm_proj(xs, norm_mixer_pre[layer], attn_w_q[j].astype(BF16), mode="qT",
                            batch=batch, seq=seq, scale=HEAD_DIM ** -0.5 * LOG2E)
            lam_vecs = jnp.stack([attn_lambda_q1[j], attn_lambda_k1[j],
                                  attn_lambda_q2[j], attn_lambda_k2[j]])
            o = _diff_attention(qt, k, vt, bias, lam_vecs, attn_head_norm[j], lam_init=lam_init)
            xs = _out_proj(o.reshape(batch * seq, d), attn_w_o[j].astype(BF16), xs,
                           norm_mixer_post[layer], glu=False, chunked=False)
        xs = _mlp(xs, norm_mlp_pre[layer], mlp_w_up[layer].astype(BF16),
                  mlp_w_down[layer].astype(BF16), norm_mlp_post[layer])
    return xs.reshape(batch, seq, d).astype(x.dtype)
```

```python
import functools
import math

import numpy as np
import jax
import jax.numpy as jnp
from jax import lax
from jax.experimental import pallas as pl
from jax.experimental.pallas import tpu as pltpu

F32 = jnp.float32
BF16 = jnp.bfloat16

SSM_GROUP = 16
SSM_STATE = 64
HEAD_DIM = 64
V_DIM = 2 * HEAD_DIM
NUM_BUCKETS = 32
MAX_DISTANCE = 128
NORM_EPS = 1e-6
NEG_INF = -1e30
LOG2E = math.log2(math.e)

MXU_WIDTH = 256
LANES = 128
ROW_TILE = 512
SCAN_CHUNK = 8
SCAN_ROWS = 512
ATTN_TILE = 512
ATTN_QTILE = 1024
FF_CHUNK = 1024
VMEM_LIMIT = 56 * 1024 * 1024


def _params(*sem, flags=None):
    return pltpu.CompilerParams(dimension_semantics=sem, vmem_limit_bytes=VMEM_LIMIT, flags=flags)


def _rms(xf, gain):
    return xf * lax.rsqrt(jnp.mean(xf * xf, axis=-1, keepdims=True) + NORM_EPS) * gain


def _const_spec(shape):
    zeros = (0,) * len(shape)
    return pl.BlockSpec(shape, lambda *_: zeros, pipeline_mode=pl.Buffered(1))


def _project(x, g_ref, w_ref, out_refs, slab_ref, mode, scale):
    hn = _rms(x, g_ref[...]).astype(BF16)
    acc = jnp.dot(hn, w_ref[...], preferred_element_type=F32)
    rows = acc.shape[0]
    if mode == "colblock":
        (o_ref,) = out_refs
        L = SCAN_CHUNK
        for sl in range(slab_ref.shape[0]):
            slab_ref[sl] = acc[:, sl * LANES:(sl + 1) * LANES]
        for cb in range(o_ref.shape[0]):
            for t in range(L):
                for hf in range(MXU_WIDTH // LANES):
                    lo = t * MXU_WIDTH + hf * LANES
                    piece = slab_ref[cb * (MXU_WIDTH // LANES) + hf, pl.ds(t, rows // L, stride=L), :]
                    o_ref[cb, :, lo:lo + LANES] = piece.astype(BF16)
    elif mode == "qT":
        (o_ref,) = out_refs
        o_ref[0] = (acc * scale).T.astype(BF16)
    else:
        k_ref, vt_ref = out_refs
        d = k_ref.shape[1]
        k_ref[...] = acc[:, :d].astype(BF16)
        vt = acc[:, d:].T.astype(BF16)
        vt_ref[0, :, 0] = vt.reshape(d // V_DIM, V_DIM, rows)


def _proj_outputs(mode, n, d, nout, batch, seq):
    tm = ROW_TILE
    per_b = seq // tm
    if mode == "colblock":
        ncb = nout // MXU_WIDTH
        L = SCAN_CHUNK
        return ([jax.ShapeDtypeStruct((ncb, n // L, L * MXU_WIDTH), BF16)],
                [pl.BlockSpec((ncb, tm // L, L * MXU_WIDTH), lambda i: (0, i, 0))])
    if mode == "qT":
        return ([jax.ShapeDtypeStruct((batch, nout, seq), BF16)],
                [pl.BlockSpec((1, nout, tm), lambda i: (i // per_b, 0, i % per_b))])
    heads = d // V_DIM
    assert mode == "kv" and tm == ATTN_TILE
    return ([jax.ShapeDtypeStruct((n, d), BF16),
             jax.ShapeDtypeStruct((batch, heads, per_b, V_DIM, tm), BF16)],
            [pl.BlockSpec((tm, d), lambda i: (i, 0)),
             pl.BlockSpec((1, heads, 1, V_DIM, tm), lambda i: (i // per_b, 0, i % per_b, 0, 0))])


def _slab_scratch(d):
    return [pltpu.VMEM((d // LANES, ROW_TILE, LANES), F32), pltpu.VMEM((ROW_TILE, d), BF16)]


def _proj_kernel(x_ref, g_ref, w_ref, *refs, mode, scale):
    slab_ref = refs[-2] if mode == "colblock" else None
    nout = 2 if mode == "kv" else 1
    _project(x_ref[...], g_ref, w_ref, refs[:nout], slab_ref, mode, scale)


def _norm_proj(x, gain, w, *, mode, batch, seq, scale=1.0):
    n, d = x.shape
    nout = w.shape[1]
    tm = ROW_TILE
    out_shape, out_specs = _proj_outputs(mode, n, d, nout, batch, seq)
    out = pl.pallas_call(
        functools.partial(_proj_kernel, mode=mode, scale=scale),
        grid=(n // tm,),
        in_specs=[pl.BlockSpec((tm, d), lambda i: (i, 0)),
                  _const_spec((1, d)),
                  _const_spec((d, nout))],
        out_specs=out_specs,
        out_shape=out_shape,
        scratch_shapes=_slab_scratch(nout) if mode == "colblock" else [],
        compiler_params=_params("parallel"),
        name="norm_proj_" + mode,
    )(x, gain.reshape(1, d), w)
    return out[0] if len(out) == 1 else out


def _mixer_out(y_ref, w_ref, slab_ref, tok_ref, *, glu, chunked, d):
    if chunked:
        L = SCAN_CHUNK
        nchunk = y_ref.shape[1]
        for cb in range(y_ref.shape[0]):
            for t in range(L):
                for hf in range(MXU_WIDTH // LANES):
                    lo = t * MXU_WIDTH + hf * LANES
                    slab_ref[cb * (MXU_WIDTH // LANES) + hf, pl.ds(t, nchunk, stride=L), :] = (
                        y_ref[cb, :, lo:lo + LANES].astype(F32))
        for sl in range(slab_ref.shape[0]):
            tok_ref[:, sl * LANES:(sl + 1) * LANES] = slab_ref[sl].astype(BF16)
        acc = jnp.dot(tok_ref[...], w_ref[...], preferred_element_type=F32)
    else:
        acc = jnp.dot(y_ref[...], w_ref[...], preferred_element_type=F32)
    if glu:
        acc = acc[:, :d] * jax.nn.sigmoid(acc[:, d:])
    return acc


def _mlp_block(x, g1_ref, wu_ref, wd_ref, g2_ref):
    hn = _rms(x, g1_ref[...]).astype(BF16)
    acc = None
    for c in range(wu_ref.shape[1] // FF_CHUNK):
        a = jnp.dot(hn, wu_ref[:, c * FF_CHUNK:(c + 1) * FF_CHUNK], preferred_element_type=F32)
        a = jnp.maximum(a, 0.0)
        a = (a * a).astype(BF16)
        part = jnp.dot(a, wd_ref[c * FF_CHUNK:(c + 1) * FF_CHUNK, :], preferred_element_type=F32)
        acc = part if acc is None else acc + part
    return x + _rms(acc, g2_ref[...])


def _tail_kernel(*refs, glu, chunked, projs, use_slab):
    n_in = 8 + 2 * len(projs)
    y_ref, wo_ref, x_ref, gpost_ref, gpre_ref, wu_ref, wd_ref, gmlp_ref = refs[:8]
    proj_in = refs[8:n_in]
    n_out = 1 + sum(2 if mode == "kv" else 1 for mode, _ in projs)
    o_ref = refs[n_in]
    proj_out = refs[n_in + 1:n_in + n_out]
    slab_ref, tok_ref = refs[n_in + n_out:] if use_slab else (None, None)
    d = o_ref.shape[1]
    acc = _mixer_out(y_ref, wo_ref, slab_ref, tok_ref, glu=glu, chunked=chunked, d=d)
    x = x_ref[...] + _rms(acc, gpost_ref[...])
    x = _mlp_block(x, gpre_ref, wu_ref, wd_ref, gmlp_ref)
    o_ref[...] = x
    k = 0
    for i, (mode, scale) in enumerate(projs):
        cnt = 2 if mode == "kv" else 1
        _project(x, proj_in[2 * i], proj_in[2 * i + 1], proj_out[k:k + cnt], slab_ref, mode, scale)
        k += cnt


def _layer_tail(y, w_out, x, g_post, g_pre, w_up, w_down, g_mlp, projs, *, glu, chunked, batch, seq):
    n, d = x.shape
    tm = ROW_TILE
    row = pl.BlockSpec((tm, d), lambda i: (i, 0))
    if chunked:
        ncb, _, width = y.shape
        y_spec = pl.BlockSpec((ncb, tm // SCAN_CHUNK, width), lambda i: (0, i, 0))
    else:
        y_spec = pl.BlockSpec((tm, y.shape[1]), lambda i: (i, 0))
    in_specs = [y_spec, _const_spec(w_out.shape), row, _const_spec((1, d)), _const_spec((1, d)),
                _const_spec(w_up.shape), _const_spec(w_down.shape), _const_spec((1, d))]
    args = [y, w_out, x, g_post.reshape(1, d), g_pre.reshape(1, d), w_up, w_down, g_mlp.reshape(1, d)]
    out_shape = [jax.ShapeDtypeStruct((n, d), F32)]
    out_specs = [row]
    for mode, gain, w, _ in projs:
        in_specs += [_const_spec((1, d)), _const_spec(w.shape)]
        args += [gain.reshape(1, d), w]
        shapes, specs = _proj_outputs(mode, n, d, w.shape[1], batch, seq)
        out_shape += shapes
        out_specs += specs
    use_slab = chunked or any(mode == "colblock" for mode, *_ in projs)
    out = pl.pallas_call(
        functools.partial(_tail_kernel, glu=glu, chunked=chunked,
                          projs=tuple((mode, scale) for mode, _, _, scale in projs), use_slab=use_slab),
        grid=(n // tm,),
        in_specs=in_specs,
        out_specs=out_specs,
        out_shape=out_shape,
        scratch_shapes=_slab_scratch(d) if use_slab else [],
        compiler_params=_params("parallel"),
        name="layer_tail",
    )(*args)
    return out[0], list(out[1:])


def _s5_kernel(u_ref, wc_ref, kc_ref, pc_ref, apr_ref, api_ref, d_ref, o_ref,
               w_ref, k_ref, p_ref, v_ref, sr_ref, si_ref):
    cw = MXU_WIDTH
    ns = v_ref.shape[1] // 2
    L = u_ref.shape[2] // cw
    grp_shift = SSM_GROUP.bit_length() - 1
    st_shift = SSM_STATE.bit_length() - 1

    @pl.when(jnp.logical_and(pl.program_id(1) == 0, pl.program_id(2) == 0))
    def _():
        def iota(shape, dim):
            return lax.broadcasted_iota(jnp.int32, shape, dim)

        wmask = (iota((cw, ns), 0) >> grp_shift) == (iota((cw, ns), 1) >> st_shift)
        pmask = (iota((ns, cw), 0) >> st_shift) == (iota((ns, cw), 1) >> grp_shift)
        kmask = (iota((cw, cw), 0) >> grp_shift) == (iota((cw, cw), 1) >> grp_shift)
        for t in range(L):
            for h in range(2):
                piece = wc_ref[0, t, h].astype(F32)
                tiled = jnp.concatenate([piece] * (ns // LANES), axis=1)
                w_ref[t * cw:(t + 1) * cw, h * ns:(h + 1) * ns] = (
                    jnp.where(wmask, tiled, 0.0).astype(BF16))
                piece = pc_ref[0, t, h * SSM_STATE:(h + 1) * SSM_STATE, :].astype(F32)
                tiled = jnp.concatenate([piece] * (ns // SSM_STATE), axis=0)
                p_ref[t, h * ns:(h + 1) * ns, :] = jnp.where(pmask, tiled, 0.0).astype(BF16)
            piece = kc_ref[0, t].astype(F32)
            tiled = jnp.concatenate([piece] * (cw // SSM_GROUP), axis=0)
            k_ref[t * cw:(t + 1) * cw, :] = jnp.where(kmask, tiled, 0.0).astype(BF16)

    @pl.when(pl.program_id(2) == 0)
    def _():
        sr_ref[...] = jnp.zeros_like(sr_ref)
        si_ref[...] = jnp.zeros_like(si_ref)

    x = u_ref[0]
    v_ref[...] = jnp.dot(x, w_ref[...], preferred_element_type=F32)

    apr = apr_ref[0]
    api = api_ref[0]
    row = lax.broadcasted_iota(jnp.int32, (8, ns), 0)

    def shift(z, k, fill):
        return jnp.where(row >= k, pltpu.roll(z, k, 0), fill)

    def block(i, carry):
        sr, si = carry
        r0 = pl.multiple_of(i * 8, 8)
        zr = v_ref[pl.ds(r0, 8), 0:ns]
        zi = v_ref[pl.ds(r0, 8), ns:2 * ns]
        for k in (1, 2, 4):
            ar, ai = apr[k - 1:k], api[k - 1:k]
            hr, hi = shift(zr, k, 0.0), shift(zi, k, 0.0)
            zr, zi = zr + ar * hr - ai * hi, zi + ar * hi + ai * hr
        fr = zr + apr * sr - api * si
        fi = zi + apr * si + api * sr
        v_ref[pl.ds(r0, 8), 0:ns] = shift(fr, 1, sr)
        v_ref[pl.ds(r0, 8), ns:2 * ns] = shift(fi, 1, si)
        return fr[7:8], fi[7:8]

    sr, si = lax.fori_loop(0, v_ref.shape[0] // 8, block, (sr_ref[...], si_ref[...]))
    sr_ref[...] = sr
    si_ref[...] = si

    sp = v_ref[...].astype(BF16)
    steps = x.shape[1] // cw
    for t in range(steps):
        acc = jnp.dot(x[:, :(t + 1) * cw], k_ref[(steps - 1 - t) * cw:, :],
                      preferred_element_type=F32)
        acc = acc + jnp.dot(sp, p_ref[t], preferred_element_type=F32)
        y = acc + d_ref[0] * x[:, t * cw:(t + 1) * cw].astype(F32)
        o_ref[0, :, t * cw:(t + 1) * cw] = jax.nn.gelu(y).astype(BF16)


def _s5_core(u_v, ops, d_skip, *, batch, seq):
    wc, kc, pc, apr, api = ops
    L = SCAN_CHUNK
    cw = MXU_WIDTH
    ncb = u_v.shape[0]
    rows_per_b = seq // L
    halves = rows_per_b // SCAN_ROWS
    ns = apr.shape[2]
    blk = pl.BlockSpec((1, SCAN_ROWS, L * cw), lambda cb, b, h: (cb, b * halves + h, 0))
    return pl.pallas_call(
        _s5_kernel,
        grid=(ncb, batch, halves),
        in_specs=[blk,
                  pl.BlockSpec((1,) + wc.shape[1:], lambda cb, b, h: (cb, 0, 0, 0, 0)),
                  pl.BlockSpec((1,) + kc.shape[1:], lambda cb, b, h: (cb, 0, 0, 0)),
                  pl.BlockSpec((1,) + pc.shape[1:], lambda cb, b, h: (cb, 0, 0, 0)),
                  pl.BlockSpec((1, 8, ns), lambda cb, b, h: (cb, 0, 0)),
                  pl.BlockSpec((1, 8, ns), lambda cb, b, h: (cb, 0, 0)),
                  pl.BlockSpec((1, 1, cw), lambda cb, b, h: (cb, 0, 0))],
        out_specs=blk,
        out_shape=jax.ShapeDtypeStruct(u_v.shape, BF16),
        scratch_shapes=[pltpu.VMEM((L * cw, 2 * ns), BF16),
                        pltpu.VMEM((L * cw, cw), BF16),
                        pltpu.VMEM((L, 2 * ns, cw), BF16),
                        pltpu.VMEM((SCAN_ROWS, 2 * ns), F32),
                        pltpu.VMEM((1, ns), F32),
                        pltpu.VMEM((1, ns), F32)],
        compiler_params=_params("arbitrary", "arbitrary", "arbitrary"),
        name="s5_core",
    )(u_v, wc, kc, pc, apr, api, d_skip.reshape(ncb, 1, cw))


def _s5_operators(a_re, a_im, log_dt, b_re, b_im, c_re, c_im):
    L = SCAN_CHUNK
    groups, p = a_re.shape
    gl = MXU_WIDTH // SSM_GROUP
    ncb = groups // gl
    dt = jnp.exp(log_dt.astype(F32))[:, None]
    lr, li = a_re.astype(F32), a_im.astype(F32)
    mag = jnp.exp(lr * dt)
    ab_re = mag * jnp.cos(li * dt)
    ab_im = mag * jnp.sin(li * dt)
    den = lr * lr + li * li
    coef_re = ((ab_re - 1.0) * lr + ab_im * li) / den
    coef_im = (ab_im * lr - (ab_re - 1.0) * li) / den
    br, bi = b_re.astype(F32), b_im.astype(F32)
    bb_re = coef_re[..., None] * br - coef_im[..., None] * bi
    bb_im = coef_re[..., None] * bi + coef_im[..., None] * br
    cr, ci = c_re.astype(F32), c_im.astype(F32)

    def apow(m):
        m = jnp.asarray(m, F32)[:, None, None]
        mg = jnp.exp(m * (lr * dt))
        return mg * jnp.cos(m * (li * dt)), mg * jnp.sin(m * (li * dt))

    pr, pi = apow(np.arange(L))
    ca_re = cr[None] * pr[:, :, None, :] - ci[None] * pi[:, :, None, :]
    ca_im = cr[None] * pi[:, :, None, :] + ci[None] * pr[:, :, None, :]
    km = jnp.einsum("mgap,gpc->mgca", ca_re, bb_re) - jnp.einsum("mgap,gpc->mgca", ca_im, bb_im)
    km = km[::-1].reshape(L, ncb, gl, SSM_GROUP, SSM_GROUP)
    kc = km.transpose(1, 0, 3, 2, 4).reshape(ncb, L, SSM_GROUP, MXU_WIDTH)

    pr, pi = apow(np.arange(L - 1, -1, -1))
    w_re = pr[..., None] * bb_re[None] - pi[..., None] * bb_im[None]
    w_im = pr[..., None] * bb_im[None] + pi[..., None] * bb_re[None]
    wc = jnp.stack([w_re, w_im], axis=1).transpose(0, 1, 2, 4, 3)
    wc = wc.reshape(L, 2, ncb, MXU_WIDTH, p).transpose(2, 0, 1, 3, 4)
    wc = jnp.concatenate([wc] * (LANES // p), axis=-1)

    pr, pi = apow(np.arange(1, L + 1))
    q_re = cr[None] * pr[:, :, None, :] - ci[None] * pi[:, :, None, :]
    q_im = cr[None] * pi[:, :, None, :] + ci[None] * pr[:, :, None, :]
    pc = jnp.stack([q_re, -q_im], axis=1).reshape(L, 2, ncb, gl, SSM_GROUP, p)
    pc = pc.transpose(2, 0, 1, 5, 3, 4).reshape(ncb, L, 2 * p, MXU_WIDTH)

    pr, pi = apow(L * np.arange(1, 9))
    apr = pr.reshape(8, ncb, gl * p).transpose(1, 0, 2)
    api = pi.reshape(8, ncb, gl * p).transpose(1, 0, 2)
    return wc.astype(BF16), kc.astype(BF16), pc.astype(BF16), apr, api


def _bucket_tiles(t):
    i = np.arange(t)[:, None]
    j = np.arange(t)[None, :]
    tiles = []
    for off in (0, t):
        rel = j + off - i
        n = np.maximum(rel, 0)
        max_exact = NUM_BUCKETS // 2
        large = max_exact + (np.log(np.maximum(n, max_exact).astype(np.float32) / max_exact)
                             / math.log(MAX_DISTANCE / max_exact)
                             * (NUM_BUCKETS - max_exact)).astype(np.int32)
        large = np.minimum(large, NUM_BUCKETS - 1)
        b = np.where(n < max_exact, n, large)
        tiles.append(np.where(rel >= 0, b, -1).astype(np.int32))
    return np.stack(tiles)


def _bias_kernel(bkt_ref, tab_ref, o_ref):
    h = pl.program_id(0)
    far = tab_ref[NUM_BUCKETS - 1, h]
    for typ in range(2):
        b = bkt_ref[typ]
        acc = jnp.zeros(b.shape, F32)
        for i in range(NUM_BUCKETS - 1):
            acc = jnp.where(b == i, (tab_ref[i, h] - far) * LOG2E, acc)
        o_ref[0, typ] = jnp.where(b < 0, NEG_INF, acc)


def _bias_tiles(rel_bias, t):
    heads = rel_bias.shape[1]
    return pl.pallas_call(
        _bias_kernel,
        grid=(heads,),
        in_specs=[_const_spec((2, t, t)), pl.BlockSpec(memory_space=pltpu.SMEM)],
        out_specs=pl.BlockSpec((1, 2, t, t), lambda h: (h, 0, 0, 0)),
        out_shape=jax.ShapeDtypeStruct((heads, 2, t, t), F32),
        compiler_params=_params("parallel"),
        name="rel_bias_tiles",
    )(jnp.asarray(_bucket_tiles(t)), rel_bias.astype(F32))


def _flash_kernel(q_ref, k_ref, vt_ref, bias_ref, lam_ref, hn_ref, o_ref,
                  qcat_ref, sa_ref, sb_ref, m_ref, l_ref, acc_ref, *, lam_init):
    tq = q_ref.shape[2]
    tk = bias_ref.shape[2]
    cw = MXU_WIDTH
    per_branch = tq // cw
    ncol = 2 * per_branch
    per_key = tk // cw
    inner = tq // tk
    assert inner == 2, "the far-tile loop below relies on an odd number of far tiles"
    qi = pl.program_id(2)
    qt = q_ref[0]
    row = lax.broadcasted_iota(jnp.int32, qt.shape, 0)
    zero = jnp.zeros_like(qt)
    qcat_ref[:, :tq] = jnp.where(row < HEAD_DIM, qt, zero)
    qcat_ref[:, tq:] = jnp.where(row >= HEAD_DIM, qt, zero)

    m_ref[...] = jnp.full(m_ref.shape, NEG_INF, F32)
    l_ref[...] = jnp.zeros_like(l_ref)
    acc_ref[...] = jnp.zeros_like(acc_ref)

    def bias_of(kind, c):
        if kind is None:
            return None
        cq = c % per_branch
        dist = cq // per_key - kind
        if dist < 0:
            return "skip"
        if dist > 1:
            return None
        return dist, (cq % per_key) * cw

    def scores(j, dst_ref, kind):
        kblk = k_ref[0, pl.ds(pl.multiple_of(j * tk, tk), tk), :]
        cmax = []
        for c in range(ncol):
            if bias_of(kind, c) == "skip":
                continue
            s = jnp.dot(kblk, qcat_ref[:, c * cw:(c + 1) * cw], preferred_element_type=F32)
            dst_ref[c] = s
            if kind is None:
                cmax.append(jnp.max(s, axis=0, keepdims=True))
        return tuple(cmax)

    def softmax_pv(j, src_ref, cmax, kind):
        vblk = vt_ref[0, 0, j]
        for c in range(ncol):
            spec = bias_of(kind, c)
            if spec == "skip":
                continue
            if kind is None:
                s, cm = src_ref[c], cmax[c]
            else:
                s = src_ref[c]
                if spec is not None:
                    s = s + bias_ref[0, spec[0], :, spec[1]:spec[1] + cw]
                cm = jnp.max(s, axis=0, keepdims=True)
            m_old = m_ref[c]
            m_new = jnp.maximum(m_old, cm)
            alpha = jnp.exp2(m_old - m_new)
            p = jnp.exp2(s - m_new)
            l_ref[c] = alpha * l_ref[c] + jnp.sum(p, axis=0, keepdims=True)
            m_ref[c] = m_new
            acc_ref[c] = alpha * acc_ref[c] + jnp.dot(vblk, p.astype(BF16),
                                                      preferred_element_type=F32)

    def run(tiles, first_max=None):
        bufs = (sa_ref, sb_ref)
        cmax = first_max
        for n, (j, kind) in enumerate(tiles):
            nxt = None
            if n + 1 < len(tiles):
                nxt = scores(tiles[n + 1][0], bufs[(n + 1) % 2], tiles[n + 1][1])
            softmax_pv(j, bufs[n % 2], cmax, kind)
            cmax = nxt

    first_diag = qi * inner
    diag = [(first_diag + r, r) for r in range(inner)]

    @pl.when(qi == 0)
    def _():
        scores(0, sa_ref, 0)
        run(diag)

    @pl.when(qi >= 1)
    def _():
        n_far = first_diag - 1

        def pair(i, cm_a):
            j = 2 * i
            cm_b = scores(j + 1, sb_ref, None)
            softmax_pv(j, sa_ref, cm_a, None)
            cm_next = scores(j + 2, sa_ref, None)
            softmax_pv(j + 1, sb_ref, cm_b, None)
            return cm_next

        cm_a = lax.fori_loop(0, (n_far - 1) // 2, pair, scores(0, sa_ref, None))
        run([(n_far - 1, None), (n_far, -1)] + diag, cm_a)

    lv = lam_ref[...]
    lam = (jnp.exp(jnp.sum(lv[0:1] * lv[1:2], axis=1, keepdims=True))
           - jnp.exp(jnp.sum(lv[2:3] * lv[3:4], axis=1, keepdims=True)) + lam_init)
    for c in range(per_branch):
        o1 = acc_ref[c] * (1.0 / l_ref[c])
        o2 = acc_ref[per_branch + c] * (1.0 / l_ref[per_branch + c])
        o = o1 - lam * o2
        o = o * lax.rsqrt(jnp.mean(o * o, axis=0, keepdims=True) + NORM_EPS)
        o = o * hn_ref[...] * (1.0 - lam_init)
        o_ref[0, c * cw:(c + 1) * cw, :] = o.T.astype(BF16)


def _diff_attention(qt, k, vt, bias, lam_vecs, head_norm, *, lam_init):
    batch, d, seq = qt.shape
    heads = d // V_DIM
    t = ATTN_TILE
    tq = ATTN_QTILE
    nblk = seq // t
    ncol = 2 * tq // MXU_WIDTH
    lam_pad = jnp.zeros((8, V_DIM), F32).at[:4, :HEAD_DIM].set(lam_vecs.astype(F32))
    return pl.pallas_call(
        functools.partial(_flash_kernel, lam_init=lam_init),
        grid=(batch, heads, seq // tq),
        in_specs=[pl.BlockSpec((1, V_DIM, tq), lambda b, h, i: (b, h, i)),
                  pl.BlockSpec((1, seq, V_DIM), lambda b, h, i: (b, 0, h)),
                  pl.BlockSpec((1, 1, nblk, V_DIM, t), lambda b, h, i: (b, h, 0, 0, 0)),
                  pl.BlockSpec((1, 2, t, t), lambda b, h, i: (h, 0, 0, 0)),
                  _const_spec((8, V_DIM)),
                  _const_spec((V_DIM, 1))],
        out_specs=pl.BlockSpec((1, tq, V_DIM), lambda b, h, i: (b, i, h)),
        out_shape=jax.ShapeDtypeStruct((batch, seq, d), BF16),
        scratch_shapes=[pltpu.VMEM((V_DIM, 2 * tq), BF16),
                        pltpu.VMEM((ncol, t, MXU_WIDTH), F32),
                        pltpu.VMEM((ncol, t, MXU_WIDTH), F32),
                        pltpu.VMEM((ncol, 1, MXU_WIDTH), F32),
                        pltpu.VMEM((ncol, 1, MXU_WIDTH), F32),
                        pltpu.VMEM((ncol, V_DIM, MXU_WIDTH), F32)],
        compiler_params=_params("parallel", "parallel", "arbitrary"),
        name="diff_attention",
    )(qt, k.reshape(batch, seq, d), vt, bias, lam_pad, head_norm.astype(F32).reshape(V_DIM, 1))


def kernel(x, norm_mixer_pre, norm_mixer_post, norm_mlp_pre, norm_mlp_post, mlp_w_up, mlp_w_down,
           ssm_w_in, ssm_a_re, ssm_a_im, ssm_log_dt, ssm_b_re, ssm_b_im, ssm_c_re, ssm_c_im, ssm_d,
           ssm_w_glu, kv_norm, w_kv, attn_w_q, attn_lambda_q1, attn_lambda_k1, attn_lambda_q2,
           attn_lambda_k2, attn_head_norm, attn_w_o, rel_bias):
    batch, seq, d = x.shape
    depth = norm_mixer_pre.shape[0]
    n_ssm = ssm_w_in.shape[0]
    assert 0 < n_ssm < depth
    xs = x.reshape(batch * seq, d).astype(F32)
    q_scale = HEAD_DIM ** -0.5 * LOG2E

    def mixer_projections(layer):
        if layer >= depth:
            return []
        if layer < n_ssm:
            return [("colblock", norm_mixer_pre[layer], ssm_w_in[layer].astype(BF16), 1.0)]
        projs = []
        if layer == n_ssm:
            projs.append(("kv", kv_norm, w_kv.astype(BF16), 1.0))
        projs.append(("qT", norm_mixer_pre[layer], attn_w_q[layer - n_ssm].astype(BF16), q_scale))
        return projs

    (_, gain, w, _), = mixer_projections(0)
    feeds = [_norm_proj(xs, gain, w, mode="colblock", batch=batch, seq=seq)]
    bias = _bias_tiles(rel_bias, ATTN_TILE)
    k = vt = None
    for layer in range(depth):
        if layer < n_ssm:
            i = layer
            ops = _s5_operators(ssm_a_re[i], ssm_a_im[i], ssm_log_dt[i], ssm_b_re[i], ssm_b_im[i],
                                ssm_c_re[i], ssm_c_im[i])
            y = _s5_core(feeds[0], ops, ssm_d[i].astype(F32), batch=batch, seq=seq)
            w_out = ssm_w_glu[i].astype(BF16)
        else:
            if layer == n_ssm:
                k, vt = feeds[0], feeds[1]
            j = layer - n_ssm
            lam_init = 0.8 - 0.6 * math.exp(-0.3 * layer)
            lam_vecs = jnp.stack([attn_lambda_q1[j], attn_lambda_k1[j],
                                  attn_lambda_q2[j], attn_lambda_k2[j]])
            y = _diff_attention(feeds[-1], k, vt, bias, lam_vecs, attn_head_norm[j], lam_init=lam_init)
            y = y.reshape(batch * seq, d)
            w_out = attn_w_o[j].astype(BF16)
        xs, feeds = _layer_tail(y, w_out, xs, norm_mixer_post[layer], norm_mlp_pre[layer],
                                mlp_w_up[layer].astype(BF16), mlp_w_down[layer].astype(BF16),
                                norm_mlp_post[layer], mixer_projections(layer + 1),
                                glu=layer < n_ssm, chunked=layer < n_ssm, batch=batch, seq=seq)
    return xs.reshape(batch, seq, d).astype(x.dtype)
```

```python
import functools
import math

import numpy as np
import jax
import jax.numpy as jnp
from jax import lax
from jax.experimental import pallas as pl
from jax.experimental.pallas import tpu as pltpu

F32 = jnp.float32
BF16 = jnp.bfloat16

SSM_GROUP = 16
SSM_STATE = 64
HEAD_DIM = 64
V_DIM = 2 * HEAD_DIM
NUM_BUCKETS = 32
MAX_DISTANCE = 128
NORM_EPS = 1e-6
NEG_INF = -1e30
LOG2E = math.log2(math.e)

MXU_WIDTH = 256
LANES = 128
ROW_TILE = 512
SCAN_CHUNK = 8
SCAN_ROWS = 512
ATTN_TILE = 512
ATTN_QTILE = 1024
FF_CHUNK = 1024
VMEM_LIMIT = 56 * 1024 * 1024


def _params(*sem, flags=None):
    return pltpu.CompilerParams(dimension_semantics=sem, vmem_limit_bytes=VMEM_LIMIT, flags=flags)


def _rms(xf, gain):
    return xf * lax.rsqrt(jnp.mean(xf * xf, axis=-1, keepdims=True) + NORM_EPS) * gain


def _const_spec(shape):
    zeros = (0,) * len(shape)
    return pl.BlockSpec(shape, lambda *_: zeros, pipeline_mode=pl.Buffered(1))


def _project(x, g_ref, w_ref, out_refs, slab_ref, mode, scale):
    hn = _rms(x, g_ref[...]).astype(BF16)
    acc = jnp.dot(hn, w_ref[...], preferred_element_type=F32)
    rows = acc.shape[0]
    if mode == "colblock":
        (o_ref,) = out_refs
        L = SCAN_CHUNK
        for sl in range(slab_ref.shape[0]):
            slab_ref[sl] = acc[:, sl * LANES:(sl + 1) * LANES]
        for cb in range(o_ref.shape[0]):
            for t in range(L):
                for hf in range(MXU_WIDTH // LANES):
                    lo = t * MXU_WIDTH + hf * LANES
                    piece = slab_ref[cb * (MXU_WIDTH // LANES) + hf, pl.ds(t, rows // L, stride=L), :]
                    o_ref[cb, :, lo:lo + LANES] = piece.astype(BF16)
    elif mode == "qT":
        (o_ref,) = out_refs
        o_ref[0] = (acc * scale).T.astype(BF16)
    else:
        k_ref, vt_ref = out_refs
        d = k_ref.shape[1]
        k_ref[...] = acc[:, :d].astype(BF16)
        vt = acc[:, d:].T.astype(BF16)
        vt_ref[0, :, 0] = vt.reshape(d // V_DIM, V_DIM, rows)


def _proj_outputs(mode, n, d, nout, batch, seq):
    tm = ROW_TILE
    per_b = seq // tm
    if mode == "colblock":
        ncb = nout // MXU_WIDTH
        L = SCAN_CHUNK
        return ([jax.ShapeDtypeStruct((ncb, n // L, L * MXU_WIDTH), BF16)],
                [pl.BlockSpec((ncb, tm // L, L * MXU_WIDTH), lambda i: (0, i, 0))])
    if mode == "qT":
        return ([jax.ShapeDtypeStruct((batch, nout, seq), BF16)],
                [pl.BlockSpec((1, nout, tm), lambda i: (i // per_b, 0, i % per_b))])
    heads = d // V_DIM
    assert mode == "kv" and tm == ATTN_TILE
    return ([jax.ShapeDtypeStruct((n, d), BF16),
             jax.ShapeDtypeStruct((batch, heads, per_b, V_DIM, tm), BF16)],
            [pl.BlockSpec((tm, d), lambda i: (i, 0)),
             pl.BlockSpec((1, heads, 1, V_DIM, tm), lambda i: (i // per_b, 0, i % per_b, 0, 0))])


def _slab_scratch(d):
    return [pltpu.VMEM((d // LANES, ROW_TILE, LANES), F32), pltpu.VMEM((ROW_TILE, d), BF16)]


def _proj_kernel(x_ref, g_ref, w_ref, *refs, mode, scale):
    slab_ref = refs[-2] if mode == "colblock" else None
    nout = 2 if mode == "kv" else 1
    _project(x_ref[...], g_ref, w_ref, refs[:nout], slab_ref, mode, scale)


def _norm_proj(x, gain, w, *, mode, batch, seq, scale=1.0):
    n, d = x.shape
    nout = w.shape[1]
    tm = ROW_TILE
    out_shape, out_specs = _proj_outputs(mode, n, d, nout, batch, seq)
    out = pl.pallas_call(
        functools.partial(_proj_kernel, mode=mode, scale=scale),
        grid=(n // tm,),
        in_specs=[pl.BlockSpec((tm, d), lambda i: (i, 0)),
                  _const_spec((1, d)),
                  _const_spec((d, nout))],
        out_specs=out_specs,
        out_shape=out_shape,
        scratch_shapes=_slab_scratch(nout) if mode == "colblock" else [],
        compiler_params=_params("parallel"),
        name="norm_proj_" + mode,
    )(x, gain.reshape(1, d), w)
    return out[0] if len(out) == 1 else out


def _mixer_out(y_ref, w_ref, slab_ref, tok_ref, *, glu, chunked, d):
    if chunked:
        L = SCAN_CHUNK
        nchunk = y_ref.shape[1]
        for cb in range(y_ref.shape[0]):
            for t in range(L):
                for hf in range(MXU_WIDTH // LANES):
                    lo = t * MXU_WIDTH + hf * LANES
                    slab_ref[cb * (MXU_WIDTH // LANES) + hf, pl.ds(t, nchunk, stride=L), :] = (
                        y_ref[cb, :, lo:lo + LANES].astype(F32))
        for sl in range(slab_ref.shape[0]):
            tok_ref[:, sl * LANES:(sl + 1) * LANES] = slab_ref[sl].astype(BF16)
        acc = jnp.dot(tok_ref[...], w_ref[...], preferred_element_type=F32)
    else:
        acc = jnp.dot(y_ref[...], w_ref[...], preferred_element_type=F32)
    if glu:
        acc = acc[:, :d] * jax.nn.sigmoid(acc[:, d:])
    return acc


def _mlp_block(x, g1_ref, wu_ref, wd_ref, g2_ref):
    hn = _rms(x, g1_ref[...]).astype(BF16)
    acc = None
    for c in range(wu_ref.shape[1] // FF_CHUNK):
        a = jnp.dot(hn, wu_ref[:, c * FF_CHUNK:(c + 1) * FF_CHUNK], preferred_element_type=F32)
        a = jnp.maximum(a, 0.0)
        a = (a * a).astype(BF16)
        part = jnp.dot(a, wd_ref[c * FF_CHUNK:(c + 1) * FF_CHUNK, :], preferred_element_type=F32)
        acc = part if acc is None else acc + part
    return x + _rms(acc, g2_ref[...])


def _tail_kernel(*refs, glu, chunked, projs, use_slab):
    n_in = 8 + 2 * len(projs)
    y_ref, wo_ref, x_ref, gpost_ref, gpre_ref, wu_ref, wd_ref, gmlp_ref = refs[:8]
    proj_in = refs[8:n_in]
    n_out = 1 + sum(2 if mode == "kv" else 1 for mode, _ in projs)
    o_ref = refs[n_in]
    proj_out = refs[n_in + 1:n_in + n_out]
    slab_ref, tok_ref = refs[n_in + n_out:] if use_slab else (None, None)
    d = o_ref.shape[1]
    acc = _mixer_out(y_ref, wo_ref, slab_ref, tok_ref, glu=glu, chunked=chunked, d=d)
    x = x_ref[...] + _rms(acc, gpost_ref[...])
    x = _mlp_block(x, gpre_ref, wu_ref, wd_ref, gmlp_ref)
    o_ref[...] = x
    k = 0
    for i, (mode, scale) in enumerate(projs):
        cnt = 2 if mode == "kv" else 1
        _project(x, proj_in[2 * i], proj_in[2 * i + 1], proj_out[k:k + cnt], slab_ref, mode, scale)
        k += cnt


def _layer_tail(y, w_out, x, g_post, g_pre, w_up, w_down, g_mlp, projs, *, glu, chunked, batch, seq):
    n, d = x.shape
    tm = ROW_TILE
    row = pl.BlockSpec((tm, d), lambda i: (i, 0))
    if chunked:
        ncb, _, width = y.shape
        y_spec = pl.BlockSpec((ncb, tm // SCAN_CHUNK, width), lambda i: (0, i, 0))
    else:
        y_spec = pl.BlockSpec((tm, y.shape[1]), lambda i: (i, 0))
    in_specs = [y_spec, _const_spec(w_out.shape), row, _const_spec((1, d)), _const_spec((1, d)),
                _const_spec(w_up.shape), _const_spec(w_down.shape), _const_spec((1, d))]
    args = [y, w_out, x, g_post.reshape(1, d), g_pre.reshape(1, d), w_up, w_down, g_mlp.reshape(1, d)]
    out_shape = [jax.ShapeDtypeStruct((n, d), F32)]
    out_specs = [row]
    for mode, gain, w, _ in projs:
        in_specs += [_const_spec((1, d)), _const_spec(w.shape)]
        args += [gain.reshape(1, d), w]
        shapes, specs = _proj_outputs(mode, n, d, w.shape[1], batch, seq)
        out_shape += shapes
        out_specs += specs
    use_slab = chunked or any(mode == "colblock" for mode, *_ in projs)
    out = pl.pallas_call(
        functools.partial(_tail_kernel, glu=glu, chunked=chunked,
                          projs=tuple((mode, scale) for mode, _, _, scale in projs), use_slab=use_slab),
        grid=(n // tm,),
        in_specs=in_specs,
        out_specs=out_specs,
        out_shape=out_shape,
        scratch_shapes=_slab_scratch(d) if use_slab else [],
        compiler_params=_params("parallel"),
        name="layer_tail",
    )(*args)
    return out[0], list(out[1:])


def _s5_kernel(u_ref, wc_ref, kc_ref, pc_ref, apr_ref, api_ref, d_ref, o_ref,
               w_ref, k_ref, p_ref, v_ref, sr_ref, si_ref):
    cw = MXU_WIDTH
    ns = v_ref.shape[1] // 2
    L = u_ref.shape[2] // cw
    grp_shift = SSM_GROUP.bit_length() - 1
    st_shift = SSM_STATE.bit_length() - 1

    @pl.when(jnp.logical_and(pl.program_id(1) == 0, pl.program_id(2) == 0))
    def _():
        def iota(shape, dim):
            return lax.broadcasted_iota(jnp.int32, shape, dim)

        wmask = (iota((cw, ns), 0) >> grp_shift) == (iota((cw, ns), 1) >> st_shift)
        pmask = (iota((ns, cw), 0) >> st_shift) == (iota((ns, cw), 1) >> grp_shift)
        kmask = (iota((cw, cw), 0) >> grp_shift) == (iota((cw, cw), 1) >> grp_shift)
        for t in range(L):
            for h in range(2):
                piece = wc_ref[0, t, h].astype(F32)
                tiled = jnp.concatenate([piece] * (ns // LANES), axis=1)
                w_ref[t * cw:(t + 1) * cw, h * ns:(h + 1) * ns] = (
                    jnp.where(wmask, tiled, 0.0).astype(BF16))
                piece = pc_ref[0, t, h * SSM_STATE:(h + 1) * SSM_STATE, :].astype(F32)
                tiled = jnp.concatenate([piece] * (ns // SSM_STATE), axis=0)
                p_ref[t, h * ns:(h + 1) * ns, :] = jnp.where(pmask, tiled, 0.0).astype(BF16)
            piece = kc_ref[0, t].astype(F32)
            tiled = jnp.concatenate([piece] * (cw // SSM_GROUP), axis=0)
            k_ref[t * cw:(t + 1) * cw, :] = jnp.where(kmask, tiled, 0.0).astype(BF16)

    @pl.when(pl.program_id(2) == 0)
    def _():
        sr_ref[...] = jnp.zeros_like(sr_ref)
        si_ref[...] = jnp.zeros_like(si_ref)

    x = u_ref[0]
    v_ref[...] = jnp.dot(x, w_ref[...], preferred_element_type=F32)

    apr = apr_ref[0]
    api = api_ref[0]
    row = lax.broadcasted_iota(jnp.int32, (8, ns), 0)

    def shift(z, k, fill):
        return jnp.where(row >= k, pltpu.roll(z, k, 0), fill)

    def block(i, carry):
        sr, si = carry
        r0 = pl.multiple_of(i * 8, 8)
        zr = v_ref[pl.ds(r0, 8), 0:ns]
        zi = v_ref[pl.ds(r0, 8), ns:2 * ns]
        for k in (1, 2, 4):
            ar, ai = apr[k - 1:k], api[k - 1:k]
            hr, hi = shift(zr, k, 0.0), shift(zi, k, 0.0)
            zr, zi = zr + ar * hr - ai * hi, zi + ar * hi + ai * hr
        fr = zr + apr * sr - api * si
        fi = zi + apr * si + api * sr
        v_ref[pl.ds(r0, 8), 0:ns] = shift(fr, 1, sr)
        v_ref[pl.ds(r0, 8), ns:2 * ns] = shift(fi, 1, si)
        return fr[7:8], fi[7:8]

    sr, si = lax.fori_loop(0, v_ref.shape[0] // 8, block, (sr_ref[...], si_ref[...]))
    sr_ref[...] = sr
    si_ref[...] = si

    sp = v_ref[...].astype(BF16)
    steps = x.shape[1] // cw
    for t in range(steps):
        acc = jnp.dot(x[:, :(t + 1) * cw], k_ref[(steps - 1 - t) * cw:, :],
                      preferred_element_type=F32)
        acc = acc + jnp.dot(sp, p_ref[t], preferred_element_type=F32)
        y = acc + d_ref[0] * x[:, t * cw:(t + 1) * cw].astype(F32)
        o_ref[0, :, t * cw:(t + 1) * cw] = jax.nn.gelu(y).astype(BF16)


def _s5_core(u_v, ops, d_skip, *, batch, seq):
    wc, kc, pc, apr, api = ops
    L = SCAN_CHUNK
    cw = MXU_WIDTH
    ncb = u_v.shape[0]
    rows_per_b = seq // L
    halves = rows_per_b // SCAN_ROWS
    ns = apr.shape[2]
    blk = pl.BlockSpec((1, SCAN_ROWS, L * cw), lambda cb, b, h: (cb, b * halves + h, 0))
    return pl.pallas_call(
        _s5_kernel,
        grid=(ncb, batch, halves),
        in_specs=[blk,
                  pl.BlockSpec((1,) + wc.shape[1:], lambda cb, b, h: (cb, 0, 0, 0, 0)),
                  pl.BlockSpec((1,) + kc.shape[1:], lambda cb, b, h: (cb, 0, 0, 0)),
                  pl.BlockSpec((1,) + pc.shape[1:], lambda cb, b, h: (cb, 0, 0, 0)),
                  pl.BlockSpec((1, 8, ns), lambda cb, b, h: (cb, 0, 0)),
                  pl.BlockSpec((1, 8, ns), lambda cb, b, h: (cb, 0, 0)),
                  pl.BlockSpec((1, 1, cw), lambda cb, b, h: (cb, 0, 0))],
        out_specs=blk,
        out_shape=jax.ShapeDtypeStruct(u_v.shape, BF16),
        scratch_shapes=[pltpu.VMEM((L * cw, 2 * ns), BF16),
                        pltpu.VMEM((L * cw, cw), BF16),
                        pltpu.VMEM((L, 2 * ns, cw), BF16),
                        pltpu.VMEM((SCAN_ROWS, 2 * ns), F32),
                        pltpu.VMEM((1, ns), F32),
                        pltpu.VMEM((1, ns), F32)],
        compiler_params=_params("arbitrary", "arbitrary", "arbitrary"),
        name="s5_core",
    )(u_v, wc, kc, pc, apr, api, d_skip.reshape(ncb, 1, cw))


def _s5_operators(a_re, a_im, log_dt, b_re, b_im, c_re, c_im):
    L = SCAN_CHUNK
    groups, p = a_re.shape
    gl = MXU_WIDTH // SSM_GROUP
    ncb = groups // gl
    dt = jnp.exp(log_dt.astype(F32))[:, None]
    lr, li = a_re.astype(F32), a_im.astype(F32)
    mag = jnp.exp(lr * dt)
    ab_re = mag * jnp.cos(li * dt)
    ab_im = mag * jnp.sin(li * dt)
    den = lr * lr + li * li
    coef_re = ((ab_re - 1.0) * lr + ab_im * li) / den
    coef_im = (ab_im * lr - (ab_re - 1.0) * li) / den
    br, bi = b_re.astype(F32), b_im.astype(F32)
    bb_re = coef_re[..., None] * br - coef_im[..., None] * bi
    bb_im = coef_re[..., None] * bi + coef_im[..., None] * br
    cr, ci = c_re.astype(F32), c_im.astype(F32)

    def apow(m):
        m = jnp.asarray(m, F32)[:, None, None]
        mg = jnp.exp(m * (lr * dt))
        return mg * jnp.cos(m * (li * dt)), mg * jnp.sin(m * (li * dt))

    pr, pi = apow(np.arange(L))
    ca_re = cr[None] * pr[:, :, None, :] - ci[None] * pi[:, :, None, :]
    ca_im = cr[None] * pi[:, :, None, :] + ci[None] * pr[:, :, None, :]
    km = jnp.einsum("mgap,gpc->mgca", ca_re, bb_re) - jnp.einsum("mgap,gpc->mgca", ca_im, bb_im)
    km = km[::-1].reshape(L, ncb, gl, SSM_GROUP, SSM_GROUP)
    kc = km.transpose(1, 0, 3, 2, 4).reshape(ncb, L, SSM_GROUP, MXU_WIDTH)

    pr, pi = apow(np.arange(L - 1, -1, -1))
    w_re = pr[..., None] * bb_re[None] - pi[..., None] * bb_im[None]
    w_im = pr[..., None] * bb_im[None] + pi[..., None] * bb_re[None]
    wc = jnp.stack([w_re, w_im], axis=1).transpose(0, 1, 2, 4, 3)
    wc = wc.reshape(L, 2, ncb, MXU_WIDTH, p).transpose(2, 0, 1, 3, 4)
    wc = jnp.concatenate([wc] * (LANES // p), axis=-1)

    pr, pi = apow(np.arange(1, L + 1))
    q_re = cr[None] * pr[:, :, None, :] - ci[None] * pi[:, :, None, :]
    q_im = cr[None] * pi[:, :, None, :] + ci[None] * pr[:, :, None, :]
    pc = jnp.stack([q_re, -q_im], axis=1).reshape(L, 2, ncb, gl, SSM_GROUP, p)
    pc = pc.transpose(2, 0, 1, 5, 3, 4).reshape(ncb, L, 2 * p, MXU_WIDTH)

    pr, pi = apow(L * np.arange(1, 9))
    apr = pr.reshape(8, ncb, gl * p).transpose(1, 0, 2)
    api = pi.reshape(8, ncb, gl * p).transpose(1, 0, 2)
    return wc.astype(BF16), kc.astype(BF16), pc.astype(BF16), apr, api


def _bucket_tiles(t):
    i = np.arange(t)[:, None]
    j = np.arange(t)[None, :]
    tiles = []
    for off in (0, t):
        rel = j + off - i
        n = np.maximum(rel, 0)
        max_exact = NUM_BUCKETS // 2
        large = max_exact + (np.log(np.maximum(n, max_exact).astype(np.float32) / max_exact)
                             / math.log(MAX_DISTANCE / max_exact)
                             * (NUM_BUCKETS - max_exact)).astype(np.int32)
        large = np.minimum(large, NUM_BUCKETS - 1)
        b = np.where(n < max_exact, n, large)
        tiles.append(np.where(rel >= 0, b, -1).astype(np.int32))
    return np.stack(tiles)


def _bias_kernel(bkt_ref, tab_ref, o_ref):
    h = pl.program_id(0)
    far = tab_ref[NUM_BUCKETS - 1, h]
    for typ in range(2):
        b = bkt_ref[typ]
        acc = jnp.zeros(b.shape, F32)
        for i in range(NUM_BUCKETS - 1):
            acc = jnp.where(b == i, (tab_ref[i, h] - far) * LOG2E, acc)
        o_ref[0, typ] = jnp.where(b < 0, NEG_INF, acc)


def _bias_tiles(rel_bias, t):
    heads = rel_bias.shape[1]
    return pl.pallas_call(
        _bias_kernel,
        grid=(heads,),
        in_specs=[_const_spec((2, t, t)), pl.BlockSpec(memory_space=pltpu.SMEM)],
        out_specs=pl.BlockSpec((1, 2, t, t), lambda h: (h, 0, 0, 0)),
        out_shape=jax.ShapeDtypeStruct((heads, 2, t, t), F32),
        compiler_params=_params("parallel"),
        name="rel_bias_tiles",
    )(jnp.asarray(_bucket_tiles(t)), rel_bias.astype(F32))


def _flash_kernel(q_ref, k_ref, vt_ref, bias_ref, lam_ref, hn_ref, o_ref,
                  qcat_ref, sa_ref, sb_ref, m_ref, l_ref, acc_ref, *, lam_init):
    tq = q_ref.shape[2]
    tk = bias_ref.shape[2]
    cw = MXU_WIDTH
    per_branch = tq // cw
    ncol = 2 * per_branch
    per_key = tk // cw
    inner = tq // tk
    assert inner == 2, "the far-tile loop below relies on an odd number of far tiles"
    qi = pl.program_id(2)
    qt = q_ref[0]
    row = lax.broadcasted_iota(jnp.int32, qt.shape, 0)
    zero = jnp.zeros_like(qt)
    qcat_ref[:, :tq] = jnp.where(row < HEAD_DIM, qt, zero)
    qcat_ref[:, tq:] = jnp.where(row >= HEAD_DIM, qt, zero)

    m_ref[...] = jnp.full(m_ref.shape, NEG_INF, F32)
    l_ref[...] = jnp.zeros_like(l_ref)
    acc_ref[...] = jnp.zeros_like(acc_ref)

    def bias_of(kind, c):
        if kind is None:
            return None
        cq = c % per_branch
        dist = cq // per_key - kind
        if dist < 0:
            return "skip"
        if dist > 1:
            return None
        return dist, (cq % per_key) * cw

    def scores_col(kblk, dst_ref, kind, c):
        if bias_of(kind, c) == "skip":
            return None
        s = jnp.dot(kblk, qcat_ref[:, c * cw:(c + 1) * cw], preferred_element_type=F32)
        dst_ref[c] = s
        return jnp.max(s, axis=0, keepdims=True) if kind is None else None

    def softmax_pv_col(vblk, src_ref, cm, kind, c):
        spec = bias_of(kind, c)
        if spec == "skip":
            return
        s = src_ref[c]
        if kind is not None:
            if spec is not None:
                s = s + bias_ref[0, spec[0], :, spec[1]:spec[1] + cw]
            cm = jnp.max(s, axis=0, keepdims=True)
        m_old = m_ref[c]
        m_new = jnp.maximum(m_old, cm)
        alpha = jnp.exp2(m_old - m_new)
        p = jnp.exp2(s - m_new)
        l_ref[c] = alpha * l_ref[c] + jnp.sum(p, axis=0, keepdims=True)
        m_ref[c] = m_new
        acc_ref[c] = alpha * acc_ref[c] + jnp.dot(vblk, p.astype(BF16), preferred_element_type=F32)

    def key_tile(j):
        return k_ref[0, pl.ds(pl.multiple_of(j * tk, tk), tk), :]

    def scores(j, dst_ref, kind):
        kblk = key_tile(j)
        return tuple(scores_col(kblk, dst_ref, kind, c) for c in range(ncol))

    def step(nxt, cur, cmax):
        j, kind, src_ref = cur
        vblk = vt_ref[0, 0, j]
        kblk = key_tile(nxt[0]) if nxt is not None else None
        cm_next = []
        for c in range(ncol):
            if nxt is not None:
                cm_next.append(scores_col(kblk, nxt[2], nxt[1], c))
            softmax_pv_col(vblk, src_ref, cmax[c] if kind is None else None, kind, c)
        return tuple(cm_next)

    def run(tiles, first_max=None):
        bufs = (sa_ref, sb_ref)
        cmax = first_max
        for n, (j, kind) in enumerate(tiles):
            nxt = None
            if n + 1 < len(tiles):
                nxt = tiles[n + 1] + (bufs[(n + 1) % 2],)
            cmax = step(nxt, (j, kind, bufs[n % 2]), cmax)

    first_diag = qi * inner
    diag = [(first_diag + r, r) for r in range(inner)]

    @pl.when(qi == 0)
    def _():
        scores(0, sa_ref, 0)
        run(diag)

    @pl.when(qi >= 1)
    def _():
        n_far = first_diag - 1

        def pair(i, cm_a):
            j = 2 * i
            cm_b = step((j + 1, None, sb_ref), (j, None, sa_ref), cm_a)
            return step((j + 2, None, sa_ref), (j + 1, None, sb_ref), cm_b)

        cm_a = lax.fori_loop(0, (n_far - 1) // 2, pair, scores(0, sa_ref, None))
        run([(n_far - 1, None), (n_far, -1)] + diag, cm_a)

    lv = lam_ref[...]
    lam = (jnp.exp(jnp.sum(lv[0:1] * lv[1:2], axis=1, keepdims=True))
           - jnp.exp(jnp.sum(lv[2:3] * lv[3:4], axis=1, keepdims=True)) + lam_init)
    for c in range(per_branch):
        o1 = acc_ref[c] * (1.0 / l_ref[c])
        o2 = acc_ref[per_branch + c] * (1.0 / l_ref[per_branch + c])
        o = o1 - lam * o2
        o = o * lax.rsqrt(jnp.mean(o * o, axis=0, keepdims=True) + NORM_EPS)
        o = o * hn_ref[...] * (1.0 - lam_init)
        o_ref[0, c * cw:(c + 1) * cw, :] = o.T.astype(BF16)


def _diff_attention(qt, k, vt, bias, lam_vecs, head_norm, *, lam_init):
    batch, d, seq = qt.shape
    heads = d // V_DIM
    t = ATTN_TILE
    tq = ATTN_QTILE
    nblk = seq // t
    ncol = 2 * tq // MXU_WIDTH
    lam_pad = jnp.zeros((8, V_DIM), F32).at[:4, :HEAD_DIM].set(lam_vecs.astype(F32))
    return pl.pallas_call(
        functools.partial(_flash_kernel, lam_init=lam_init),
        grid=(batch, heads, seq // tq),
        in_specs=[pl.BlockSpec((1, V_DIM, tq), lambda b, h, i: (b, h, i)),
                  pl.BlockSpec((1, seq, V_DIM), lambda b, h, i: (b, 0, h)),
                  pl.BlockSpec((1, 1, nblk, V_DIM, t), lambda b, h, i: (b, h, 0, 0, 0)),
                  pl.BlockSpec((1, 2, t, t), lambda b, h, i: (h, 0, 0, 0)),
                  _const_spec((8, V_DIM)),
                  _const_spec((V_DIM, 1))],
        out_specs=pl.BlockSpec((1, tq, V_DIM), lambda b, h, i: (b, i, h)),
        out_shape=jax.ShapeDtypeStruct((batch, seq, d), BF16),
        scratch_shapes=[pltpu.VMEM((V_DIM, 2 * tq), BF16),
                        pltpu.VMEM((ncol, t, MXU_WIDTH), F32),
                        pltpu.VMEM((ncol, t, MXU_WIDTH), F32),
                        pltpu.VMEM((ncol, 1, MXU_WIDTH), F32),
                        pltpu.VMEM((ncol, 1, MXU_WIDTH), F32),
                        pltpu.VMEM((ncol, V_DIM, MXU_WIDTH), F32)],
        compiler_params=_params("parallel", "parallel", "arbitrary"),
        name="diff_attention",
    )(qt, k.reshape(batch, seq, d), vt, bias, lam_pad, head_norm.astype(F32).reshape(V_DIM, 1))


def kernel(x, norm_mixer_pre, norm_mixer_post, norm_mlp_pre, norm_mlp_post, mlp_w_up, mlp_w_down,
           ssm_w_in, ssm_a_re, ssm_a_im, ssm_log_dt, ssm_b_re, ssm_b_im, ssm_c_re, ssm_c_im, ssm_d,
           ssm_w_glu, kv_norm, w_kv, attn_w_q, attn_lambda_q1, attn_lambda_k1, attn_lambda_q2,
           attn_lambda_k2, attn_head_norm, attn_w_o, rel_bias):
    batch, seq, d = x.shape
    depth = norm_mixer_pre.shape[0]
    n_ssm = ssm_w_in.shape[0]
    assert 0 < n_ssm < depth
    xs = x.reshape(batch * seq, d).astype(F32)
    q_scale = HEAD_DIM ** -0.5 * LOG2E

    def mixer_projections(layer):
        if layer >= depth:
            return []
        if layer < n_ssm:
            return [("colblock", norm_mixer_pre[layer], ssm_w_in[layer].astype(BF16), 1.0)]
        projs = []
        if layer == n_ssm:
            projs.append(("kv", kv_norm, w_kv.astype(BF16), 1.0))
        projs.append(("qT", norm_mixer_pre[layer], attn_w_q[layer - n_ssm].astype(BF16), q_scale))
        return projs

    (_, gain, w, _), = mixer_projections(0)
    feeds = [_norm_proj(xs, gain, w, mode="colblock", batch=batch, seq=seq)]
    bias = _bias_tiles(rel_bias, ATTN_TILE)
    k = vt = None
    for layer in range(depth):
        if layer < n_ssm:
            i = layer
            ops = _s5_operators(ssm_a_re[i], ssm_a_im[i], ssm_log_dt[i], ssm_b_re[i], ssm_b_im[i],
                                ssm_c_re[i], ssm_c_im[i])
            y = _s5_core(feeds[0], ops, ssm_d[i].astype(F32), batch=batch, seq=seq)
            w_out = ssm_w_glu[i].astype(BF16)
        else:
            if layer == n_ssm:
                k, vt = feeds[0], feeds[1]
            j = layer - n_ssm
            lam_init = 0.8 - 0.6 * math.exp(-0.3 * layer)
            lam_vecs = jnp.stack([attn_lambda_q1[j], attn_lambda_k1[j],
                                  attn_lambda_q2[j], attn_lambda_k2[j]])
            y = _diff_attention(feeds[-1], k, vt, bias, lam_vecs, attn_head_norm[j], lam_init=lam_init)
            y = y.reshape(batch * seq, d)
            w_out = attn_w_o[j].astype(BF16)
        xs, feeds = _layer_tail(y, w_out, xs, norm_mixer_post[layer], norm_mlp_pre[layer],
                                mlp_w_up[layer].astype(BF16), mlp_w_down[layer].astype(BF16),
                                norm_mlp_post[layer], mixer_projections(layer + 1),
                                glu=layer < n_ssm, chunked=layer < n_ssm, batch=batch, seq=seq)
    return xs.reshape(batch, seq, d).astype(x.dtype)
```

```python
import functools
import math

import numpy as np
import jax
import jax.numpy as jnp
from jax import lax
from jax.experimental import pallas as pl
from jax.experimental.pallas import tpu as pltpu

F32 = jnp.float32
BF16 = jnp.bfloat16

SSM_GROUP = 16
SSM_STATE = 64
HEAD_DIM = 64
V_DIM = 2 * HEAD_DIM
NUM_BUCKETS = 32
MAX_DISTANCE = 128
NORM_EPS = 1e-6
NEG_INF = -1e30
LOG2E = math.log2(math.e)

MXU_WIDTH = 256
LANES = 128
ROW_TILE = 512
SCAN_CHUNK = 8
SCAN_ROWS = 512
ATTN_TILE = 512
ATTN_QTILE = 1024
FF_CHUNK = 1024
VMEM_LIMIT = 56 * 1024 * 1024


def _params(*sem, flags=None):
    return pltpu.CompilerParams(dimension_semantics=sem, vmem_limit_bytes=VMEM_LIMIT, flags=flags)


def _rms(xf, gain):
    return xf * lax.rsqrt(jnp.mean(xf * xf, axis=-1, keepdims=True) + NORM_EPS) * gain


def _const_spec(shape):
    zeros = (0,) * len(shape)
    return pl.BlockSpec(shape, lambda *_: zeros, pipeline_mode=pl.Buffered(1))


def _project(x, g_ref, w_ref, out_refs, slab_ref, mode, scale):
    hn = _rms(x, g_ref[...]).astype(BF16)
    acc = jnp.dot(hn, w_ref[...], preferred_element_type=F32)
    rows = acc.shape[0]
    if mode == "colblock":
        (o_ref,) = out_refs
        L = SCAN_CHUNK
        for sl in range(slab_ref.shape[0]):
            slab_ref[sl] = acc[:, sl * LANES:(sl + 1) * LANES]
        for cb in range(o_ref.shape[0]):
            for t in range(L):
                for hf in range(MXU_WIDTH // LANES):
                    lo = t * MXU_WIDTH + hf * LANES
                    piece = slab_ref[cb * (MXU_WIDTH // LANES) + hf, pl.ds(t, rows // L, stride=L), :]
                    o_ref[cb, :, lo:lo + LANES] = piece.astype(BF16)
    elif mode == "qT":
        (o_ref,) = out_refs
        o_ref[0] = (acc * scale).T.astype(BF16)
    else:
        k_ref, vt_ref = out_refs
        d = k_ref.shape[1]
        k_ref[...] = acc[:, :d].astype(BF16)
        vt = acc[:, d:].T.astype(BF16)
        vt_ref[0, :, 0] = vt.reshape(d // V_DIM, V_DIM, rows)


def _proj_outputs(mode, n, d, nout, batch, seq):
    tm = ROW_TILE
    per_b = seq // tm
    if mode == "colblock":
        ncb = nout // MXU_WIDTH
        L = SCAN_CHUNK
        return ([jax.ShapeDtypeStruct((ncb, n // L, L * MXU_WIDTH), BF16)],
                [pl.BlockSpec((ncb, tm // L, L * MXU_WIDTH), lambda i: (0, i, 0))])
    if mode == "qT":
        return ([jax.ShapeDtypeStruct((batch, nout, seq), BF16)],
                [pl.BlockSpec((1, nout, tm), lambda i: (i // per_b, 0, i % per_b))])
    heads = d // V_DIM
    assert mode == "kv" and tm == ATTN_TILE
    return ([jax.ShapeDtypeStruct((n, d), BF16),
             jax.ShapeDtypeStruct((batch, heads, per_b, V_DIM, tm), BF16)],
            [pl.BlockSpec((tm, d), lambda i: (i, 0)),
             pl.BlockSpec((1, heads, 1, V_DIM, tm), lambda i: (i // per_b, 0, i % per_b, 0, 0))])


def _slab_scratch(d):
    return [pltpu.VMEM((d // LANES, ROW_TILE, LANES), F32), pltpu.VMEM((ROW_TILE, d), BF16)]


def _proj_kernel(x_ref, g_ref, w_ref, *refs, mode, scale):
    slab_ref = refs[-2] if mode == "colblock" else None
    nout = 2 if mode == "kv" else 1
    _project(x_ref[...], g_ref, w_ref, refs[:nout], slab_ref, mode, scale)


def _norm_proj(x, gain, w, *, mode, batch, seq, scale=1.0):
    n, d = x.shape
    nout = w.shape[1]
    tm = ROW_TILE
    out_shape, out_specs = _proj_outputs(mode, n, d, nout, batch, seq)
    out = pl.pallas_call(
        functools.partial(_proj_kernel, mode=mode, scale=scale),
        grid=(n // tm,),
        in_specs=[pl.BlockSpec((tm, d), lambda i: (i, 0)),
                  _const_spec((1, d)),
                  _const_spec((d, nout))],
        out_specs=out_specs,
        out_shape=out_shape,
        scratch_shapes=_slab_scratch(nout) if mode == "colblock" else [],
        compiler_params=_params("parallel"),
        name="norm_proj_" + mode,
    )(x, gain.reshape(1, d), w)
    return out[0] if len(out) == 1 else out


def _mixer_out(y_ref, w_ref, slab_ref, tok_ref, *, glu, chunked, d):
    if chunked:
        L = SCAN_CHUNK
        nchunk = y_ref.shape[1]
        for cb in range(y_ref.shape[0]):
            for t in range(L):
                for hf in range(MXU_WIDTH // LANES):
                    lo = t * MXU_WIDTH + hf * LANES
                    slab_ref[cb * (MXU_WIDTH // LANES) + hf, pl.ds(t, nchunk, stride=L), :] = (
                        y_ref[cb, :, lo:lo + LANES].astype(F32))
        for sl in range(slab_ref.shape[0]):
            tok_ref[:, sl * LANES:(sl + 1) * LANES] = slab_ref[sl].astype(BF16)
        acc = jnp.dot(tok_ref[...], w_ref[...], preferred_element_type=F32)
    else:
        acc = jnp.dot(y_ref[...], w_ref[...], preferred_element_type=F32)
    if glu:
        acc = acc[:, :d] * jax.nn.sigmoid(acc[:, d:])
    return acc


def _mlp_block(x, g1_ref, wu_ref, wd_ref, g2_ref):
    hn = _rms(x, g1_ref[...]).astype(BF16)
    acc = None
    for c in range(wu_ref.shape[1] // FF_CHUNK):
        a = jnp.dot(hn, wu_ref[:, c * FF_CHUNK:(c + 1) * FF_CHUNK], preferred_element_type=F32)
        a = jnp.maximum(a, 0.0)
        a = (a * a).astype(BF16)
        part = jnp.dot(a, wd_ref[c * FF_CHUNK:(c + 1) * FF_CHUNK, :], preferred_element_type=F32)
        acc = part if acc is None else acc + part
    return x + _rms(acc, g2_ref[...])


def _tail_kernel(*refs, glu, chunked, projs, use_slab):
    n_in = 8 + 2 * len(projs)
    y_ref, wo_ref, x_ref, gpost_ref, gpre_ref, wu_ref, wd_ref, gmlp_ref = refs[:8]
    proj_in = refs[8:n_in]
    n_out = 1 + sum(2 if mode == "kv" else 1 for mode, _ in projs)
    o_ref = refs[n_in]
    proj_out = refs[n_in + 1:n_in + n_out]
    slab_ref, tok_ref = refs[n_in + n_out:] if use_slab else (None, None)
    d = o_ref.shape[1]
    acc = _mixer_out(y_ref, wo_ref, slab_ref, tok_ref, glu=glu, chunked=chunked, d=d)
    x = x_ref[...] + _rms(acc, gpost_ref[...])
    x = _mlp_block(x, gpre_ref, wu_ref, wd_ref, gmlp_ref)
    o_ref[...] = x
    k = 0
    for i, (mode, scale) in enumerate(projs):
        cnt = 2 if mode == "kv" else 1
        _project(x, proj_in[2 * i], proj_in[2 * i + 1], proj_out[k:k + cnt], slab_ref, mode, scale)
        k += cnt


def _layer_tail(y, w_out, x, g_post, g_pre, w_up, w_down, g_mlp, projs, *, glu, chunked, batch, seq):
    n, d = x.shape
    tm = ROW_TILE
    row = pl.BlockSpec((tm, d), lambda i: (i, 0))
    if chunked:
        ncb, _, width = y.shape
        y_spec = pl.BlockSpec((ncb, tm // SCAN_CHUNK, width), lambda i: (0, i, 0))
    else:
        y_spec = pl.BlockSpec((tm, y.shape[1]), lambda i: (i, 0))
    in_specs = [y_spec, _const_spec(w_out.shape), row, _const_spec((1, d)), _const_spec((1, d)),
                _const_spec(w_up.shape), _const_spec(w_down.shape), _const_spec((1, d))]
    args = [y, w_out, x, g_post.reshape(1, d), g_pre.reshape(1, d), w_up, w_down, g_mlp.reshape(1, d)]
    out_shape = [jax.ShapeDtypeStruct((n, d), F32)]
    out_specs = [row]
    for mode, gain, w, _ in projs:
        in_specs += [_const_spec((1, d)), _const_spec(w.shape)]
        args += [gain.reshape(1, d), w]
        shapes, specs = _proj_outputs(mode, n, d, w.shape[1], batch, seq)
        out_shape += shapes
        out_specs += specs
    use_slab = chunked or any(mode == "colblock" for mode, *_ in projs)
    out = pl.pallas_call(
        functools.partial(_tail_kernel, glu=glu, chunked=chunked,
                          projs=tuple((mode, scale) for mode, _, _, scale in projs), use_slab=use_slab),
        grid=(n // tm,),
        in_specs=in_specs,
        out_specs=out_specs,
        out_shape=out_shape,
        scratch_shapes=_slab_scratch(d) if use_slab else [],
        compiler_params=_params("parallel"),
        name="layer_tail",
    )(*args)
    return out[0], list(out[1:])


def _s5_kernel(u_ref, wc_ref, kc_ref, pc_ref, apr_ref, api_ref, d_ref, o_ref,
               w_ref, k_ref, p_ref, v_ref, sp_ref, sr_ref, si_ref):
    cw = MXU_WIDTH
    ns = v_ref.shape[1] // 2
    L = u_ref.shape[2] // cw
    grp_shift = SSM_GROUP.bit_length() - 1
    st_shift = SSM_STATE.bit_length() - 1

    @pl.when(jnp.logical_and(pl.program_id(1) == 0, pl.program_id(2) == 0))
    def _():
        def iota(shape, dim):
            return lax.broadcasted_iota(jnp.int32, shape, dim)

        wmask = (iota((cw, ns), 0) >> grp_shift) == (iota((cw, ns), 1) >> st_shift)
        pmask = (iota((ns, cw), 0) >> st_shift) == (iota((ns, cw), 1) >> grp_shift)
        kmask = (iota((cw, cw), 0) >> grp_shift) == (iota((cw, cw), 1) >> grp_shift)
        for t in range(L):
            for h in range(2):
                piece = wc_ref[0, t, h].astype(F32)
                tiled = jnp.concatenate([piece] * (ns // LANES), axis=1)
                w_ref[t * cw:(t + 1) * cw, h * ns:(h + 1) * ns] = (
                    jnp.where(wmask, tiled, 0.0).astype(BF16))
                piece = pc_ref[0, t, h * SSM_STATE:(h + 1) * SSM_STATE, :].astype(F32)
                tiled = jnp.concatenate([piece] * (ns // SSM_STATE), axis=0)
                p_ref[t, h * ns:(h + 1) * ns, :] = jnp.where(pmask, tiled, 0.0).astype(BF16)
            piece = kc_ref[0, t].astype(F32)
            tiled = jnp.concatenate([piece] * (cw // SSM_GROUP), axis=0)
            k_ref[t * cw:(t + 1) * cw, :] = jnp.where(kmask, tiled, 0.0).astype(BF16)

    @pl.when(pl.program_id(2) == 0)
    def _():
        sr_ref[...] = jnp.zeros_like(sr_ref)
        si_ref[...] = jnp.zeros_like(si_ref)

    rows = v_ref.shape[0]
    half = rows // 2
    apr = apr_ref[0]
    api = api_ref[0]
    row = lax.broadcasted_iota(jnp.int32, (8, ns), 0)

    def shift(z, k, fill):
        return jnp.where(row >= k, pltpu.roll(z, k, 0), fill)

    def scan_block(i, carry):
        sr, si = carry
        zr = v_ref[8 * i:8 * i + 8, 0:ns]
        zi = v_ref[8 * i:8 * i + 8, ns:2 * ns]
        for k in (1, 2, 4):
            ar, ai = apr[k - 1:k], api[k - 1:k]
            hr, hi = shift(zr, k, 0.0), shift(zi, k, 0.0)
            zr, zi = zr + ar * hr - ai * hi, zi + ar * hi + ai * hr
        fr = zr + apr * sr - api * si
        fi = zi + apr * si + api * sr
        v_ref[8 * i:8 * i + 8, 0:ns] = shift(fr, 1, sr)
        v_ref[8 * i:8 * i + 8, ns:2 * ns] = shift(fi, 1, si)
        return fr[7:8], fi[7:8]

    def outputs(lo, t):
        xs = u_ref[0, lo:lo + half, :]
        acc = jnp.dot(xs[:, :(t + 1) * cw], k_ref[(L - 1 - t) * cw:, :], preferred_element_type=F32)
        acc = acc + jnp.dot(sp_ref[lo:lo + half, :], p_ref[t], preferred_element_type=F32)
        y = acc + d_ref[0] * xs[:, t * cw:(t + 1) * cw].astype(F32)
        o_ref[0, lo:lo + half, t * cw:(t + 1) * cw] = jax.nn.gelu(y).astype(BF16)

    nblk_half = half // 8
    pieces = 4
    ncols = 2 * ns // pieces
    v_ref[:half, :] = jnp.dot(u_ref[0, :half, :], w_ref[...], preferred_element_type=F32)
    carry = (sr_ref[...], si_ref[...])
    for n in range(pieces):
        v_ref[half:, n * ncols:(n + 1) * ncols] = jnp.dot(
            u_ref[0, half:, :], w_ref[:, n * ncols:(n + 1) * ncols], preferred_element_type=F32)
        for i in range(n * nblk_half // pieces, (n + 1) * nblk_half // pieces):
            carry = scan_block(i, carry)
    sp_ref[:half, :] = v_ref[:half, :].astype(BF16)
    for t in range(L):
        outputs(0, t)
        for i in range(t * nblk_half // L, (t + 1) * nblk_half // L):
            carry = scan_block(nblk_half + i, carry)
    sr_ref[...], si_ref[...] = carry
    sp_ref[half:, :] = v_ref[half:, :].astype(BF16)
    for t in range(L):
        outputs(half, t)


def _s5_core(u_v, ops, d_skip, *, batch, seq):
    wc, kc, pc, apr, api = ops
    L = SCAN_CHUNK
    cw = MXU_WIDTH
    ncb = u_v.shape[0]
    rows_per_b = seq // L
    halves = rows_per_b // SCAN_ROWS
    ns = apr.shape[2]
    blk = pl.BlockSpec((1, SCAN_ROWS, L * cw), lambda cb, b, h: (cb, b * halves + h, 0))
    return pl.pallas_call(
        _s5_kernel,
        grid=(ncb, batch, halves),
        in_specs=[blk,
                  pl.BlockSpec((1,) + wc.shape[1:], lambda cb, b, h: (cb, 0, 0, 0, 0)),
                  pl.BlockSpec((1,) + kc.shape[1:], lambda cb, b, h: (cb, 0, 0, 0)),
                  pl.BlockSpec((1,) + pc.shape[1:], lambda cb, b, h: (cb, 0, 0, 0)),
                  pl.BlockSpec((1, 8, ns), lambda cb, b, h: (cb, 0, 0)),
                  pl.BlockSpec((1, 8, ns), lambda cb, b, h: (cb, 0, 0)),
                  pl.BlockSpec((1, 1, cw), lambda cb, b, h: (cb, 0, 0))],
        out_specs=blk,
        out_shape=jax.ShapeDtypeStruct(u_v.shape, BF16),
        scratch_shapes=[pltpu.VMEM((L * cw, 2 * ns), BF16),
                        pltpu.VMEM((L * cw, cw), BF16),
                        pltpu.VMEM((L, 2 * ns, cw), BF16),
                        pltpu.VMEM((SCAN_ROWS, 2 * ns), F32),
                        pltpu.VMEM((SCAN_ROWS, 2 * ns), BF16),
                        pltpu.VMEM((1, ns), F32),
                        pltpu.VMEM((1, ns), F32)],
        compiler_params=_params("arbitrary", "arbitrary", "arbitrary"),
        name="s5_core",
    )(u_v, wc, kc, pc, apr, api, d_skip.reshape(ncb, 1, cw))


def _s5_operators(a_re, a_im, log_dt, b_re, b_im, c_re, c_im):
    L = SCAN_CHUNK
    groups, p = a_re.shape
    gl = MXU_WIDTH // SSM_GROUP
    ncb = groups // gl
    dt = jnp.exp(log_dt.astype(F32))[:, None]
    lr, li = a_re.astype(F32), a_im.astype(F32)
    mag = jnp.exp(lr * dt)
    ab_re = mag * jnp.cos(li * dt)
    ab_im = mag * jnp.sin(li * dt)
    den = lr * lr + li * li
    coef_re = ((ab_re - 1.0) * lr + ab_im * li) / den
    coef_im = (ab_im * lr - (ab_re - 1.0) * li) / den
    br, bi = b_re.astype(F32), b_im.astype(F32)
    bb_re = coef_re[..., None] * br - coef_im[..., None] * bi
    bb_im = coef_re[..., None] * bi + coef_im[..., None] * br
    cr, ci = c_re.astype(F32), c_im.astype(F32)

    def apow(m):
        m = jnp.asarray(m, F32)[:, None, None]
        mg = jnp.exp(m * (lr * dt))
        return mg * jnp.cos(m * (li * dt)), mg * jnp.sin(m * (li * dt))

    pr, pi = apow(np.arange(L))
    ca_re = cr[None] * pr[:, :, None, :] - ci[None] * pi[:, :, None, :]
    ca_im = cr[None] * pi[:, :, None, :] + ci[None] * pr[:, :, None, :]
    km = jnp.einsum("mgap,gpc->mgca", ca_re, bb_re) - jnp.einsum("mgap,gpc->mgca", ca_im, bb_im)
    km = km[::-1].reshape(L, ncb, gl, SSM_GROUP, SSM_GROUP)
    kc = km.transpose(1, 0, 3, 2, 4).reshape(ncb, L, SSM_GROUP, MXU_WIDTH)

    pr, pi = apow(np.arange(L - 1, -1, -1))
    w_re = pr[..., None] * bb_re[None] - pi[..., None] * bb_im[None]
    w_im = pr[..., None] * bb_im[None] + pi[..., None] * bb_re[None]
    wc = jnp.stack([w_re, w_im], axis=1).transpose(0, 1, 2, 4, 3)
    wc = wc.reshape(L, 2, ncb, MXU_WIDTH, p).transpose(2, 0, 1, 3, 4)
    wc = jnp.concatenate([wc] * (LANES // p), axis=-1)

    pr, pi = apow(np.arange(1, L + 1))
    q_re = cr[None] * pr[:, :, None, :] - ci[None] * pi[:, :, None, :]
    q_im = cr[None] * pi[:, :, None, :] + ci[None] * pr[:, :, None, :]
    pc = jnp.stack([q_re, -q_im], axis=1).reshape(L, 2, ncb, gl, SSM_GROUP, p)
    pc = pc.transpose(2, 0, 1, 5, 3, 4).reshape(ncb, L, 2 * p, MXU_WIDTH)

    pr, pi = apow(L * np.arange(1, 9))
    apr = pr.reshape(8, ncb, gl * p).transpose(1, 0, 2)
    api = pi.reshape(8, ncb, gl * p).transpose(1, 0, 2)
    return wc.astype(BF16), kc.astype(BF16), pc.astype(BF16), apr, api


def _bucket_tiles(t):
    i = np.arange(t)[:, None]
    j = np.arange(t)[None, :]
    tiles = []
    for off in (0, t):
        rel = j + off - i
        n = np.maximum(rel, 0)
        max_exact = NUM_BUCKETS // 2
        large = max_exact + (np.log(np.maximum(n, max_exact).astype(np.float32) / max_exact)
                             / math.log(MAX_DISTANCE / max_exact)
                             * (NUM_BUCKETS - max_exact)).astype(np.int32)
        large = np.minimum(large, NUM_BUCKETS - 1)
        b = np.where(n < max_exact, n, large)
        tiles.append(np.where(rel >= 0, b, -1).astype(np.int32))
    return np.stack(tiles)


def _bias_kernel(bkt_ref, tab_ref, o_ref):
    h = pl.program_id(0)
    far = tab_ref[NUM_BUCKETS - 1, h]
    for typ in range(2):
        b = bkt_ref[typ]
        acc = jnp.zeros(b.shape, F32)
        for i in range(NUM_BUCKETS - 1):
            acc = jnp.where(b == i, (tab_ref[i, h] - far) * LOG2E, acc)
        o_ref[0, typ] = jnp.where(b < 0, NEG_INF, acc)


def _bias_tiles(rel_bias, t):
    heads = rel_bias.shape[1]
    return pl.pallas_call(
        _bias_kernel,
        grid=(heads,),
        in_specs=[_const_spec((2, t, t)), pl.BlockSpec(memory_space=pltpu.SMEM)],
        out_specs=pl.BlockSpec((1, 2, t, t), lambda h: (h, 0, 0, 0)),
        out_shape=jax.ShapeDtypeStruct((heads, 2, t, t), F32),
        compiler_params=_params("parallel"),
        name="rel_bias_tiles",
    )(jnp.asarray(_bucket_tiles(t)), rel_bias.astype(F32))


def _flash_kernel(q_ref, k_ref, vt_ref, bias_ref, lam_ref, hn_ref, o_ref,
                  qcat_ref, sa_ref, sb_ref, m_ref, l_ref, acc_ref, *, lam_init):
    tq = q_ref.shape[2]
    tk = bias_ref.shape[2]
    cw = MXU_WIDTH
    per_branch = tq // cw
    ncol = 2 * per_branch
    per_key = tk // cw
    inner = tq // tk
    assert inner == 2, "the far-tile loop below relies on an odd number of far tiles"
    qi = pl.program_id(2)
    qt = q_ref[0]
    row = lax.broadcasted_iota(jnp.int32, qt.shape, 0)
    zero = jnp.zeros_like(qt)
    qcat_ref[:, :tq] = jnp.where(row < HEAD_DIM, qt, zero)
    qcat_ref[:, tq:] = jnp.where(row >= HEAD_DIM, qt, zero)

    m_ref[...] = jnp.full(m_ref.shape, NEG_INF, F32)
    l_ref[...] = jnp.zeros_like(l_ref)
    acc_ref[...] = jnp.zeros_like(acc_ref)

    def bias_of(kind, c):
        if kind is None:
            return None
        cq = c % per_branch
        dist = cq // per_key - kind
        if dist < 0:
            return "skip"
        if dist > 1:
            return None
        return dist, (cq % per_key) * cw

    def scores_col(kblk, dst_ref, kind, c):
        spec = bias_of(kind, c)
        if spec == "skip":
            return None
        s = jnp.dot(kblk, qcat_ref[:, c * cw:(c + 1) * cw], preferred_element_type=F32)
        if spec is not None:
            s = s + bias_ref[0, spec[0], :, spec[1]:spec[1] + cw]
        dst_ref[c] = s
        return jnp.max(s, axis=0, keepdims=True)

    def softmax_pv_col(vblk, src_ref, cm, kind, c):
        if bias_of(kind, c) == "skip":
            return
        s = src_ref[c]
        m_old = m_ref[c]
        m_new = jnp.maximum(m_old, cm)
        alpha = jnp.exp2(m_old - m_new)
        p = jnp.exp2(s - m_new)
        l_ref[c] = alpha * l_ref[c] + jnp.sum(p, axis=0, keepdims=True)
        m_ref[c] = m_new
        acc_ref[c] = alpha * acc_ref[c] + jnp.dot(vblk, p.astype(BF16), preferred_element_type=F32)

    def key_tile(j):
        return k_ref[0, pl.ds(pl.multiple_of(j * tk, tk), tk), :]

    def scores(j, dst_ref, kind):
        kblk = key_tile(j)
        return tuple(scores_col(kblk, dst_ref, kind, c) for c in range(ncol))

    def step(nxt, cur, cmax):
        j, kind, src_ref = cur
        vblk = vt_ref[0, 0, j]
        kblk = key_tile(nxt[0]) if nxt is not None else None
        cm_next = []
        for c in range(ncol):
            if nxt is not None:
                cm_next.append(scores_col(kblk, nxt[2], nxt[1], c))
            softmax_pv_col(vblk, src_ref, cmax[c], kind, c)
        return tuple(cm_next)

    def run(tiles, first_max):
        bufs = (sa_ref, sb_ref)
        cmax = first_max
        for n, (j, kind) in enumerate(tiles):
            nxt = None
            if n + 1 < len(tiles):
                nxt = tiles[n + 1] + (bufs[(n + 1) % 2],)
            cmax = step(nxt, (j, kind, bufs[n % 2]), cmax)

    first_diag = qi * inner
    diag = [(first_diag + r, r) for r in range(inner)]

    @pl.when(qi == 0)
    def _():
        run(diag, scores(0, sa_ref, 0))

    @pl.when(qi >= 1)
    def _():
        n_far = first_diag - 1

        def pair(i, cm_a):
            j = 2 * i
            cm_b = step((j + 1, None, sb_ref), (j, None, sa_ref), cm_a)
            return step((j + 2, None, sa_ref), (j + 1, None, sb_ref), cm_b)

        cm_a = lax.fori_loop(0, (n_far - 1) // 2, pair, scores(0, sa_ref, None))
        run([(n_far - 1, None), (n_far, -1)] + diag, cm_a)

    lv = lam_ref[...]
    lam = (jnp.exp(jnp.sum(lv[0:1] * lv[1:2], axis=1, keepdims=True))
           - jnp.exp(jnp.sum(lv[2:3] * lv[3:4], axis=1, keepdims=True)) + lam_init)
    for c in range(per_branch):
        o1 = acc_ref[c] * (1.0 / l_ref[c])
        o2 = acc_ref[per_branch + c] * (1.0 / l_ref[per_branch + c])
        o = o1 - lam * o2
        o = o * lax.rsqrt(jnp.mean(o * o, axis=0, keepdims=True) + NORM_EPS)
        o = o * hn_ref[...] * (1.0 - lam_init)
        o_ref[0, c * cw:(c + 1) * cw, :] = o.T.astype(BF16)


def _diff_attention(qt, k, vt, bias, lam_vecs, head_norm, *, lam_init):
    batch, d, seq = qt.shape
    heads = d // V_DIM
    t = ATTN_TILE
    tq = ATTN_QTILE
    nblk = seq // t
    ncol = 2 * tq // MXU_WIDTH
    lam_pad = jnp.zeros((8, V_DIM), F32).at[:4, :HEAD_DIM].set(lam_vecs.astype(F32))
    return pl.pallas_call(
        functools.partial(_flash_kernel, lam_init=lam_init),
        grid=(batch, heads, seq // tq),
        in_specs=[pl.BlockSpec((1, V_DIM, tq), lambda b, h, i: (b, h, i)),
                  pl.BlockSpec((1, seq, V_DIM), lambda b, h, i: (b, 0, h)),
                  pl.BlockSpec((1, 1, nblk, V_DIM, t), lambda b, h, i: (b, h, 0, 0, 0)),
                  pl.BlockSpec((1, 2, t, t), lambda b, h, i: (h, 0, 0, 0)),
                  _const_spec((8, V_DIM)),
                  _const_spec((V_DIM, 1))],
        out_specs=pl.BlockSpec((1, tq, V_DIM), lambda b, h, i: (b, i, h)),
        out_shape=jax.ShapeDtypeStruct((batch, seq, d), BF16),
        scratch_shapes=[pltpu.VMEM((V_DIM, 2 * tq), BF16),
                        pltpu.VMEM((ncol, t, MXU_WIDTH), F32),
                        pltpu.VMEM((ncol, t, MXU_WIDTH), F32),
                        pltpu.VMEM((ncol, 1, MXU_WIDTH), F32),
                        pltpu.VMEM((ncol, 1, MXU_WIDTH), F32),
                        pltpu.VMEM((ncol, V_DIM, MXU_WIDTH), F32)],
        compiler_params=_params("parallel", "parallel", "arbitrary"),
        name="diff_attention",
    )(qt, k.reshape(batch, seq, d), vt, bias, lam_pad, head_norm.astype(F32).reshape(V_DIM, 1))


def kernel(x, norm_mixer_pre, norm_mixer_post, norm_mlp_pre, norm_mlp_post, mlp_w_up, mlp_w_down,
           ssm_w_in, ssm_a_re, ssm_a_im, ssm_log_dt, ssm_b_re, ssm_b_im, ssm_c_re, ssm_c_im, ssm_d,
           ssm_w_glu, kv_norm, w_kv, attn_w_q, attn_lambda_q1, attn_lambda_k1, attn_lambda_q2,
           attn_lambda_k2, attn_head_norm, attn_w_o, rel_bias):
    batch, seq, d = x.shape
    depth = norm_mixer_pre.shape[0]
    n_ssm = ssm_w_in.shape[0]
    assert 0 < n_ssm < depth
    xs = x.reshape(batch * seq, d).astype(F32)
    q_scale = HEAD_DIM ** -0.5 * LOG2E

    def mixer_projections(layer):
        if layer >= depth:
            return []
        if layer < n_ssm:
            return [("colblock", norm_mixer_pre[layer], ssm_w_in[layer].astype(BF16), 1.0)]
        projs = []
        if layer == n_ssm:
            projs.append(("kv", kv_norm, w_kv.astype(BF16), 1.0))
        projs.append(("qT", norm_mixer_pre[layer], attn_w_q[layer - n_ssm].astype(BF16), q_scale))
        return projs

    (_, gain, w, _), = mixer_projections(0)
    feeds = [_norm_proj(xs, gain, w, mode="colblock", batch=batch, seq=seq)]
    bias = _bias_tiles(rel_bias, ATTN_TILE)
    k = vt = None
    for layer in range(depth):
        if layer < n_ssm:
            i = layer
            ops = _s5_operators(ssm_a_re[i], ssm_a_im[i], ssm_log_dt[i], ssm_b_re[i], ssm_b_im[i],
                                ssm_c_re[i], ssm_c_im[i])
            y = _s5_core(feeds[0], ops, ssm_d[i].astype(F32), batch=batch, seq=seq)
            w_out = ssm_w_glu[i].astype(BF16)
        else:
            if layer == n_ssm:
                k, vt = feeds[0], feeds[1]
            j = layer - n_ssm
            lam_init = 0.8 - 0.6 * math.exp(-0.3 * layer)
            lam_vecs = jnp.stack([attn_lambda_q1[j], attn_lambda_k1[j],
                                  attn_lambda_q2[j], attn_lambda_k2[j]])
            y = _diff_attention(feeds[-1], k, vt, bias, lam_vecs, attn_head_norm[j], lam_init=lam_init)
            y = y.reshape(batch * seq, d)
            w_out = attn_w_o[j].astype(BF16)
        xs, feeds = _layer_tail(y, w_out, xs, norm_mixer_post[layer], norm_mlp_pre[layer],
                                mlp_w_up[layer].astype(BF16), mlp_w_down[layer].astype(BF16),
                                norm_mlp_post[layer], mixer_projections(layer + 1),
                                glu=layer < n_ssm, chunked=layer < n_ssm, batch=batch, seq=seq)
    return xs.reshape(batch, seq, d).astype(x.dtype)
```

```python
import functools
import math

import numpy as np
import jax
import jax.numpy as jnp
from jax import lax
from jax.experimental import pallas as pl
from jax.experimental.pallas import tpu as pltpu

F32 = jnp.float32
BF16 = jnp.bfloat16

SSM_GROUP = 16
SSM_STATE = 64
HEAD_DIM = 64
V_DIM = 2 * HEAD_DIM
NUM_BUCKETS = 32
MAX_DISTANCE = 128
NORM_EPS = 1e-6
NEG_INF = -1e30
LOG2E = math.log2(math.e)

MXU_WIDTH = 256
LANES = 128
ROW_TILE = 512
SCAN_CHUNK = 8
SCAN_ROWS = 512
ATTN_TILE = 512
ATTN_QTILE = 1024
FF_CHUNK = 1024
VMEM_LIMIT = 56 * 1024 * 1024


def _params(*sem, flags=None):
    return pltpu.CompilerParams(dimension_semantics=sem, vmem_limit_bytes=VMEM_LIMIT, flags=flags)


def _rms(xf, gain):
    return xf * lax.rsqrt(jnp.mean(xf * xf, axis=-1, keepdims=True) + NORM_EPS) * gain


def _const_spec(shape):
    zeros = (0,) * len(shape)
    return pl.BlockSpec(shape, lambda *_: zeros, pipeline_mode=pl.Buffered(1))


def _project(x, g_ref, w_ref, out_refs, slab_ref, mode, scale):
    hn = _rms(x, g_ref[...]).astype(BF16)
    acc = jnp.dot(hn, w_ref[...], preferred_element_type=F32)
    rows = acc.shape[0]
    if mode == "colblock":
        (o_ref,) = out_refs
        L = SCAN_CHUNK
        for sl in range(slab_ref.shape[0]):
            slab_ref[sl] = acc[:, sl * LANES:(sl + 1) * LANES]
        for cb in range(o_ref.shape[0]):
            for t in range(L):
                for hf in range(MXU_WIDTH // LANES):
                    lo = t * MXU_WIDTH + hf * LANES
                    piece = slab_ref[cb * (MXU_WIDTH // LANES) + hf, pl.ds(t, rows // L, stride=L), :]
                    o_ref[cb, :, lo:lo + LANES] = piece.astype(BF16)
    elif mode == "qT":
        (o_ref,) = out_refs
        o_ref[0] = (acc * scale).T.astype(BF16)
    else:
        k_ref, vt_ref = out_refs
        d = k_ref.shape[1]
        k_ref[...] = acc[:, :d].astype(BF16)
        vt = acc[:, d:].T.astype(BF16)
        vt_ref[0, :, 0] = vt.reshape(d // V_DIM, V_DIM, rows)


def _proj_outputs(mode, n, d, nout, batch, seq):
    tm = ROW_TILE
    per_b = seq // tm
    if mode == "colblock":
        ncb = nout // MXU_WIDTH
        L = SCAN_CHUNK
        return ([jax.ShapeDtypeStruct((ncb, n // L, L * MXU_WIDTH), BF16)],
                [pl.BlockSpec((ncb, tm // L, L * MXU_WIDTH), lambda i: (0, i, 0))])
    if mode == "qT":
        return ([jax.ShapeDtypeStruct((batch, nout, seq), BF16)],
                [pl.BlockSpec((1, nout, tm), lambda i: (i // per_b, 0, i % per_b))])
    heads = d // V_DIM
    assert mode == "kv" and tm == ATTN_TILE
    return ([jax.ShapeDtypeStruct((n, d), BF16),
             jax.ShapeDtypeStruct((batch, heads, per_b, V_DIM, tm), BF16)],
            [pl.BlockSpec((tm, d), lambda i: (i, 0)),
             pl.BlockSpec((1, heads, 1, V_DIM, tm), lambda i: (i // per_b, 0, i % per_b, 0, 0))])


def _slab_scratch(d):
    return [pltpu.VMEM((d // LANES, ROW_TILE, LANES), F32), pltpu.VMEM((ROW_TILE, d), BF16)]


def _proj_kernel(x_ref, g_ref, w_ref, *refs, mode, scale):
    slab_ref = refs[-2] if mode == "colblock" else None
    nout = 2 if mode == "kv" else 1
    _project(x_ref[...], g_ref, w_ref, refs[:nout], slab_ref, mode, scale)


def _norm_proj(x, gain, w, *, mode, batch, seq, scale=1.0):
    n, d = x.shape
    nout = w.shape[1]
    tm = ROW_TILE
    out_shape, out_specs = _proj_outputs(mode, n, d, nout, batch, seq)
    out = pl.pallas_call(
        functools.partial(_proj_kernel, mode=mode, scale=scale),
        grid=(n // tm,),
        in_specs=[pl.BlockSpec((tm, d), lambda i: (i, 0)),
                  _const_spec((1, d)),
                  _const_spec((d, nout))],
        out_specs=out_specs,
        out_shape=out_shape,
        scratch_shapes=_slab_scratch(nout) if mode == "colblock" else [],
        compiler_params=_params("parallel"),
        name="norm_proj_" + mode,
    )(x, gain.reshape(1, d), w)
    return out[0] if len(out) == 1 else out


def _mixer_out(y_ref, w_ref, slab_ref, tok_ref, *, glu, chunked, d):
    if chunked:
        L = SCAN_CHUNK
        nchunk = y_ref.shape[1]
        for cb in range(y_ref.shape[0]):
            for t in range(L):
                for hf in range(MXU_WIDTH // LANES):
                    lo = t * MXU_WIDTH + hf * LANES
                    slab_ref[cb * (MXU_WIDTH // LANES) + hf, pl.ds(t, nchunk, stride=L), :] = (
                        y_ref[cb, :, lo:lo + LANES].astype(F32))
        for sl in range(slab_ref.shape[0]):
            tok_ref[:, sl * LANES:(sl + 1) * LANES] = slab_ref[sl].astype(BF16)
        acc = jnp.dot(tok_ref[...], w_ref[...], preferred_element_type=F32)
    else:
        acc = jnp.dot(y_ref[...], w_ref[...], preferred_element_type=F32)
    if glu:
        acc = acc[:, :d] * jax.nn.sigmoid(acc[:, d:])
    return acc


def _mlp_block(x, g1_ref, wu_ref, wd_ref, g2_ref):
    hn = _rms(x, g1_ref[...]).astype(BF16)
    acc = None
    for c in range(wu_ref.shape[1] // FF_CHUNK):
        a = jnp.dot(hn, wu_ref[:, c * FF_CHUNK:(c + 1) * FF_CHUNK], preferred_element_type=F32)
        a = jnp.maximum(a, 0.0)
        a = (a * a).astype(BF16)
        part = jnp.dot(a, wd_ref[c * FF_CHUNK:(c + 1) * FF_CHUNK, :], preferred_element_type=F32)
        acc = part if acc is None else acc + part
    return x + _rms(acc, g2_ref[...])


def _tail_kernel(*refs, glu, chunked, projs, use_slab):
    n_in = 8 + 2 * len(projs)
    y_ref, wo_ref, x_ref, gpost_ref, gpre_ref, wu_ref, wd_ref, gmlp_ref = refs[:8]
    proj_in = refs[8:n_in]
    n_out = 1 + sum(2 if mode == "kv" else 1 for mode, _ in projs)
    o_ref = refs[n_in]
    proj_out = refs[n_in + 1:n_in + n_out]
    slab_ref, tok_ref = refs[n_in + n_out:] if use_slab else (None, None)
    d = o_ref.shape[1]
    acc = _mixer_out(y_ref, wo_ref, slab_ref, tok_ref, glu=glu, chunked=chunked, d=d)
    x = x_ref[...] + _rms(acc, gpost_ref[...])
    x = _mlp_block(x, gpre_ref, wu_ref, wd_ref, gmlp_ref)
    o_ref[...] = x
    k = 0
    for i, (mode, scale) in enumerate(projs):
        cnt = 2 if mode == "kv" else 1
        _project(x, proj_in[2 * i], proj_in[2 * i + 1], proj_out[k:k + cnt], slab_ref, mode, scale)
        k += cnt


def _layer_weight(w):
    if isinstance(w, tuple):
        stack, layer = w
        return stack, pl.BlockSpec((None,) + stack.shape[1:], lambda *_: (layer, 0, 0),
                                   pipeline_mode=pl.Buffered(1))
    return w, _const_spec(w.shape)


def _layer_tail(y, w_out, x, g_post, g_pre, w_up, w_down, g_mlp, projs, *, glu, chunked, batch, seq):
    n, d = x.shape
    tm = ROW_TILE
    row = pl.BlockSpec((tm, d), lambda i: (i, 0))
    if chunked:
        ncb, _, width = y.shape
        y_spec = pl.BlockSpec((ncb, tm // SCAN_CHUNK, width), lambda i: (0, i, 0))
    else:
        y_spec = pl.BlockSpec((tm, y.shape[1]), lambda i: (i, 0))
    (w_out, wo_spec), (w_up, wu_spec), (w_down, wd_spec) = map(_layer_weight, (w_out, w_up, w_down))
    in_specs = [y_spec, wo_spec, row, _const_spec((1, d)), _const_spec((1, d)),
                wu_spec, wd_spec, _const_spec((1, d))]
    args = [y, w_out, x, g_post.reshape(1, d), g_pre.reshape(1, d), w_up, w_down, g_mlp.reshape(1, d)]
    out_shape = [jax.ShapeDtypeStruct((n, d), F32)]
    out_specs = [row]
    for mode, gain, w, _ in projs:
        w, w_spec = _layer_weight(w)
        in_specs += [_const_spec((1, d)), w_spec]
        args += [gain.reshape(1, d), w]
        shapes, specs = _proj_outputs(mode, n, d, w.shape[-1], batch, seq)
        out_shape += shapes
        out_specs += specs
    use_slab = chunked or any(mode == "colblock" for mode, *_ in projs)
    out = pl.pallas_call(
        functools.partial(_tail_kernel, glu=glu, chunked=chunked,
                          projs=tuple((mode, scale) for mode, _, _, scale in projs), use_slab=use_slab),
        grid=(n // tm,),
        in_specs=in_specs,
        out_specs=out_specs,
        out_shape=out_shape,
        scratch_shapes=_slab_scratch(d) if use_slab else [],
        compiler_params=_params("parallel"),
        name="layer_tail",
    )(*args)
    return out[0], list(out[1:])


def _s5_kernel(u_ref, wc_ref, kc_ref, pc_ref, apr_ref, api_ref, d_ref, o_ref,
               w_ref, k_ref, p_ref, v_ref, sp_ref, sr_ref, si_ref):
    cw = MXU_WIDTH
    ns = v_ref.shape[1] // 2
    L = u_ref.shape[2] // cw
    grp_shift = SSM_GROUP.bit_length() - 1
    st_shift = SSM_STATE.bit_length() - 1

    @pl.when(jnp.logical_and(pl.program_id(1) == 0, pl.program_id(2) == 0))
    def _():
        def iota(shape, dim):
            return lax.broadcasted_iota(jnp.int32, shape, dim)

        wmask = (iota((cw, ns), 0) >> grp_shift) == (iota((cw, ns), 1) >> st_shift)
        pmask = (iota((ns, cw), 0) >> st_shift) == (iota((ns, cw), 1) >> grp_shift)
        kmask = (iota((cw, cw), 0) >> grp_shift) == (iota((cw, cw), 1) >> grp_shift)
        for t in range(L):
            for h in range(2):
                piece = wc_ref[0, t, h].astype(F32)
                tiled = jnp.concatenate([piece] * (ns // LANES), axis=1)
                w_ref[t * cw:(t + 1) * cw, h * ns:(h + 1) * ns] = (
                    jnp.where(wmask, tiled, 0.0).astype(BF16))
                piece = pc_ref[0, t, h * SSM_STATE:(h + 1) * SSM_STATE, :].astype(F32)
                tiled = jnp.concatenate([piece] * (ns // SSM_STATE), axis=0)
                p_ref[t, h * ns:(h + 1) * ns, :] = jnp.where(pmask, tiled, 0.0).astype(BF16)
            piece = kc_ref[0, t].astype(F32)
            tiled = jnp.concatenate([piece] * (cw // SSM_GROUP), axis=0)
            k_ref[t * cw:(t + 1) * cw, :] = jnp.where(kmask, tiled, 0.0).astype(BF16)

    @pl.when(pl.program_id(2) == 0)
    def _():
        sr_ref[...] = jnp.zeros_like(sr_ref)
        si_ref[...] = jnp.zeros_like(si_ref)

    rows = v_ref.shape[0]
    half = rows // 2
    apr = apr_ref[0]
    api = api_ref[0]
    row = lax.broadcasted_iota(jnp.int32, (8, ns), 0)

    def shift(z, k, fill):
        return jnp.where(row >= k, pltpu.roll(z, k, 0), fill)

    def scan_block(i, carry):
        sr, si = carry
        zr = v_ref[8 * i:8 * i + 8, 0:ns]
        zi = v_ref[8 * i:8 * i + 8, ns:2 * ns]
        for k in (1, 2, 4):
            ar, ai = apr[k - 1:k], api[k - 1:k]
            hr, hi = shift(zr, k, 0.0), shift(zi, k, 0.0)
            zr, zi = zr + ar * hr - ai * hi, zi + ar * hi + ai * hr
        fr = zr + apr * sr - api * si
        fi = zi + apr * si + api * sr
        v_ref[8 * i:8 * i + 8, 0:ns] = shift(fr, 1, sr)
        v_ref[8 * i:8 * i + 8, ns:2 * ns] = shift(fi, 1, si)
        return fr[7:8], fi[7:8]

    def outputs(lo, t):
        xs = u_ref[0, lo:lo + half, :]
        acc = jnp.dot(xs[:, :(t + 1) * cw], k_ref[(L - 1 - t) * cw:, :], preferred_element_type=F32)
        acc = acc + jnp.dot(sp_ref[lo:lo + half, :], p_ref[t], preferred_element_type=F32)
        y = acc + d_ref[0] * xs[:, t * cw:(t + 1) * cw].astype(F32)
        o_ref[0, lo:lo + half, t * cw:(t + 1) * cw] = jax.nn.gelu(y).astype(BF16)

    nblk_half = half // 8
    pieces = 4
    ncols = 2 * ns // pieces
    v_ref[:half, :] = jnp.dot(u_ref[0, :half, :], w_ref[...], preferred_element_type=F32)
    carry = (sr_ref[...], si_ref[...])
    for n in range(pieces):
        v_ref[half:, n * ncols:(n + 1) * ncols] = jnp.dot(
            u_ref[0, half:, :], w_ref[:, n * ncols:(n + 1) * ncols], preferred_element_type=F32)
        for i in range(n * nblk_half // pieces, (n + 1) * nblk_half // pieces):
            carry = scan_block(i, carry)
    sp_ref[:half, :] = v_ref[:half, :].astype(BF16)
    for t in range(L):
        outputs(0, t)
        for i in range(t * nblk_half // L, (t + 1) * nblk_half // L):
            carry = scan_block(nblk_half + i, carry)
    sr_ref[...], si_ref[...] = carry
    sp_ref[half:, :] = v_ref[half:, :].astype(BF16)
    for t in range(L):
        outputs(half, t)


def _s5_core(u_v, ops, d_skip, *, batch, seq):
    wc, kc, pc, apr, api = ops
    L = SCAN_CHUNK
    cw = MXU_WIDTH
    ncb = u_v.shape[0]
    rows_per_b = seq // L
    halves = rows_per_b // SCAN_ROWS
    ns = apr.shape[2]
    blk = pl.BlockSpec((1, SCAN_ROWS, L * cw), lambda cb, b, h: (cb, b * halves + h, 0))
    return pl.pallas_call(
        _s5_kernel,
        grid=(ncb, batch, halves),
        in_specs=[blk,
                  pl.BlockSpec((1,) + wc.shape[1:], lambda cb, b, h: (cb, 0, 0, 0, 0)),
                  pl.BlockSpec((1,) + kc.shape[1:], lambda cb, b, h: (cb, 0, 0, 0)),
                  pl.BlockSpec((1,) + pc.shape[1:], lambda cb, b, h: (cb, 0, 0, 0)),
                  pl.BlockSpec((1, 8, ns), lambda cb, b, h: (cb, 0, 0)),
                  pl.BlockSpec((1, 8, ns), lambda cb, b, h: (cb, 0, 0)),
                  pl.BlockSpec((1, 1, cw), lambda cb, b, h: (cb, 0, 0))],
        out_specs=blk,
        out_shape=jax.ShapeDtypeStruct(u_v.shape, BF16),
        scratch_shapes=[pltpu.VMEM((L * cw, 2 * ns), BF16),
                        pltpu.VMEM((L * cw, cw), BF16),
                        pltpu.VMEM((L, 2 * ns, cw), BF16),
                        pltpu.VMEM((SCAN_ROWS, 2 * ns), F32),
                        pltpu.VMEM((SCAN_ROWS, 2 * ns), BF16),
                        pltpu.VMEM((1, ns), F32),
                        pltpu.VMEM((1, ns), F32)],
        compiler_params=_params("arbitrary", "arbitrary", "arbitrary"),
        name="s5_core",
    )(u_v, wc, kc, pc, apr, api, d_skip.reshape(ncb, 1, cw))


def _s5_operators(a_re, a_im, log_dt, b_re, b_im, c_re, c_im):
    L = SCAN_CHUNK
    groups, p = a_re.shape
    gl = MXU_WIDTH // SSM_GROUP
    ncb = groups // gl
    dt = jnp.exp(log_dt.astype(F32))[:, None]
    lr, li = a_re.astype(F32), a_im.astype(F32)
    mag = jnp.exp(lr * dt)
    ab_re = mag * jnp.cos(li * dt)
    ab_im = mag * jnp.sin(li * dt)
    den = lr * lr + li * li
    coef_re = ((ab_re - 1.0) * lr + ab_im * li) / den
    coef_im = (ab_im * lr - (ab_re - 1.0) * li) / den
    br, bi = b_re.astype(F32), b_im.astype(F32)
    bb_re = coef_re[..., None] * br - coef_im[..., None] * bi
    bb_im = coef_re[..., None] * bi + coef_im[..., None] * br
    cr, ci = c_re.astype(F32), c_im.astype(F32)

    def apow(m):
        m = jnp.asarray(m, F32)[:, None, None]
        mg = jnp.exp(m * (lr * dt))
        return mg * jnp.cos(m * (li * dt)), mg * jnp.sin(m * (li * dt))

    pr, pi = apow(np.arange(L))
    ca_re = cr[None] * pr[:, :, None, :] - ci[None] * pi[:, :, None, :]
    ca_im = cr[None] * pi[:, :, None, :] + ci[None] * pr[:, :, None, :]
    km = jnp.einsum("mgap,gpc->mgca", ca_re, bb_re) - jnp.einsum("mgap,gpc->mgca", ca_im, bb_im)
    km = km[::-1].reshape(L, ncb, gl, SSM_GROUP, SSM_GROUP)
    kc = km.transpose(1, 0, 3, 2, 4).reshape(ncb, L, SSM_GROUP, MXU_WIDTH)

    pr, pi = apow(np.arange(L - 1, -1, -1))
    w_re = pr[..., None] * bb_re[None] - pi[..., None] * bb_im[None]
    w_im = pr[..., None] * bb_im[None] + pi[..., None] * bb_re[None]
    wc = jnp.stack([w_re, w_im], axis=1).transpose(0, 1, 2, 4, 3)
    wc = wc.reshape(L, 2, ncb, MXU_WIDTH, p).transpose(2, 0, 1, 3, 4)
    wc = jnp.concatenate([wc] * (LANES // p), axis=-1)

    pr, pi = apow(np.arange(1, L + 1))
    q_re = cr[None] * pr[:, :, None, :] - ci[None] * pi[:, :, None, :]
    q_im = cr[None] * pi[:, :, None, :] + ci[None] * pr[:, :, None, :]
    pc = jnp.stack([q_re, -q_im], axis=1).reshape(L, 2, ncb, gl, SSM_GROUP, p)
    pc = pc.transpose(2, 0, 1, 5, 3, 4).reshape(ncb, L, 2 * p, MXU_WIDTH)

    pr, pi = apow(L * np.arange(1, 9))
    apr = pr.reshape(8, ncb, gl * p).transpose(1, 0, 2)
    api = pi.reshape(8, ncb, gl * p).transpose(1, 0, 2)
    return wc.astype(BF16), kc.astype(BF16), pc.astype(BF16), apr, api


def _bucket_tiles(t):
    i = np.arange(t)[:, None]
    j = np.arange(t)[None, :]
    tiles = []
    for off in (0, t):
        rel = j + off - i
        n = np.maximum(rel, 0)
        max_exact = NUM_BUCKETS // 2
        large = max_exact + (np.log(np.maximum(n, max_exact).astype(np.float32) / max_exact)
                             / math.log(MAX_DISTANCE / max_exact)
                             * (NUM_BUCKETS - max_exact)).astype(np.int32)
        large = np.minimum(large, NUM_BUCKETS - 1)
        b = np.where(n < max_exact, n, large)
        tiles.append(np.where(rel >= 0, b, -1).astype(np.int32))
    return np.stack(tiles)


def _bias_kernel(bkt_ref, tab_ref, o_ref):
    h = pl.program_id(0)
    far = tab_ref[NUM_BUCKETS - 1, h]
    for typ in range(2):
        b = bkt_ref[typ]
        acc = jnp.zeros(b.shape, F32)
        for i in range(NUM_BUCKETS - 1):
            acc = jnp.where(b == i, (tab_ref[i, h] - far) * LOG2E, acc)
        o_ref[0, typ] = jnp.where(b < 0, NEG_INF, acc)


def _bias_tiles(rel_bias, t):
    heads = rel_bias.shape[1]
    return pl.pallas_call(
        _bias_kernel,
        grid=(heads,),
        in_specs=[_const_spec((2, t, t)), pl.BlockSpec(memory_space=pltpu.SMEM)],
        out_specs=pl.BlockSpec((1, 2, t, t), lambda h: (h, 0, 0, 0)),
        out_shape=jax.ShapeDtypeStruct((heads, 2, t, t), F32),
        compiler_params=_params("parallel"),
        name="rel_bias_tiles",
    )(jnp.asarray(_bucket_tiles(t)), rel_bias.astype(F32))


def _flash_kernel(q_ref, k_ref, vt_ref, bias_ref, lam_ref, hn_ref, o_ref,
                  qcat_ref, sa_ref, sb_ref, m_ref, l_ref, acc_ref, *, lam_init):
    tq = q_ref.shape[2]
    tk = bias_ref.shape[2]
    cw = MXU_WIDTH
    per_branch = tq // cw
    ncol = 2 * per_branch
    per_key = tk // cw
    inner = tq // tk
    assert inner == 2, "the far-tile loop below relies on an odd number of far tiles"
    qi = pl.program_id(2)
    qt = q_ref[0]
    row = lax.broadcasted_iota(jnp.int32, qt.shape, 0)
    zero = jnp.zeros_like(qt)
    qcat_ref[:, :tq] = jnp.where(row < HEAD_DIM, qt, zero)
    qcat_ref[:, tq:] = jnp.where(row >= HEAD_DIM, qt, zero)

    m_ref[...] = jnp.full(m_ref.shape, NEG_INF, F32)
    l_ref[...] = jnp.zeros_like(l_ref)
    acc_ref[...] = jnp.zeros_like(acc_ref)

    def bias_of(kind, c):
        if kind is None:
            return None
        cq = c % per_branch
        dist = cq // per_key - kind
        if dist < 0:
            return "skip"
        if dist > 1:
            return None
        return dist, (cq % per_key) * cw

    def scores_col(kblk, dst_ref, kind, c):
        spec = bias_of(kind, c)
        if spec == "skip":
            return None
        s = jnp.dot(kblk, qcat_ref[:, c * cw:(c + 1) * cw], preferred_element_type=F32)
        if spec is not None:
            s = s + bias_ref[0, spec[0], :, spec[1]:spec[1] + cw]
        dst_ref[c] = s
        return jnp.max(s, axis=0, keepdims=True)

    def softmax_pv_col(vblk, src_ref, cm, kind, c):
        if bias_of(kind, c) == "skip":
            return
        s = src_ref[c]
        m_old = m_ref[c]
        m_new = jnp.maximum(m_old, cm)
        alpha = jnp.exp2(m_old - m_new)
        p = jnp.exp2(s - m_new)
        l_ref[c] = alpha * l_ref[c] + jnp.sum(p, axis=0, keepdims=True)
        m_ref[c] = m_new
        acc_ref[c] = alpha * acc_ref[c] + jnp.dot(vblk, p.astype(BF16), preferred_element_type=F32)

    def key_tile(j):
        return k_ref[0, pl.ds(pl.multiple_of(j * tk, tk), tk), :]

    def scores(j, dst_ref, kind):
        kblk = key_tile(j)
        return tuple(scores_col(kblk, dst_ref, kind, c) for c in range(ncol))

    def step(nxt, cur, cmax):
        j, kind, src_ref = cur
        vblk = vt_ref[0, 0, j]
        kblk = key_tile(nxt[0]) if nxt is not None else None
        cm_next = []
        for c in range(ncol):
            if nxt is not None:
                cm_next.append(scores_col(kblk, nxt[2], nxt[1], c))
            softmax_pv_col(vblk, src_ref, cmax[c], kind, c)
        return tuple(cm_next)

    def run(tiles, first_max):
        bufs = (sa_ref, sb_ref)
        cmax = first_max
        for n, (j, kind) in enumerate(tiles):
            nxt = None
            if n + 1 < len(tiles):
                nxt = tiles[n + 1] + (bufs[(n + 1) % 2],)
            cmax = step(nxt, (j, kind, bufs[n % 2]), cmax)

    first_diag = qi * inner
    diag = [(first_diag + r, r) for r in range(inner)]

    @pl.when(qi == 0)
    def _():
        run(diag, scores(0, sa_ref, 0))

    @pl.when(qi >= 1)
    def _():
        n_far = first_diag - 1

        def pair(i, cm_a):
            j = 2 * i
            cm_b = step((j + 1, None, sb_ref), (j, None, sa_ref), cm_a)
            return step((j + 2, None, sa_ref), (j + 1, None, sb_ref), cm_b)

        cm_a = lax.fori_loop(0, (n_far - 1) // 2, pair, scores(0, sa_ref, None))
        run([(n_far - 1, None), (n_far, -1)] + diag, cm_a)

    lv = lam_ref[...]
    lam = (jnp.exp(jnp.sum(lv[0:1] * lv[1:2], axis=1, keepdims=True))
           - jnp.exp(jnp.sum(lv[2:3] * lv[3:4], axis=1, keepdims=True)) + lam_init)
    for c in range(per_branch):
        o1 = acc_ref[c] * (1.0 / l_ref[c])
        o2 = acc_ref[per_branch + c] * (1.0 / l_ref[per_branch + c])
        o = o1 - lam * o2
        o = o * lax.rsqrt(jnp.mean(o * o, axis=0, keepdims=True) + NORM_EPS)
        o = o * hn_ref[...] * (1.0 - lam_init)
        o_ref[0, c * cw:(c + 1) * cw, :] = o.T.astype(BF16)


def _diff_attention(qt, k, vt, bias, lam_vecs, head_norm, *, lam_init):
    batch, d, seq = qt.shape
    heads = d // V_DIM
    t = ATTN_TILE
    tq = ATTN_QTILE
    nblk = seq // t
    ncol = 2 * tq // MXU_WIDTH
    lam_pad = jnp.zeros((8, V_DIM), F32).at[:4, :HEAD_DIM].set(lam_vecs.astype(F32))
    return pl.pallas_call(
        functools.partial(_flash_kernel, lam_init=lam_init),
        grid=(batch, heads, seq // tq),
        in_specs=[pl.BlockSpec((1, V_DIM, tq), lambda b, h, i: (b, h, i)),
                  pl.BlockSpec((1, seq, V_DIM), lambda b, h, i: (b, 0, h)),
                  pl.BlockSpec((1, 1, nblk, V_DIM, t), lambda b, h, i: (b, h, 0, 0, 0)),
                  pl.BlockSpec((1, 2, t, t), lambda b, h, i: (h, 0, 0, 0)),
                  _const_spec((8, V_DIM)),
                  _const_spec((V_DIM, 1))],
        out_specs=pl.BlockSpec((1, tq, V_DIM), lambda b, h, i: (b, i, h)),
        out_shape=jax.ShapeDtypeStruct((batch, seq, d), BF16),
        scratch_shapes=[pltpu.VMEM((V_DIM, 2 * tq), BF16),
                        pltpu.VMEM((ncol, t, MXU_WIDTH), F32),
                        pltpu.VMEM((ncol, t, MXU_WIDTH), F32),
                        pltpu.VMEM((ncol, 1, MXU_WIDTH), F32),
                        pltpu.VMEM((ncol, 1, MXU_WIDTH), F32),
                        pltpu.VMEM((ncol, V_DIM, MXU_WIDTH), F32)],
        compiler_params=_params("parallel", "parallel", "arbitrary"),
        name="diff_attention",
    )(qt, k.reshape(batch, seq, d), vt, bias, lam_pad, head_norm.astype(F32).reshape(V_DIM, 1))


def kernel(x, norm_mixer_pre, norm_mixer_post, norm_mlp_pre, norm_mlp_post, mlp_w_up, mlp_w_down,
           ssm_w_in, ssm_a_re, ssm_a_im, ssm_log_dt, ssm_b_re, ssm_b_im, ssm_c_re, ssm_c_im, ssm_d,
           ssm_w_glu, kv_norm, w_kv, attn_w_q, attn_lambda_q1, attn_lambda_k1, attn_lambda_q2,
           attn_lambda_k2, attn_head_norm, attn_w_o, rel_bias):
    batch, seq, d = x.shape
    depth = norm_mixer_pre.shape[0]
    n_ssm = ssm_w_in.shape[0]
    assert 0 < n_ssm < depth
    xs = x.reshape(batch * seq, d).astype(F32)
    q_scale = HEAD_DIM ** -0.5 * LOG2E
    w_up, w_down, w_in, w_glu, w_q, w_o = (
        w.astype(BF16) for w in (mlp_w_up, mlp_w_down, ssm_w_in, ssm_w_glu, attn_w_q, attn_w_o))

    def mixer_projections(layer):
        if layer >= depth:
            return []
        if layer < n_ssm:
            return [("colblock", norm_mixer_pre[layer], (w_in, layer), 1.0)]
        projs = []
        if layer == n_ssm:
            projs.append(("kv", kv_norm, w_kv.astype(BF16), 1.0))
        projs.append(("qT", norm_mixer_pre[layer], (w_q, layer - n_ssm), q_scale))
        return projs

    feeds = [_norm_proj(xs, norm_mixer_pre[0], w_in[0], mode="colblock", batch=batch, seq=seq)]
    bias = _bias_tiles(rel_bias, ATTN_TILE)
    k = vt = None
    for layer in range(depth):
        if layer < n_ssm:
            i = layer
            ops = _s5_operators(ssm_a_re[i], ssm_a_im[i], ssm_log_dt[i], ssm_b_re[i], ssm_b_im[i],
                                ssm_c_re[i], ssm_c_im[i])
            y = _s5_core(feeds[0], ops, ssm_d[i].astype(F32), batch=batch, seq=seq)
            w_out = (w_glu, i)
        else:
            if layer == n_ssm:
                k, vt = feeds[0], feeds[1]
            j = layer - n_ssm
            lam_init = 0.8 - 0.6 * math.exp(-0.3 * layer)
            lam_vecs = jnp.stack([attn_lambda_q1[j], attn_lambda_k1[j],
                                  attn_lambda_q2[j], attn_lambda_k2[j]])
            y = _diff_attention(feeds[-1], k, vt, bias, lam_vecs, attn_head_norm[j], lam_init=lam_init)
            y = y.reshape(batch * seq, d)
            w_out = (w_o, j)
        xs, feeds = _layer_tail(y, w_out, xs, norm_mixer_post[layer], norm_mlp_pre[layer],
                                (w_up, layer), (w_down, layer),
                                norm_mlp_post[layer], mixer_projections(layer + 1),
                                glu=layer < n_ssm, chunked=layer < n_ssm, batch=batch, seq=seq)
    return xs.reshape(batch, seq, d).astype(x.dtype)
```

```python
import functools
import math

import numpy as np
import jax
import jax.numpy as jnp
from jax import lax
from jax.experimental import pallas as pl
from jax.experimental.pallas import tpu as pltpu

F32 = jnp.float32
BF16 = jnp.bfloat16

SSM_GROUP = 16
SSM_STATE = 64
HEAD_DIM = 64
V_DIM = 2 * HEAD_DIM
NUM_BUCKETS = 32
MAX_DISTANCE = 128
NORM_EPS = 1e-6
NEG_INF = -1e30
LOG2E = math.log2(math.e)

MXU_WIDTH = 256
LANES = 128
ROW_TILE = 512
SCAN_CHUNK = 8
SCAN_ROWS = 512
ATTN_TILE = 512
ATTN_QTILE = 1024
FF_CHUNK = 1024
VMEM_LIMIT = 56 * 1024 * 1024


def _params(*sem, flags=None):
    return pltpu.CompilerParams(dimension_semantics=sem, vmem_limit_bytes=VMEM_LIMIT, flags=flags)


def _rms(xf, gain):
    return xf * lax.rsqrt(jnp.mean(xf * xf, axis=-1, keepdims=True) + NORM_EPS) * gain


def _const_spec(shape):
    zeros = (0,) * len(shape)
    return pl.BlockSpec(shape, lambda *_: zeros, pipeline_mode=pl.Buffered(1))


def _project(x, g_ref, w_ref, out_refs, slab_ref, mode, scale):
    hn = _rms(x, g_ref[...]).astype(BF16)
    acc = jnp.dot(hn, w_ref[...], preferred_element_type=F32)
    rows = acc.shape[0]
    if mode == "colblock":
        (o_ref,) = out_refs
        L = SCAN_CHUNK
        for sl in range(slab_ref.shape[0]):
            slab_ref[sl] = acc[:, sl * LANES:(sl + 1) * LANES]
        for cb in range(o_ref.shape[0]):
            for t in range(L):
                for hf in range(MXU_WIDTH // LANES):
                    lo = t * MXU_WIDTH + hf * LANES
                    piece = slab_ref[cb * (MXU_WIDTH // LANES) + hf, pl.ds(t, rows // L, stride=L), :]
                    o_ref[cb, :, lo:lo + LANES] = piece.astype(BF16)
    elif mode == "qT":
        (o_ref,) = out_refs
        o_ref[0] = (acc * scale).T.astype(BF16)
    else:
        k_ref, vt_ref = out_refs
        d = k_ref.shape[1]
        k_ref[...] = acc[:, :d].astype(BF16)
        vt = acc[:, d:].T.astype(BF16)
        vt_ref[0, :, 0] = vt.reshape(d // V_DIM, V_DIM, rows)


def _proj_outputs(mode, n, d, nout, batch, seq):
    tm = ROW_TILE
    per_b = seq // tm
    if mode == "colblock":
        ncb = nout // MXU_WIDTH
        L = SCAN_CHUNK
        return ([jax.ShapeDtypeStruct((ncb, n // L, L * MXU_WIDTH), BF16)],
                [pl.BlockSpec((ncb, tm // L, L * MXU_WIDTH), lambda i: (0, i, 0))])
    if mode == "qT":
        return ([jax.ShapeDtypeStruct((batch, nout, seq), BF16)],
                [pl.BlockSpec((1, nout, tm), lambda i: (i // per_b, 0, i % per_b))])
    heads = d // V_DIM
    assert mode == "kv" and tm == ATTN_TILE
    return ([jax.ShapeDtypeStruct((n, d), BF16),
             jax.ShapeDtypeStruct((batch, heads, per_b, V_DIM, tm), BF16)],
            [pl.BlockSpec((tm, d), lambda i: (i, 0)),
             pl.BlockSpec((1, heads, 1, V_DIM, tm), lambda i: (i // per_b, 0, i % per_b, 0, 0))])


def _slab_scratch(d):
    return [pltpu.VMEM((d // LANES, ROW_TILE, LANES), F32), pltpu.VMEM((ROW_TILE, d), BF16)]


def _proj_kernel(x_ref, g_ref, w_ref, *refs, mode, scale):
    slab_ref = refs[-2] if mode == "colblock" else None
    nout = 2 if mode == "kv" else 1
    _project(x_ref[...], g_ref, w_ref, refs[:nout], slab_ref, mode, scale)


def _norm_proj(x, gain, w, *, mode, batch, seq, scale=1.0):
    n, d = x.shape
    nout = w.shape[1]
    tm = ROW_TILE
    out_shape, out_specs = _proj_outputs(mode, n, d, nout, batch, seq)
    out = pl.pallas_call(
        functools.partial(_proj_kernel, mode=mode, scale=scale),
        grid=(n // tm,),
        in_specs=[pl.BlockSpec((tm, d), lambda i: (i, 0)),
                  _const_spec((1, d)),
                  _const_spec((d, nout))],
        out_specs=out_specs,
        out_shape=out_shape,
        scratch_shapes=_slab_scratch(nout) if mode == "colblock" else [],
        compiler_params=_params("parallel"),
        name="norm_proj_" + mode,
    )(x, gain.reshape(1, d), w)
    return out[0] if len(out) == 1 else out


def _mixer_out(y_ref, w_ref, slab_ref, tok_ref, *, glu, chunked, d):
    if chunked:
        L = SCAN_CHUNK
        nchunk = y_ref.shape[1]
        for cb in range(y_ref.shape[0]):
            for t in range(L):
                for hf in range(MXU_WIDTH // LANES):
                    lo = t * MXU_WIDTH + hf * LANES
                    slab_ref[cb * (MXU_WIDTH // LANES) + hf, pl.ds(t, nchunk, stride=L), :] = (
                        y_ref[cb, :, lo:lo + LANES].astype(F32))
        for sl in range(slab_ref.shape[0]):
            tok_ref[:, sl * LANES:(sl + 1) * LANES] = slab_ref[sl].astype(BF16)
        acc = jnp.dot(tok_ref[...], w_ref[...], preferred_element_type=F32)
    else:
        acc = jnp.dot(y_ref[...], w_ref[...], preferred_element_type=F32)
    if glu:
        acc = acc[:, :d] * jax.nn.sigmoid(acc[:, d:])
    return acc


def _mlp_block(x, g1_ref, wu_ref, wd_ref, g2_ref):
    hn = _rms(x, g1_ref[...]).astype(BF16)
    acc = None
    for c in range(wu_ref.shape[1] // FF_CHUNK):
        a = jnp.dot(hn, wu_ref[:, c * FF_CHUNK:(c + 1) * FF_CHUNK], preferred_element_type=F32)
        a = jnp.maximum(a, 0.0)
        a = (a * a).astype(BF16)
        part = jnp.dot(a, wd_ref[c * FF_CHUNK:(c + 1) * FF_CHUNK, :], preferred_element_type=F32)
        acc = part if acc is None else acc + part
    return x + _rms(acc, g2_ref[...])


def _tail_kernel(*refs, glu, chunked, projs, use_slab):
    n_in = 8 + 2 * len(projs)
    y_ref, wo_ref, x_ref, gpost_ref, gpre_ref, wu_ref, wd_ref, gmlp_ref = refs[:8]
    proj_in = refs[8:n_in]
    n_out = 1 + sum(2 if mode == "kv" else 1 for mode, _ in projs)
    o_ref = refs[n_in]
    proj_out = refs[n_in + 1:n_in + n_out]
    slab_ref, tok_ref = refs[n_in + n_out:] if use_slab else (None, None)
    d = o_ref.shape[1]
    acc = _mixer_out(y_ref, wo_ref, slab_ref, tok_ref, glu=glu, chunked=chunked, d=d)
    x = x_ref[...] + _rms(acc, gpost_ref[...])
    x = _mlp_block(x, gpre_ref, wu_ref, wd_ref, gmlp_ref)
    o_ref[...] = x
    k = 0
    for i, (mode, scale) in enumerate(projs):
        cnt = 2 if mode == "kv" else 1
        _project(x, proj_in[2 * i], proj_in[2 * i + 1], proj_out[k:k + cnt], slab_ref, mode, scale)
        k += cnt


def _layer_weight(w):
    if isinstance(w, tuple):
        stack, layer = w
        return stack, pl.BlockSpec((None,) + stack.shape[1:], lambda *_: (layer, 0, 0),
                                   pipeline_mode=pl.Buffered(1))
    return w, _const_spec(w.shape)


def _layer_tail(y, w_out, x, g_post, g_pre, w_up, w_down, g_mlp, projs, *, glu, chunked, batch, seq):
    n, d = x.shape
    tm = ROW_TILE
    row = pl.BlockSpec((tm, d), lambda i: (i, 0))
    if chunked:
        ncb, _, width = y.shape
        y_spec = pl.BlockSpec((ncb, tm // SCAN_CHUNK, width), lambda i: (0, i, 0))
    else:
        y_spec = pl.BlockSpec((tm, y.shape[1]), lambda i: (i, 0))
    (w_out, wo_spec), (w_up, wu_spec), (w_down, wd_spec) = map(_layer_weight, (w_out, w_up, w_down))
    in_specs = [y_spec, wo_spec, row, _const_spec((1, d)), _const_spec((1, d)),
                wu_spec, wd_spec, _const_spec((1, d))]
    args = [y, w_out, x, g_post.reshape(1, d), g_pre.reshape(1, d), w_up, w_down, g_mlp.reshape(1, d)]
    out_shape = [jax.ShapeDtypeStruct((n, d), F32)]
    out_specs = [row]
    for mode, gain, w, _ in projs:
        w, w_spec = _layer_weight(w)
        in_specs += [_const_spec((1, d)), w_spec]
        args += [gain.reshape(1, d), w]
        shapes, specs = _proj_outputs(mode, n, d, w.shape[-1], batch, seq)
        out_shape += shapes
        out_specs += specs
    use_slab = chunked or any(mode == "colblock" for mode, *_ in projs)
    out = pl.pallas_call(
        functools.partial(_tail_kernel, glu=glu, chunked=chunked,
                          projs=tuple((mode, scale) for mode, _, _, scale in projs), use_slab=use_slab),
        grid=(n // tm,),
        in_specs=in_specs,
        out_specs=out_specs,
        out_shape=out_shape,
        scratch_shapes=_slab_scratch(d) if use_slab else [],
        compiler_params=_params("parallel"),
        name="layer_tail",
    )(*args)
    return out[0], list(out[1:])


def _cmul(a, b):
    return a[0] * b[0] - a[1] * b[1], a[0] * b[1] + a[1] * b[0]


def _zoh(lr, li, log_dt):
    dt = jnp.exp(log_dt)
    mag = jnp.exp(lr * dt)
    ab = (mag * jnp.cos(li * dt), mag * jnp.sin(li * dt))
    den = lr * lr + li * li
    coef = (((ab[0] - 1.0) * lr + ab[1] * li) / den, (ab[1] * lr - (ab[0] - 1.0) * li) / den)
    return ab, coef


def _s5_kernel(u_ref, lamc_ref, lamr_ref, bt_ref, ct_ref, d_ref, o_ref,
               w_ref, k_ref, p_ref, apr_ref, api_ref, v_ref, sp_ref, sr_ref, si_ref):
    cw = MXU_WIDTH
    ns = v_ref.shape[1] // 2
    L = u_ref.shape[2] // cw
    grp_shift = SSM_GROUP.bit_length() - 1
    st_shift = SSM_STATE.bit_length() - 1

    @pl.when(jnp.logical_and(pl.program_id(1) == 0, pl.program_id(2) == 0))
    def _():
        def iota(shape, dim):
            return lax.broadcasted_iota(jnp.int32, shape, dim)

        def nt_dot(a, b):
            return lax.dot_general(a, b, (((1,), (1,)), ((), ())), precision=lax.Precision.HIGHEST,
                                   preferred_element_type=F32)

        wmask = (iota((cw, ns), 0) >> grp_shift) == (iota((cw, ns), 1) >> st_shift)
        pmask = (iota((ns, cw), 0) >> st_shift) == (iota((ns, cw), 1) >> grp_shift)
        kmask = (iota((cw, cw), 0) >> grp_shift) == (iota((cw, cw), 1) >> grp_shift)
        ab, coef = _zoh(lamc_ref[0, 0], lamc_ref[0, 1], lamc_ref[0, 2])
        bb = _cmul(coef, (bt_ref[0, 0], bt_ref[0, 1]))
        ct = (ct_ref[0, 0], ct_ref[0, 1])
        pw = [(jnp.ones_like(ab[0]), jnp.zeros_like(ab[0]))]
        for _ in range(L):
            pw.append(_cmul(pw[-1], ab))
        for t in range(L):
            w = _cmul(pw[L - 1 - t], bb)
            for h in range(2):
                tiled = jnp.concatenate([w[h]] * (ns // LANES), axis=1)
                w_ref[t * cw:(t + 1) * cw, h * ns:(h + 1) * ns] = (
                    jnp.where(wmask, tiled, 0.0).astype(BF16))
            km = 0.5 * (nt_dot(w[0], ct[0]) - nt_dot(w[1], ct[1]))
            k_ref[t * cw:(t + 1) * cw, :] = jnp.where(kmask, km, 0.0).astype(BF16)
            q = _cmul(ct, pw[t + 1])
            for h, qh in enumerate((q[0], -q[1])):
                piece = qh.T[:SSM_STATE, :]
                tiled = jnp.concatenate([piece] * (ns // SSM_STATE), axis=0)
                p_ref[t, h * ns:(h + 1) * ns, :] = jnp.where(pmask, tiled, 0.0).astype(BF16)
        a, _ = _zoh(lamr_ref[0, 0:1], lamr_ref[0, 1:2], lamr_ref[0, 2:3])
        al = a
        for _ in range(L - 1):
            al = _cmul(al, a)
        q = al
        for r in range(8):
            apr_ref[r:r + 1, :] = q[0]
            api_ref[r:r + 1, :] = q[1]
            q = _cmul(q, al)

    @pl.when(pl.program_id(2) == 0)
    def _():
        sr_ref[...] = jnp.zeros_like(sr_ref)
        si_ref[...] = jnp.zeros_like(si_ref)

    rows = v_ref.shape[0]
    half = rows // 2
    apr = apr_ref[...]
    api = api_ref[...]
    row = lax.broadcasted_iota(jnp.int32, (8, ns), 0)

    def shift(z, k, fill):
        return jnp.where(row >= k, pltpu.roll(z, k, 0), fill)

    def scan_block(i, carry):
        sr, si = carry
        zr = v_ref[8 * i:8 * i + 8, 0:ns]
        zi = v_ref[8 * i:8 * i + 8, ns:2 * ns]
        for k in (1, 2, 4):
            ar, ai = apr[k - 1:k], api[k - 1:k]
            hr, hi = shift(zr, k, 0.0), shift(zi, k, 0.0)
            zr, zi = zr + ar * hr - ai * hi, zi + ar * hi + ai * hr
        fr = zr + apr * sr - api * si
        fi = zi + apr * si + api * sr
        v_ref[8 * i:8 * i + 8, 0:ns] = shift(fr, 1, sr)
        v_ref[8 * i:8 * i + 8, ns:2 * ns] = shift(fi, 1, si)
        return fr[7:8], fi[7:8]

    def outputs(lo, t):
        xs = u_ref[0, lo:lo + half, :]
        acc = jnp.dot(xs[:, :(t + 1) * cw], k_ref[(L - 1 - t) * cw:, :], preferred_element_type=F32)
        acc = acc + jnp.dot(sp_ref[lo:lo + half, :], p_ref[t], preferred_element_type=F32)
        y = acc + d_ref[0] * xs[:, t * cw:(t + 1) * cw].astype(F32)
        o_ref[0, lo:lo + half, t * cw:(t + 1) * cw] = jax.nn.gelu(y).astype(BF16)

    nblk_half = half // 8
    pieces = 4
    ncols = 2 * ns // pieces
    v_ref[:half, :] = jnp.dot(u_ref[0, :half, :], w_ref[...], preferred_element_type=F32)
    carry = (sr_ref[...], si_ref[...])
    for n in range(pieces):
        v_ref[half:, n * ncols:(n + 1) * ncols] = jnp.dot(
            u_ref[0, half:, :], w_ref[:, n * ncols:(n + 1) * ncols], preferred_element_type=F32)
        for i in range(n * nblk_half // pieces, (n + 1) * nblk_half // pieces):
            carry = scan_block(i, carry)
    sp_ref[:half, :] = v_ref[:half, :].astype(BF16)
    for t in range(L):
        outputs(0, t)
        for i in range(t * nblk_half // L, (t + 1) * nblk_half // L):
            carry = scan_block(nblk_half + i, carry)
    sr_ref[...], si_ref[...] = carry
    sp_ref[half:, :] = v_ref[half:, :].astype(BF16)
    for t in range(L):
        outputs(half, t)


def _s5_core(u_v, ops, d_skip, *, batch, seq):
    lamc, lamr, bt, ct = ops
    L = SCAN_CHUNK
    cw = MXU_WIDTH
    ncb = u_v.shape[0]
    rows_per_b = seq // L
    halves = rows_per_b // SCAN_ROWS
    ns = lamr.shape[2]
    blk = pl.BlockSpec((1, SCAN_ROWS, L * cw), lambda cb, b, h: (cb, b * halves + h, 0))

    def per_block(a):
        zeros = (0,) * (a.ndim - 1)
        return pl.BlockSpec((1,) + a.shape[1:], lambda cb, b, h: (cb,) + zeros)

    return pl.pallas_call(
        _s5_kernel,
        grid=(ncb, batch, halves),
        in_specs=[blk, per_block(lamc), per_block(lamr), per_block(bt), per_block(ct),
                  pl.BlockSpec((1, 1, cw), lambda cb, b, h: (cb, 0, 0))],
        out_specs=blk,
        out_shape=jax.ShapeDtypeStruct(u_v.shape, BF16),
        scratch_shapes=[pltpu.VMEM((L * cw, 2 * ns), BF16),
                        pltpu.VMEM((L * cw, cw), BF16),
                        pltpu.VMEM((L, 2 * ns, cw), BF16),
                        pltpu.VMEM((8, ns), F32),
                        pltpu.VMEM((8, ns), F32),
                        pltpu.VMEM((SCAN_ROWS, 2 * ns), F32),
                        pltpu.VMEM((SCAN_ROWS, 2 * ns), BF16),
                        pltpu.VMEM((1, ns), F32),
                        pltpu.VMEM((1, ns), F32)],
        compiler_params=_params("arbitrary", "arbitrary", "arbitrary"),
        name="s5_core",
    )(u_v, lamc, lamr, bt, ct, d_skip.reshape(ncb, 1, cw))


def _s5_param_layout(a_re, a_im, log_dt, b_re, b_im, c_re, c_im):
    groups, p = a_re.shape
    gl = MXU_WIDTH // SSM_GROUP
    ncb = groups // gl
    lam = jnp.stack([a_re, a_im, jnp.broadcast_to(log_dt[:, None], (groups, p))]).astype(F32)
    lamc = jnp.broadcast_to(lam[:, :, None, :], (3, groups, SSM_GROUP, p)).reshape(3, ncb, MXU_WIDTH, p)
    lamr = jnp.pad(lam.reshape(3, ncb, gl * p), ((0, 5), (0, 0), (0, 0)))
    bt = jnp.stack([b_re, b_im]).astype(F32).transpose(0, 1, 3, 2).reshape(2, ncb, MXU_WIDTH, p)
    ct = jnp.stack([c_re, c_im]).astype(F32).reshape(2, ncb, MXU_WIDTH, p)

    def twice(a):
        return jnp.concatenate([a] * (LANES // p), axis=-1).transpose(1, 0, 2, 3)

    return twice(lamc), lamr.transpose(1, 0, 2), twice(bt), twice(ct)


def _bucket_tiles(t):
    i = np.arange(t)[:, None]
    j = np.arange(t)[None, :]
    tiles = []
    for off in (0, t):
        rel = j + off - i
        n = np.maximum(rel, 0)
        max_exact = NUM_BUCKETS // 2
        large = max_exact + (np.log(np.maximum(n, max_exact).astype(np.float32) / max_exact)
                             / math.log(MAX_DISTANCE / max_exact)
                             * (NUM_BUCKETS - max_exact)).astype(np.int32)
        large = np.minimum(large, NUM_BUCKETS - 1)
        b = np.where(n < max_exact, n, large)
        tiles.append(np.where(rel >= 0, b, -1).astype(np.int32))
    return np.stack(tiles)


def _bias_kernel(bkt_ref, tab_ref, o_ref):
    h = pl.program_id(0)
    far = tab_ref[NUM_BUCKETS - 1, h]
    for typ in range(2):
        b = bkt_ref[typ]
        acc = jnp.zeros(b.shape, F32)
        for i in range(NUM_BUCKETS - 1):
            acc = jnp.where(b == i, (tab_ref[i, h] - far) * LOG2E, acc)
        o_ref[0, typ] = jnp.where(b < 0, NEG_INF, acc)


def _bias_tiles(rel_bias, t):
    heads = rel_bias.shape[1]
    return pl.pallas_call(
        _bias_kernel,
        grid=(heads,),
        in_specs=[_const_spec((2, t, t)), pl.BlockSpec(memory_space=pltpu.SMEM)],
        out_specs=pl.BlockSpec((1, 2, t, t), lambda h: (h, 0, 0, 0)),
        out_shape=jax.ShapeDtypeStruct((heads, 2, t, t), F32),
        compiler_params=_params("parallel"),
        name="rel_bias_tiles",
    )(jnp.asarray(_bucket_tiles(t)), rel_bias.astype(F32))


def _flash_kernel(q_ref, k_ref, vt_ref, bias_ref, lam_ref, hn_ref, o_ref,
                  qcat_ref, sa_ref, sb_ref, m_ref, l_ref, acc_ref, *, lam_init):
    tq = q_ref.shape[2]
    tk = bias_ref.shape[2]
    cw = MXU_WIDTH
    per_branch = tq // cw
    ncol = 2 * per_branch
    per_key = tk // cw
    inner = tq // tk
    assert inner == 2, "the far-tile loop below relies on an odd number of far tiles"
    qi = pl.program_id(2)
    qt = q_ref[0]
    row = lax.broadcasted_iota(jnp.int32, qt.shape, 0)
    zero = jnp.zeros_like(qt)
    qcat_ref[:, :tq] = jnp.where(row < HEAD_DIM, qt, zero)
    qcat_ref[:, tq:] = jnp.where(row >= HEAD_DIM, qt, zero)

    m_ref[...] = jnp.full(m_ref.shape, NEG_INF, F32)
    l_ref[...] = jnp.zeros_like(l_ref)
    acc_ref[...] = jnp.zeros_like(acc_ref)

    def bias_of(kind, c):
        if kind is None:
            return None
        cq = c % per_branch
        dist = cq // per_key - kind
        if dist < 0:
            return "skip"
        if dist > 1:
            return None
        return dist, (cq % per_key) * cw

    def scores_col(kblk, dst_ref, kind, c):
        spec = bias_of(kind, c)
        if spec == "skip":
            return None
        s = jnp.dot(kblk, qcat_ref[:, c * cw:(c + 1) * cw], preferred_element_type=F32)
        if spec is not None:
            s = s + bias_ref[0, spec[0], :, spec[1]:spec[1] + cw]
        dst_ref[c] = s
        return jnp.max(s, axis=0, keepdims=True)

    def softmax_pv_col(vblk, src_ref, cm, kind, c):
        if bias_of(kind, c) == "skip":
            return
        s = src_ref[c]
        m_old = m_ref[c]
        m_new = jnp.maximum(m_old, cm)
        alpha = jnp.exp2(m_old - m_new)
        p = jnp.exp2(s - m_new)
        l_ref[c] = alpha * l_ref[c] + jnp.sum(p, axis=0, keepdims=True)
        m_ref[c] = m_new
        acc_ref[c] = alpha * acc_ref[c] + jnp.dot(vblk, p.astype(BF16), preferred_element_type=F32)

    def key_tile(j):
        return k_ref[0, pl.ds(pl.multiple_of(j * tk, tk), tk), :]

    def scores(j, dst_ref, kind):
        kblk = key_tile(j)
        return tuple(scores_col(kblk, dst_ref, kind, c) for c in range(ncol))

    def step(nxt, cur, cmax):
        j, kind, src_ref = cur
        vblk = vt_ref[0, 0, j]
        kblk = key_tile(nxt[0]) if nxt is not None else None
        cm_next = []
        for c in range(ncol):
            if nxt is not None:
                cm_next.append(scores_col(kblk, nxt[2], nxt[1], c))
            softmax_pv_col(vblk, src_ref, cmax[c], kind, c)
        return tuple(cm_next)

    def run(tiles, first_max):
        bufs = (sa_ref, sb_ref)
        cmax = first_max
        for n, (j, kind) in enumerate(tiles):
            nxt = None
            if n + 1 < len(tiles):
                nxt = tiles[n + 1] + (bufs[(n + 1) % 2],)
            cmax = step(nxt, (j, kind, bufs[n % 2]), cmax)

    first_diag = qi * inner
    diag = [(first_diag + r, r) for r in range(inner)]

    @pl.when(qi == 0)
    def _():
        run(diag, scores(0, sa_ref, 0))

    @pl.when(qi >= 1)
    def _():
        n_far = first_diag - 1

        def pair(i, cm_a):
            j = 2 * i
            cm_b = step((j + 1, None, sb_ref), (j, None, sa_ref), cm_a)
            return step((j + 2, None, sa_ref), (j + 1, None, sb_ref), cm_b)

        cm_a = lax.fori_loop(0, (n_far - 1) // 2, pair, scores(0, sa_ref, None))
        run([(n_far - 1, None), (n_far, -1)] + diag, cm_a)

    lv = lam_ref[...]
    lam = (jnp.exp(jnp.sum(lv[0:1] * lv[1:2], axis=1, keepdims=True))
           - jnp.exp(jnp.sum(lv[2:3] * lv[3:4], axis=1, keepdims=True)) + lam_init)
    for c in range(per_branch):
        o1 = acc_ref[c] * (1.0 / l_ref[c])
        o2 = acc_ref[per_branch + c] * (1.0 / l_ref[per_branch + c])
        o = o1 - lam * o2
        o = o * lax.rsqrt(jnp.mean(o * o, axis=0, keepdims=True) + NORM_EPS)
        o = o * hn_ref[...] * (1.0 - lam_init)
        o_ref[0, c * cw:(c + 1) * cw, :] = o.T.astype(BF16)


def _diff_attention(qt, k, vt, bias, lam_vecs, head_norm, *, lam_init):
    batch, d, seq = qt.shape
    heads = d // V_DIM
    t = ATTN_TILE
    tq = ATTN_QTILE
    nblk = seq // t
    ncol = 2 * tq // MXU_WIDTH
    lam_pad = jnp.zeros((8, V_DIM), F32).at[:4, :HEAD_DIM].set(lam_vecs.astype(F32))
    return pl.pallas_call(
        functools.partial(_flash_kernel, lam_init=lam_init),
        grid=(batch, heads, seq // tq),
        in_specs=[pl.BlockSpec((1, V_DIM, tq), lambda b, h, i: (b, h, i)),
                  pl.BlockSpec((1, seq, V_DIM), lambda b, h, i: (b, 0, h)),
                  pl.BlockSpec((1, 1, nblk, V_DIM, t), lambda b, h, i: (b, h, 0, 0, 0)),
                  pl.BlockSpec((1, 2, t, t), lambda b, h, i: (h, 0, 0, 0)),
                  _const_spec((8, V_DIM)),
                  _const_spec((V_DIM, 1))],
        out_specs=pl.BlockSpec((1, tq, V_DIM), lambda b, h, i: (b, i, h)),
        out_shape=jax.ShapeDtypeStruct((batch, seq, d), BF16),
        scratch_shapes=[pltpu.VMEM((V_DIM, 2 * tq), BF16),
                        pltpu.VMEM((ncol, t, MXU_WIDTH), F32),
                        pltpu.VMEM((ncol, t, MXU_WIDTH), F32),
                        pltpu.VMEM((ncol, 1, MXU_WIDTH), F32),
                        pltpu.VMEM((ncol, 1, MXU_WIDTH), F32),
                        pltpu.VMEM((ncol, V_DIM, MXU_WIDTH), F32)],
        compiler_params=_params("parallel", "parallel", "arbitrary"),
        name="diff_attention",
    )(qt, k.reshape(batch, seq, d), vt, bias, lam_pad, head_norm.astype(F32).reshape(V_DIM, 1))


def kernel(x, norm_mixer_pre, norm_mixer_post, norm_mlp_pre, norm_mlp_post, mlp_w_up, mlp_w_down,
           ssm_w_in, ssm_a_re, ssm_a_im, ssm_log_dt, ssm_b_re, ssm_b_im, ssm_c_re, ssm_c_im, ssm_d,
           ssm_w_glu, kv_norm, w_kv, attn_w_q, attn_lambda_q1, attn_lambda_k1, attn_lambda_q2,
           attn_lambda_k2, attn_head_norm, attn_w_o, rel_bias):
    batch, seq, d = x.shape
    depth = norm_mixer_pre.shape[0]
    n_ssm = ssm_w_in.shape[0]
    assert 0 < n_ssm < depth
    xs = x.reshape(batch * seq, d).astype(F32)
    q_scale = HEAD_DIM ** -0.5 * LOG2E
    w_up, w_down, w_in, w_glu, w_q, w_o = (
        w.astype(BF16) for w in (mlp_w_up, mlp_w_down, ssm_w_in, ssm_w_glu, attn_w_q, attn_w_o))

    def mixer_projections(layer):
        if layer >= depth:
            return []
        if layer < n_ssm:
            return [("colblock", norm_mixer_pre[layer], (w_in, layer), 1.0)]
        projs = []
        if layer == n_ssm:
            projs.append(("kv", kv_norm, w_kv.astype(BF16), 1.0))
        projs.append(("qT", norm_mixer_pre[layer], (w_q, layer - n_ssm), q_scale))
        return projs

    feeds = [_norm_proj(xs, norm_mixer_pre[0], w_in[0], mode="colblock", batch=batch, seq=seq)]
    bias = _bias_tiles(rel_bias, ATTN_TILE)
    k = vt = None
    for layer in range(depth):
        if layer < n_ssm:
            i = layer
            ops = _s5_param_layout(ssm_a_re[i], ssm_a_im[i], ssm_log_dt[i], ssm_b_re[i], ssm_b_im[i],
                                ssm_c_re[i], ssm_c_im[i])
            y = _s5_core(feeds[0], ops, ssm_d[i].astype(F32), batch=batch, seq=seq)
            w_out = (w_glu, i)
        else:
            if layer == n_ssm:
                k, vt = feeds[0], feeds[1]
            j = layer - n_ssm
            lam_init = 0.8 - 0.6 * math.exp(-0.3 * layer)
            lam_vecs = jnp.stack([attn_lambda_q1[j], attn_lambda_k1[j],
                                  attn_lambda_q2[j], attn_lambda_k2[j]])
            y = _diff_attention(feeds[-1], k, vt, bias, lam_vecs, attn_head_norm[j], lam_init=lam_init)
            y = y.reshape(batch * seq, d)
            w_out = (w_o, j)
        xs, feeds = _layer_tail(y, w_out, xs, norm_mixer_post[layer], norm_mlp_pre[layer],
                                (w_up, layer), (w_down, layer),
                                norm_mlp_post[layer], mixer_projections(layer + 1),
                                glu=layer < n_ssm, chunked=layer < n_ssm, batch=batch, seq=seq)
    return xs.reshape(batch, seq, d).astype(x.dtype)
```

```python
import functools
import math

import numpy as np
import jax
import jax.numpy as jnp
from jax import lax
from jax.experimental import pallas as pl
from jax.experimental.pallas import tpu as pltpu

F32 = jnp.float32
BF16 = jnp.bfloat16

SSM_GROUP = 16
SSM_STATE = 64
HEAD_DIM = 64
V_DIM = 2 * HEAD_DIM
NUM_BUCKETS = 32
MAX_DISTANCE = 128
NORM_EPS = 1e-6
NEG_INF = -1e30
LOG2E = math.log2(math.e)

MXU_WIDTH = 256
LANES = 128
ROW_TILE = 512
SCAN_CHUNK = 8
SCAN_ROWS = 512
ATTN_TILE = 512
ATTN_QTILE = 1024
FF_CHUNK = 1024
VMEM_LIMIT = 56 * 1024 * 1024


def _params(*sem):
    return pltpu.CompilerParams(dimension_semantics=sem, vmem_limit_bytes=VMEM_LIMIT)


def _rms(xf, gain):
    return xf * lax.rsqrt(jnp.mean(xf * xf, axis=-1, keepdims=True) + NORM_EPS) * gain


def _const_spec(shape):
    zeros = (0,) * len(shape)
    return pl.BlockSpec(shape, lambda *_: zeros, pipeline_mode=pl.Buffered(1))


def _project(x, g_ref, w_ref, out_refs, slab_ref, mode, scale):
    hn = _rms(x, g_ref[...]).astype(BF16)
    acc = jnp.dot(hn, w_ref[...], preferred_element_type=F32)
    rows = acc.shape[0]
    if mode == "colblock":
        (o_ref,) = out_refs
        L = SCAN_CHUNK
        for sl in range(slab_ref.shape[0]):
            slab_ref[sl] = acc[:, sl * LANES:(sl + 1) * LANES]
        for cb in range(o_ref.shape[0]):
            for t in range(L):
                for hf in range(MXU_WIDTH // LANES):
                    lo = t * MXU_WIDTH + hf * LANES
                    piece = slab_ref[cb * (MXU_WIDTH // LANES) + hf, pl.ds(t, rows // L, stride=L), :]
                    o_ref[cb, :, lo:lo + LANES] = piece.astype(BF16)
    elif mode == "qT":
        (o_ref,) = out_refs
        o_ref[0] = (acc * scale).T.astype(BF16)
    else:
        k_ref, vt_ref = out_refs
        d = k_ref.shape[1]
        k_ref[...] = acc[:, :d].astype(BF16)
        vt = acc[:, d:].T.astype(BF16)
        vt_ref[0, :, 0] = vt.reshape(d // V_DIM, V_DIM, rows)


def _proj_outputs(mode, n, d, nout, batch, seq):
    tm = ROW_TILE
    per_b = seq // tm
    if mode == "colblock":
        ncb = nout // MXU_WIDTH
        L = SCAN_CHUNK
        return ([jax.ShapeDtypeStruct((ncb, n // L, L * MXU_WIDTH), BF16)],
                [pl.BlockSpec((ncb, tm // L, L * MXU_WIDTH), lambda i: (0, i, 0))])
    if mode == "qT":
        return ([jax.ShapeDtypeStruct((batch, nout, seq), BF16)],
                [pl.BlockSpec((1, nout, tm), lambda i: (i // per_b, 0, i % per_b))])
    heads = d // V_DIM
    assert mode == "kv" and tm == ATTN_TILE
    return ([jax.ShapeDtypeStruct((n, d), BF16),
             jax.ShapeDtypeStruct((batch, heads, per_b, V_DIM, tm), BF16)],
            [pl.BlockSpec((tm, d), lambda i: (i, 0)),
             pl.BlockSpec((1, heads, 1, V_DIM, tm), lambda i: (i // per_b, 0, i % per_b, 0, 0))])


def _slab_scratch(d):
    return [pltpu.VMEM((d // LANES, ROW_TILE, LANES), F32), pltpu.VMEM((ROW_TILE, d), BF16)]


def _proj_kernel(x_ref, g_ref, w_ref, *refs, mode, scale):
    slab_ref = refs[-2] if mode == "colblock" else None
    nout = 2 if mode == "kv" else 1
    _project(x_ref[...], g_ref, w_ref, refs[:nout], slab_ref, mode, scale)


def _norm_proj(x, gain, w, *, mode, batch, seq, scale=1.0):
    n, d = x.shape
    nout = w.shape[1]
    tm = ROW_TILE
    out_shape, out_specs = _proj_outputs(mode, n, d, nout, batch, seq)
    out = pl.pallas_call(
        functools.partial(_proj_kernel, mode=mode, scale=scale),
        grid=(n // tm,),
        in_specs=[pl.BlockSpec((tm, d), lambda i: (i, 0)),
                  _const_spec((1, d)),
                  _const_spec((d, nout))],
        out_specs=out_specs,
        out_shape=out_shape,
        scratch_shapes=_slab_scratch(nout) if mode == "colblock" else [],
        compiler_params=_params("parallel"),
        name="norm_proj_" + mode,
    )(x, gain.reshape(1, d), w)
    return out[0] if len(out) == 1 else out


def _mixer_out(y_ref, w_ref, slab_ref, tok_ref, *, glu, chunked, d):
    if chunked:
        L = SCAN_CHUNK
        nchunk = y_ref.shape[1]
        for cb in range(y_ref.shape[0]):
            for t in range(L):
                for hf in range(MXU_WIDTH // LANES):
                    lo = t * MXU_WIDTH + hf * LANES
                    slab_ref[cb * (MXU_WIDTH // LANES) + hf, pl.ds(t, nchunk, stride=L), :] = (
                        y_ref[cb, :, lo:lo + LANES].astype(F32))
        for sl in range(slab_ref.shape[0]):
            tok_ref[:, sl * LANES:(sl + 1) * LANES] = slab_ref[sl].astype(BF16)
        acc = jnp.dot(tok_ref[...], w_ref[...], preferred_element_type=F32)
    else:
        acc = jnp.dot(y_ref[...], w_ref[...], preferred_element_type=F32)
    if glu:
        acc = acc[:, :d] * jax.nn.sigmoid(acc[:, d:])
    return acc


def _mlp_block(x, g1_ref, wu_ref, wd_ref, g2_ref):
    hn = _rms(x, g1_ref[...]).astype(BF16)
    acc = None
    for c in range(wu_ref.shape[1] // FF_CHUNK):
        a = jnp.dot(hn, wu_ref[:, c * FF_CHUNK:(c + 1) * FF_CHUNK], preferred_element_type=F32)
        a = jnp.maximum(a, 0.0)
        a = (a * a).astype(BF16)
        part = jnp.dot(a, wd_ref[c * FF_CHUNK:(c + 1) * FF_CHUNK, :], preferred_element_type=F32)
        acc = part if acc is None else acc + part
    return x + _rms(acc, g2_ref[...])


def _tail_kernel(*refs, glu, chunked, projs, use_slab):
    n_in = 8 + 2 * len(projs)
    y_ref, wo_ref, x_ref, gpost_ref, gpre_ref, wu_ref, wd_ref, gmlp_ref = refs[:8]
    proj_in = refs[8:n_in]
    n_out = 1 + sum(2 if mode == "kv" else 1 for mode, _ in projs)
    o_ref = refs[n_in]
    proj_out = refs[n_in + 1:n_in + n_out]
    slab_ref, tok_ref = refs[n_in + n_out:] if use_slab else (None, None)
    d = o_ref.shape[1]
    acc = _mixer_out(y_ref, wo_ref, slab_ref, tok_ref, glu=glu, chunked=chunked, d=d)
    x = x_ref[...] + _rms(acc, gpost_ref[...])
    x = _mlp_block(x, gpre_ref, wu_ref, wd_ref, gmlp_ref)
    o_ref[...] = x
    k = 0
    for i, (mode, scale) in enumerate(projs):
        cnt = 2 if mode == "kv" else 1
        _project(x, proj_in[2 * i], proj_in[2 * i + 1], proj_out[k:k + cnt], slab_ref, mode, scale)
        k += cnt


def _layer_weight(w):
    if isinstance(w, tuple):
        stack, layer = w
        return stack, pl.BlockSpec((None,) + stack.shape[1:], lambda *_: (layer, 0, 0),
                                   pipeline_mode=pl.Buffered(1))
    return w, _const_spec(w.shape)


def _layer_tail(y, w_out, x, g_post, g_pre, w_up, w_down, g_mlp, projs, *, glu, chunked, batch, seq):
    n, d = x.shape
    tm = ROW_TILE
    row = pl.BlockSpec((tm, d), lambda i: (i, 0))
    if chunked:
        ncb, _, width = y.shape
        y_spec = pl.BlockSpec((ncb, tm // SCAN_CHUNK, width), lambda i: (0, i, 0))
    else:
        y_spec = pl.BlockSpec((tm, y.shape[1]), lambda i: (i, 0))
    (w_out, wo_spec), (w_up, wu_spec), (w_down, wd_spec) = map(_layer_weight, (w_out, w_up, w_down))
    in_specs = [y_spec, wo_spec, row, _const_spec((1, d)), _const_spec((1, d)),
                wu_spec, wd_spec, _const_spec((1, d))]
    args = [y, w_out, x, g_post.reshape(1, d), g_pre.reshape(1, d), w_up, w_down, g_mlp.reshape(1, d)]
    out_shape = [jax.ShapeDtypeStruct((n, d), F32)]
    out_specs = [row]
    for mode, gain, w, _ in projs:
        w, w_spec = _layer_weight(w)
        in_specs += [_const_spec((1, d)), w_spec]
        args += [gain.reshape(1, d), w]
        shapes, specs = _proj_outputs(mode, n, d, w.shape[-1], batch, seq)
        out_shape += shapes
        out_specs += specs
    use_slab = chunked or any(mode == "colblock" for mode, *_ in projs)
    out = pl.pallas_call(
        functools.partial(_tail_kernel, glu=glu, chunked=chunked,
                          projs=tuple((mode, scale) for mode, _, _, scale in projs), use_slab=use_slab),
        grid=(n // tm,),
        in_specs=in_specs,
        out_specs=out_specs,
        out_shape=out_shape,
        scratch_shapes=_slab_scratch(d) if use_slab else [],
        compiler_params=_params("parallel"),
        name="layer_tail",
    )(*args)
    return out[0], list(out[1:])


def _cmul(a, b):
    return a[0] * b[0] - a[1] * b[1], a[0] * b[1] + a[1] * b[0]


def _zoh(lr, li, log_dt):
    dt = jnp.exp(log_dt)
    mag = jnp.exp(lr * dt)
    ab = (mag * jnp.cos(li * dt), mag * jnp.sin(li * dt))
    den = lr * lr + li * li
    coef = (((ab[0] - 1.0) * lr + ab[1] * li) / den, (ab[1] * lr - (ab[0] - 1.0) * li) / den)
    return ab, coef


def _s5_kernel(u_ref, lamc_ref, lamr_ref, bt_ref, ct_ref, d_ref, o_ref,
               w_ref, k_ref, p_ref, apr_ref, api_ref, v_ref, sp_ref, sr_ref, si_ref):
    cw = MXU_WIDTH
    ns = v_ref.shape[1] // 2
    L = u_ref.shape[2] // cw
    grp_shift = SSM_GROUP.bit_length() - 1
    st_shift = SSM_STATE.bit_length() - 1

    @pl.when(jnp.logical_and(pl.program_id(1) == 0, pl.program_id(2) == 0))
    def _():
        def iota(shape, dim):
            return lax.broadcasted_iota(jnp.int32, shape, dim)

        def nt_dot(a, b):
            return lax.dot_general(a, b, (((1,), (1,)), ((), ())), precision=lax.Precision.HIGHEST,
                                   preferred_element_type=F32)

        wmask = (iota((cw, ns), 0) >> grp_shift) == (iota((cw, ns), 1) >> st_shift)
        pmask = (iota((ns, cw), 0) >> st_shift) == (iota((ns, cw), 1) >> grp_shift)
        kmask = (iota((cw, cw), 0) >> grp_shift) == (iota((cw, cw), 1) >> grp_shift)
        ab, coef = _zoh(lamc_ref[0, 0], lamc_ref[0, 1], lamc_ref[0, 2])
        bb = _cmul(coef, (bt_ref[0, 0], bt_ref[0, 1]))
        ct = (ct_ref[0, 0], ct_ref[0, 1])
        pw = [(jnp.ones_like(ab[0]), jnp.zeros_like(ab[0]))]
        for _ in range(L):
            pw.append(_cmul(pw[-1], ab))
        for t in range(L):
            w = _cmul(pw[L - 1 - t], bb)
            for h in range(2):
                tiled = jnp.concatenate([w[h]] * (ns // LANES), axis=1)
                w_ref[t * cw:(t + 1) * cw, h * ns:(h + 1) * ns] = (
                    jnp.where(wmask, tiled, 0.0).astype(BF16))
            km = 0.5 * (nt_dot(w[0], ct[0]) - nt_dot(w[1], ct[1]))
            k_ref[t * cw:(t + 1) * cw, :] = jnp.where(kmask, km, 0.0).astype(BF16)
            q = _cmul(ct, pw[t + 1])
            for h, qh in enumerate((q[0], -q[1])):
                piece = qh.T[:SSM_STATE, :]
                tiled = jnp.concatenate([piece] * (ns // SSM_STATE), axis=0)
                p_ref[t, h * ns:(h + 1) * ns, :] = jnp.where(pmask, tiled, 0.0).astype(BF16)
        a, _ = _zoh(lamr_ref[0, 0:1], lamr_ref[0, 1:2], lamr_ref[0, 2:3])
        al = a
        for _ in range(L - 1):
            al = _cmul(al, a)
        q = al
        for r in range(8):
            apr_ref[r:r + 1, :] = q[0]
            api_ref[r:r + 1, :] = q[1]
            q = _cmul(q, al)

    @pl.when(pl.program_id(2) == 0)
    def _():
        sr_ref[...] = jnp.zeros_like(sr_ref)
        si_ref[...] = jnp.zeros_like(si_ref)

    rows = v_ref.shape[0]
    half = rows // 2
    apr = apr_ref[...]
    api = api_ref[...]
    row = lax.broadcasted_iota(jnp.int32, (8, ns), 0)

    def shift(z, k, fill):
        return jnp.where(row >= k, pltpu.roll(z, k, 0), fill)

    def scan_block(i, carry):
        sr, si = carry
        zr = v_ref[8 * i:8 * i + 8, 0:ns]
        zi = v_ref[8 * i:8 * i + 8, ns:2 * ns]
        for k in (1, 2, 4):
            ar, ai = apr[k - 1:k], api[k - 1:k]
            hr, hi = shift(zr, k, 0.0), shift(zi, k, 0.0)
            zr, zi = zr + ar * hr - ai * hi, zi + ar * hi + ai * hr
        fr = zr + apr * sr - api * si
        fi = zi + apr * si + api * sr
        v_ref[8 * i:8 * i + 8, 0:ns] = shift(fr, 1, sr)
        v_ref[8 * i:8 * i + 8, ns:2 * ns] = shift(fi, 1, si)
        return fr[7:8], fi[7:8]

    def outputs(lo, t):
        xs = u_ref[0, lo:lo + half, :]
        acc = jnp.dot(xs[:, :(t + 1) * cw], k_ref[(L - 1 - t) * cw:, :], preferred_element_type=F32)
        acc = acc + jnp.dot(sp_ref[lo:lo + half, :], p_ref[t], preferred_element_type=F32)
        y = acc + d_ref[0] * xs[:, t * cw:(t + 1) * cw].astype(F32)
        o_ref[0, lo:lo + half, t * cw:(t + 1) * cw] = jax.nn.gelu(y).astype(BF16)

    nblk_half = half // 8
    pieces = 4
    ncols = 2 * ns // pieces
    v_ref[:half, :] = jnp.dot(u_ref[0, :half, :], w_ref[...], preferred_element_type=F32)
    carry = (sr_ref[...], si_ref[...])
    for n in range(pieces):
        v_ref[half:, n * ncols:(n + 1) * ncols] = jnp.dot(
            u_ref[0, half:, :], w_ref[:, n * ncols:(n + 1) * ncols], preferred_element_type=F32)
        for i in range(n * nblk_half // pieces, (n + 1) * nblk_half // pieces):
            carry = scan_block(i, carry)
    sp_ref[:half, :] = v_ref[:half, :].astype(BF16)
    for t in range(L):
        outputs(0, t)
        for i in range(t * nblk_half // L, (t + 1) * nblk_half // L):
            carry = scan_block(nblk_half + i, carry)
    sr_ref[...], si_ref[...] = carry
    sp_ref[half:, :] = v_ref[half:, :].astype(BF16)
    for t in range(L):
        outputs(half, t)


def _s5_core(u_v, ops, d_skip, *, batch, seq):
    lamc, lamr, bt, ct = ops
    L = SCAN_CHUNK
    cw = MXU_WIDTH
    ncb = u_v.shape[0]
    rows_per_b = seq // L
    halves = rows_per_b // SCAN_ROWS
    ns = lamr.shape[2]
    blk = pl.BlockSpec((1, SCAN_ROWS, L * cw), lambda cb, b, h: (cb, b * halves + h, 0))

    def per_block(a):
        zeros = (0,) * (a.ndim - 1)
        return pl.BlockSpec((1,) + a.shape[1:], lambda cb, b, h: (cb,) + zeros)

    return pl.pallas_call(
        _s5_kernel,
        grid=(ncb, batch, halves),
        in_specs=[blk, per_block(lamc), per_block(lamr), per_block(bt), per_block(ct),
                  pl.BlockSpec((1, 1, cw), lambda cb, b, h: (cb, 0, 0))],
        out_specs=blk,
        out_shape=jax.ShapeDtypeStruct(u_v.shape, BF16),
        scratch_shapes=[pltpu.VMEM((L * cw, 2 * ns), BF16),
                        pltpu.VMEM((L * cw, cw), BF16),
                        pltpu.VMEM((L, 2 * ns, cw), BF16),
                        pltpu.VMEM((8, ns), F32),
                        pltpu.VMEM((8, ns), F32),
                        pltpu.VMEM((SCAN_ROWS, 2 * ns), F32),
                        pltpu.VMEM((SCAN_ROWS, 2 * ns), BF16),
                        pltpu.VMEM((1, ns), F32),
                        pltpu.VMEM((1, ns), F32)],
        compiler_params=_params("arbitrary", "arbitrary", "arbitrary"),
        name="s5_core",
    )(u_v, lamc, lamr, bt, ct, d_skip.reshape(ncb, 1, cw))


def _s5_param_layout(a_re, a_im, log_dt, b_re, b_im, c_re, c_im):
    groups, p = a_re.shape
    gl = MXU_WIDTH // SSM_GROUP
    ncb = groups // gl
    lam = jnp.stack([a_re, a_im, jnp.broadcast_to(log_dt[:, None], (groups, p))]).astype(F32)
    lamc = jnp.broadcast_to(lam[:, :, None, :], (3, groups, SSM_GROUP, p)).reshape(3, ncb, MXU_WIDTH, p)
    lamr = jnp.pad(lam.reshape(3, ncb, gl * p), ((0, 5), (0, 0), (0, 0)))
    bt = jnp.stack([b_re, b_im]).astype(F32).transpose(0, 1, 3, 2).reshape(2, ncb, MXU_WIDTH, p)
    ct = jnp.stack([c_re, c_im]).astype(F32).reshape(2, ncb, MXU_WIDTH, p)

    def twice(a):
        return jnp.concatenate([a] * (LANES // p), axis=-1).transpose(1, 0, 2, 3)

    return twice(lamc), lamr.transpose(1, 0, 2), twice(bt), twice(ct)


def _bucket_tiles(t):
    i = np.arange(t)[:, None]
    j = np.arange(t)[None, :]
    tiles = []
    for off in (0, t):
        rel = j + off - i
        n = np.maximum(rel, 0)
        max_exact = NUM_BUCKETS // 2
        large = max_exact + (np.log(np.maximum(n, max_exact).astype(np.float32) / max_exact)
                             / math.log(MAX_DISTANCE / max_exact)
                             * (NUM_BUCKETS - max_exact)).astype(np.int32)
        large = np.minimum(large, NUM_BUCKETS - 1)
        b = np.where(n < max_exact, n, large)
        tiles.append(np.where(rel >= 0, b, -1).astype(np.int32))
    return np.stack(tiles)


def _bias_kernel(bkt_ref, tab_ref, o_ref):
    h = pl.program_id(0)
    far = tab_ref[NUM_BUCKETS - 1, h]
    for typ in range(2):
        b = bkt_ref[typ]
        acc = jnp.zeros(b.shape, F32)
        for i in range(NUM_BUCKETS - 1):
            acc = jnp.where(b == i, (tab_ref[i, h] - far) * LOG2E, acc)
        o_ref[0, typ] = jnp.where(b < 0, NEG_INF, acc)


def _bias_tiles(rel_bias, t):
    heads = rel_bias.shape[1]
    return pl.pallas_call(
        _bias_kernel,
        grid=(heads,),
        in_specs=[_const_spec((2, t, t)), pl.BlockSpec(memory_space=pltpu.SMEM)],
        out_specs=pl.BlockSpec((1, 2, t, t), lambda h: (h, 0, 0, 0)),
        out_shape=jax.ShapeDtypeStruct((heads, 2, t, t), F32),
        compiler_params=_params("parallel"),
        name="rel_bias_tiles",
    )(jnp.asarray(_bucket_tiles(t)), rel_bias.astype(F32))


def _flash_kernel(q_ref, qn_ref, k_ref, vt_ref, bias_ref, lam_ref, hn_ref, o_ref,
                  qcat_ref, qnext_ref, sa_ref, sb_ref, cm_ref, m_ref, l_ref, acc_ref, *, lam_init):
    tq = q_ref.shape[2]
    tk = bias_ref.shape[2]
    cw = MXU_WIDTH
    per_branch = tq // cw
    ncol = 2 * per_branch
    per_key = tk // cw
    inner = tq // tk
    assert inner == 2, "the far-tile loop below relies on an odd number of far tiles"
    qi = pl.program_id(2)
    row = lax.broadcasted_iota(jnp.int32, q_ref.shape[1:], 0)
    for src, dst in ((q_ref, qcat_ref), (qn_ref, qnext_ref)):
        qt = src[0]
        zero = jnp.zeros_like(qt)
        dst[:, :tq] = jnp.where(row < HEAD_DIM, qt, zero)
        dst[:, tq:] = jnp.where(row >= HEAD_DIM, qt, zero)

    m_ref[...] = jnp.full(m_ref.shape, NEG_INF, F32)
    l_ref[...] = jnp.zeros_like(l_ref)
    acc_ref[...] = jnp.zeros_like(acc_ref)

    def bias_of(kind, c):
        if kind is None:
            return None
        cq = c % per_branch
        dist = cq // per_key - kind
        if dist < 0:
            return "skip"
        if dist > 1:
            return None
        return dist, (cq % per_key) * cw

    def scores_col(kblk, dst_ref, kind, c, queries=qcat_ref):
        spec = bias_of(kind, c)
        if spec == "skip":
            return None
        s = jnp.dot(kblk, queries[:, c * cw:(c + 1) * cw], preferred_element_type=F32)
        if spec is not None:
            s = s + bias_ref[0, spec[0], :, spec[1]:spec[1] + cw]
        dst_ref[c] = s
        return jnp.max(s, axis=0, keepdims=True)

    def softmax_pv_col(vblk, src_ref, cm, kind, c):
        if bias_of(kind, c) == "skip":
            return
        s = src_ref[c]
        m_old = m_ref[c]
        m_new = jnp.maximum(m_old, cm)
        alpha = jnp.exp2(m_old - m_new)
        p = jnp.exp2(s - m_new)
        l_ref[c] = alpha * l_ref[c] + jnp.sum(p, axis=0, keepdims=True)
        m_ref[c] = m_new
        acc_ref[c] = alpha * acc_ref[c] + jnp.dot(vblk, p.astype(BF16), preferred_element_type=F32)

    def key_tile(j):
        return k_ref[0, pl.ds(pl.multiple_of(j * tk, tk), tk), :]

    def scores(j, dst_ref, kind):
        kblk = key_tile(j)
        return tuple(scores_col(kblk, dst_ref, kind, c) for c in range(ncol))

    def step(nxt, cur, cmax):
        j, kind, src_ref = cur
        vblk = vt_ref[0, 0, j]
        kblk = key_tile(nxt[0])
        cm_next = []
        for c in range(ncol):
            cm_next.append(scores_col(kblk, nxt[2], nxt[1], c, *nxt[3:]))
            softmax_pv_col(vblk, src_ref, cmax[c], kind, c)
        return tuple(cm_next)

    def run(tiles, first_max):
        assert len(tiles) % 2 == 0
        bufs = (sa_ref, sb_ref)
        cmax = first_max
        for n, (j, kind) in enumerate(tiles):
            if n + 1 < len(tiles):
                nxt = tiles[n + 1] + (bufs[(n + 1) % 2],)
            else:
                nxt = (0, None, sa_ref, qnext_ref)
            cmax = step(nxt, (j, kind, bufs[n % 2]), cmax)
        for c in range(ncol):
            cm_ref[c] = cmax[c]

    first_diag = qi * inner
    diag = [(first_diag + r, r) for r in range(inner)]

    @pl.when(qi == 0)
    def _():
        run(diag, scores(0, sa_ref, 0))

    @pl.when(qi >= 1)
    def _():
        n_far = first_diag - 1

        def pair(i, cm_a):
            j = 2 * i
            cm_b = step((j + 1, None, sb_ref), (j, None, sa_ref), cm_a)
            return step((j + 2, None, sa_ref), (j + 1, None, sb_ref), cm_b)

        cm_a = lax.fori_loop(0, (n_far - 1) // 2, pair, tuple(cm_ref[c] for c in range(ncol)))
        run([(n_far - 1, None), (n_far, -1)] + diag, cm_a)

    lv = lam_ref[...]
    lam = (jnp.exp(jnp.sum(lv[0:1] * lv[1:2], axis=1, keepdims=True))
           - jnp.exp(jnp.sum(lv[2:3] * lv[3:4], axis=1, keepdims=True)) + lam_init)
    for c in range(per_branch):
        o1 = acc_ref[c] * (1.0 / l_ref[c])
        o2 = acc_ref[per_branch + c] * (1.0 / l_ref[per_branch + c])
        o = o1 - lam * o2
        o = o * lax.rsqrt(jnp.mean(o * o, axis=0, keepdims=True) + NORM_EPS)
        o = o * hn_ref[...] * (1.0 - lam_init)
        o_ref[0, c * cw:(c + 1) * cw, :] = o.T.astype(BF16)


def _diff_attention(qt, k, vt, bias, lam_vecs, head_norm, *, lam_init):
    batch, d, seq = qt.shape
    heads = d // V_DIM
    t = ATTN_TILE
    tq = ATTN_QTILE
    nblk = seq // t
    nq = seq // tq
    ncol = 2 * tq // MXU_WIDTH
    lam_pad = jnp.zeros((8, V_DIM), F32).at[:4, :HEAD_DIM].set(lam_vecs.astype(F32))
    return pl.pallas_call(
        functools.partial(_flash_kernel, lam_init=lam_init),
        grid=(batch, heads, nq),
        in_specs=[pl.BlockSpec((1, V_DIM, tq), lambda b, h, i: (b, h, i)),
                  pl.BlockSpec((1, V_DIM, tq), lambda b, h, i: (b, h, jnp.minimum(i + 1, nq - 1))),
                  pl.BlockSpec((1, seq, V_DIM), lambda b, h, i: (b, 0, h)),
                  pl.BlockSpec((1, 1, nblk, V_DIM, t), lambda b, h, i: (b, h, 0, 0, 0)),
                  pl.BlockSpec((1, 2, t, t), lambda b, h, i: (h, 0, 0, 0)),
                  _const_spec((8, V_DIM)),
                  _const_spec((V_DIM, 1))],
        out_specs=pl.BlockSpec((1, tq, V_DIM), lambda b, h, i: (b, i, h)),
        out_shape=jax.ShapeDtypeStruct((batch, seq, d), BF16),
        scratch_shapes=[pltpu.VMEM((V_DIM, 2 * tq), BF16),
                        pltpu.VMEM((V_DIM, 2 * tq), BF16),
                        pltpu.VMEM((ncol, t, MXU_WIDTH), F32),
                        pltpu.VMEM((ncol, t, MXU_WIDTH), F32),
                        pltpu.VMEM((ncol, 1, MXU_WIDTH), F32),
                        pltpu.VMEM((ncol, 1, MXU_WIDTH), F32),
                        pltpu.VMEM((ncol, 1, MXU_WIDTH), F32),
                        pltpu.VMEM((ncol, V_DIM, MXU_WIDTH), F32)],
        compiler_params=_params("arbitrary", "arbitrary", "arbitrary"),
        name="diff_attention",
    )(qt, qt, k.reshape(batch, seq, d), vt, bias, lam_pad, head_norm.astype(F32).reshape(V_DIM, 1))


def kernel(x, norm_mixer_pre, norm_mixer_post, norm_mlp_pre, norm_mlp_post, mlp_w_up, mlp_w_down,
           ssm_w_in, ssm_a_re, ssm_a_im, ssm_log_dt, ssm_b_re, ssm_b_im, ssm_c_re, ssm_c_im, ssm_d,
           ssm_w_glu, kv_norm, w_kv, attn_w_q, attn_lambda_q1, attn_lambda_k1, attn_lambda_q2,
           attn_lambda_k2, attn_head_norm, attn_w_o, rel_bias):
    batch, seq, d = x.shape
    depth = norm_mixer_pre.shape[0]
    n_ssm = ssm_w_in.shape[0]
    assert 0 < n_ssm < depth
    xs = x.reshape(batch * seq, d).astype(F32)
    q_scale = HEAD_DIM ** -0.5 * LOG2E
    w_up, w_down, w_in, w_glu, w_q, w_o = (
        w.astype(BF16) for w in (mlp_w_up, mlp_w_down, ssm_w_in, ssm_w_glu, attn_w_q, attn_w_o))

    def mixer_projections(layer):
        if layer >= depth:
            return []
        if layer < n_ssm:
            return [("colblock", norm_mixer_pre[layer], (w_in, layer), 1.0)]
        projs = []
        if layer == n_ssm:
            projs.append(("kv", kv_norm, w_kv.astype(BF16), 1.0))
        projs.append(("qT", norm_mixer_pre[layer], (w_q, layer - n_ssm), q_scale))
        return projs

    feeds = [_norm_proj(xs, norm_mixer_pre[0], w_in[0], mode="colblock", batch=batch, seq=seq)]
    bias = _bias_tiles(rel_bias, ATTN_TILE)
    k = vt = None
    for layer in range(depth):
        if layer < n_ssm:
            i = layer
            ops = _s5_param_layout(ssm_a_re[i], ssm_a_im[i], ssm_log_dt[i], ssm_b_re[i], ssm_b_im[i],
                                ssm_c_re[i], ssm_c_im[i])
            y = _s5_core(feeds[0], ops, ssm_d[i].astype(F32), batch=batch, seq=seq)
            w_out = (w_glu, i)
        else:
            if layer == n_ssm:
                k, vt = feeds[0], feeds[1]
            j = layer - n_ssm
            lam_init = 0.8 - 0.6 * math.exp(-0.3 * layer)
            lam_vecs = jnp.stack([attn_lambda_q1[j], attn_lambda_k1[j],
                                  attn_lambda_q2[j], attn_lambda_k2[j]])
            y = _diff_attention(feeds[-1], k, vt, bias, lam_vecs, attn_head_norm[j], lam_init=lam_init)
            y = y.reshape(batch * seq, d)
            w_out = (w_o, j)
        xs, feeds = _layer_tail(y, w_out, xs, norm_mixer_post[layer], norm_mlp_pre[layer],
                                (w_up, layer), (w_down, layer),
                                norm_mlp_post[layer], mixer_projections(layer + 1),
                                glu=layer < n_ssm, chunked=layer < n_ssm, batch=batch, seq=seq)
    return xs.reshape(batch, seq, d).astype(x.dtype)
```

```python
import functools
import math

import numpy as np
import jax
import jax.numpy as jnp
from jax import lax
from jax.experimental import pallas as pl
from jax.experimental.pallas import tpu as pltpu

F32 = jnp.float32
BF16 = jnp.bfloat16

SSM_GROUP = 16
SSM_STATE = 64
HEAD_DIM = 64
V_DIM = 2 * HEAD_DIM
NUM_BUCKETS = 32
MAX_DISTANCE = 128
NORM_EPS = 1e-6
NEG_INF = -1e30
LOG2E = math.log2(math.e)

MXU_WIDTH = 256
LANES = 128
ROW_TILE = 512
SCAN_CHUNK = 8
SCAN_ROWS = 512
ATTN_TILE = 512
ATTN_QTILE = 2048
FF_CHUNK = 1024
VMEM_LIMIT = 56 * 1024 * 1024


def _params(*sem):
    return pltpu.CompilerParams(dimension_semantics=sem, vmem_limit_bytes=VMEM_LIMIT)


def _rms(xf, gain):
    return xf * lax.rsqrt(jnp.mean(xf * xf, axis=-1, keepdims=True) + NORM_EPS) * gain


def _const_spec(shape):
    zeros = (0,) * len(shape)
    return pl.BlockSpec(shape, lambda *_: zeros, pipeline_mode=pl.Buffered(1))


def _project(x, g_ref, w_ref, out_refs, slab_ref, mode, scale):
    hn = _rms(x, g_ref[...]).astype(BF16)
    acc = jnp.dot(hn, w_ref[...], preferred_element_type=F32)
    rows = acc.shape[0]
    if mode == "colblock":
        (o_ref,) = out_refs
        L = SCAN_CHUNK
        for sl in range(slab_ref.shape[0]):
            slab_ref[sl] = acc[:, sl * LANES:(sl + 1) * LANES]
        for cb in range(o_ref.shape[0]):
            for t in range(L):
                for hf in range(MXU_WIDTH // LANES):
                    lo = t * MXU_WIDTH + hf * LANES
                    piece = slab_ref[cb * (MXU_WIDTH // LANES) + hf, pl.ds(t, rows // L, stride=L), :]
                    o_ref[cb, :, lo:lo + LANES] = piece.astype(BF16)
    elif mode == "qT":
        (o_ref,) = out_refs
        o_ref[0] = (acc * scale).T.astype(BF16)
    else:
        k_ref, vt_ref = out_refs
        d = k_ref.shape[1]
        k_ref[...] = acc[:, :d].astype(BF16)
        vt = acc[:, d:].T.astype(BF16)
        tk = vt_ref.shape[4]
        for n in range(rows // tk):
            vt_ref[0, :, n] = vt[:, n * tk:(n + 1) * tk].reshape(d // V_DIM, V_DIM, tk)


def _proj_outputs(mode, n, d, nout, batch, seq):
    tm = ROW_TILE
    per_b = seq // tm
    if mode == "colblock":
        ncb = nout // MXU_WIDTH
        L = SCAN_CHUNK
        return ([jax.ShapeDtypeStruct((ncb, n // L, L * MXU_WIDTH), BF16)],
                [pl.BlockSpec((ncb, tm // L, L * MXU_WIDTH), lambda i: (0, i, 0))])
    if mode == "qT":
        return ([jax.ShapeDtypeStruct((batch, nout, seq), BF16)],
                [pl.BlockSpec((1, nout, tm), lambda i: (i // per_b, 0, i % per_b))])
    heads = d // V_DIM
    tk = ATTN_TILE
    assert mode == "kv" and tm % tk == 0
    return ([jax.ShapeDtypeStruct((n, d), BF16),
             jax.ShapeDtypeStruct((batch, heads, seq // tk, V_DIM, tk), BF16)],
            [pl.BlockSpec((tm, d), lambda i: (i, 0)),
             pl.BlockSpec((1, heads, tm // tk, V_DIM, tk), lambda i: (i // per_b, 0, i % per_b, 0, 0))])


def _slab_scratch(d):
    return [pltpu.VMEM((d // LANES, ROW_TILE, LANES), F32), pltpu.VMEM((ROW_TILE, d), BF16)]


def _proj_kernel(x_ref, g_ref, w_ref, *refs, mode, scale):
    slab_ref = refs[-2] if mode == "colblock" else None
    nout = 2 if mode == "kv" else 1
    _project(x_ref[...], g_ref, w_ref, refs[:nout], slab_ref, mode, scale)


def _norm_proj(x, gain, w, *, mode, batch, seq, scale=1.0):
    n, d = x.shape
    nout = w.shape[1]
    tm = ROW_TILE
    out_shape, out_specs = _proj_outputs(mode, n, d, nout, batch, seq)
    out = pl.pallas_call(
        functools.partial(_proj_kernel, mode=mode, scale=scale),
        grid=(n // tm,),
        in_specs=[pl.BlockSpec((tm, d), lambda i: (i, 0)),
                  _const_spec((1, d)),
                  _const_spec((d, nout))],
        out_specs=out_specs,
        out_shape=out_shape,
        scratch_shapes=_slab_scratch(nout) if mode == "colblock" else [],
        compiler_params=_params("parallel"),
        name="norm_proj_" + mode,
    )(x, gain.reshape(1, d), w)
    return out[0] if len(out) == 1 else out


def _mixer_out(y_ref, w_ref, slab_ref, tok_ref, *, glu, chunked, d):
    if chunked:
        L = SCAN_CHUNK
        nchunk = y_ref.shape[1]
        for cb in range(y_ref.shape[0]):
            for t in range(L):
                for hf in range(MXU_WIDTH // LANES):
                    lo = t * MXU_WIDTH + hf * LANES
                    slab_ref[cb * (MXU_WIDTH // LANES) + hf, pl.ds(t, nchunk, stride=L), :] = (
                        y_ref[cb, :, lo:lo + LANES].astype(F32))
        for sl in range(slab_ref.shape[0]):
            tok_ref[:, sl * LANES:(sl + 1) * LANES] = slab_ref[sl].astype(BF16)
        acc = jnp.dot(tok_ref[...], w_ref[...], preferred_element_type=F32)
    else:
        acc = jnp.dot(y_ref[...], w_ref[...], preferred_element_type=F32)
    if glu:
        acc = acc[:, :d] * jax.nn.sigmoid(acc[:, d:])
    return acc


def _mlp_block(x, g1_ref, wu_ref, wd_ref, g2_ref):
    hn = _rms(x, g1_ref[...]).astype(BF16)
    acc = None
    for c in range(wu_ref.shape[1] // FF_CHUNK):
        a = jnp.dot(hn, wu_ref[:, c * FF_CHUNK:(c + 1) * FF_CHUNK], preferred_element_type=F32)
        a = jnp.maximum(a, 0.0)
        a = (a * a).astype(BF16)
        part = jnp.dot(a, wd_ref[c * FF_CHUNK:(c + 1) * FF_CHUNK, :], preferred_element_type=F32)
        acc = part if acc is None else acc + part
    return x + _rms(acc, g2_ref[...])


def _tail_kernel(*refs, glu, chunked, projs, use_slab):
    n_in = 8 + 2 * len(projs)
    y_ref, wo_ref, x_ref, gpost_ref, gpre_ref, wu_ref, wd_ref, gmlp_ref = refs[:8]
    proj_in = refs[8:n_in]
    n_out = 1 + sum(2 if mode == "kv" else 1 for mode, _ in projs)
    o_ref = refs[n_in]
    proj_out = refs[n_in + 1:n_in + n_out]
    slab_ref, tok_ref = refs[n_in + n_out:] if use_slab else (None, None)
    d = o_ref.shape[1]
    acc = _mixer_out(y_ref, wo_ref, slab_ref, tok_ref, glu=glu, chunked=chunked, d=d)
    x = x_ref[...] + _rms(acc, gpost_ref[...])
    x = _mlp_block(x, gpre_ref, wu_ref, wd_ref, gmlp_ref)
    o_ref[...] = x
    k = 0
    for i, (mode, scale) in enumerate(projs):
        cnt = 2 if mode == "kv" else 1
        _project(x, proj_in[2 * i], proj_in[2 * i + 1], proj_out[k:k + cnt], slab_ref, mode, scale)
        k += cnt


def _layer_weight(w):
    if isinstance(w, tuple):
        stack, layer = w
        return stack, pl.BlockSpec((None,) + stack.shape[1:], lambda *_: (layer, 0, 0),
                                   pipeline_mode=pl.Buffered(1))
    return w, _const_spec(w.shape)


def _layer_tail(y, w_out, x, g_post, g_pre, w_up, w_down, g_mlp, projs, *, glu, chunked, batch, seq):
    n, d = x.shape
    tm = ROW_TILE
    row = pl.BlockSpec((tm, d), lambda i: (i, 0))
    if chunked:
        ncb, _, width = y.shape
        y_spec = pl.BlockSpec((ncb, tm // SCAN_CHUNK, width), lambda i: (0, i, 0))
    else:
        y_spec = pl.BlockSpec((tm, y.shape[1]), lambda i: (i, 0))
    (w_out, wo_spec), (w_up, wu_spec), (w_down, wd_spec) = map(_layer_weight, (w_out, w_up, w_down))
    in_specs = [y_spec, wo_spec, row, _const_spec((1, d)), _const_spec((1, d)),
                wu_spec, wd_spec, _const_spec((1, d))]
    args = [y, w_out, x, g_post.reshape(1, d), g_pre.reshape(1, d), w_up, w_down, g_mlp.reshape(1, d)]
    out_shape = [jax.ShapeDtypeStruct((n, d), F32)]
    out_specs = [row]
    for mode, gain, w, _ in projs:
        w, w_spec = _layer_weight(w)
        in_specs += [_const_spec((1, d)), w_spec]
        args += [gain.reshape(1, d), w]
        shapes, specs = _proj_outputs(mode, n, d, w.shape[-1], batch, seq)
        out_shape += shapes
        out_specs += specs
    use_slab = chunked or any(mode == "colblock" for mode, *_ in projs)
    out = pl.pallas_call(
        functools.partial(_tail_kernel, glu=glu, chunked=chunked,
                          projs=tuple((mode, scale) for mode, _, _, scale in projs), use_slab=use_slab),
        grid=(n // tm,),
        in_specs=in_specs,
        out_specs=out_specs,
        out_shape=out_shape,
        scratch_shapes=_slab_scratch(d) if use_slab else [],
        compiler_params=_params("parallel"),
        name="layer_tail",
    )(*args)
    return out[0], list(out[1:])


def _cmul(a, b):
    return a[0] * b[0] - a[1] * b[1], a[0] * b[1] + a[1] * b[0]


def _zoh(lr, li, log_dt):
    dt = jnp.exp(log_dt)
    mag = jnp.exp(lr * dt)
    ab = (mag * jnp.cos(li * dt), mag * jnp.sin(li * dt))
    den = lr * lr + li * li
    coef = (((ab[0] - 1.0) * lr + ab[1] * li) / den, (ab[1] * lr - (ab[0] - 1.0) * li) / den)
    return ab, coef


def _s5_kernel(u_ref, lamc_ref, lamr_ref, bt_ref, ct_ref, d_ref, o_ref,
               w_ref, k_ref, p_ref, apr_ref, api_ref, v_ref, sp_ref, sr_ref, si_ref):
    cw = MXU_WIDTH
    ns = v_ref.shape[1] // 2
    L = u_ref.shape[2] // cw
    grp_shift = SSM_GROUP.bit_length() - 1
    st_shift = SSM_STATE.bit_length() - 1

    @pl.when(jnp.logical_and(pl.program_id(1) == 0, pl.program_id(2) == 0))
    def _():
        def iota(shape, dim):
            return lax.broadcasted_iota(jnp.int32, shape, dim)

        def nt_dot(a, b):
            return lax.dot_general(a, b, (((1,), (1,)), ((), ())), precision=lax.Precision.HIGHEST,
                                   preferred_element_type=F32)

        wmask = (iota((cw, ns), 0) >> grp_shift) == (iota((cw, ns), 1) >> st_shift)
        pmask = (iota((ns, cw), 0) >> st_shift) == (iota((ns, cw), 1) >> grp_shift)
        kmask = (iota((cw, cw), 0) >> grp_shift) == (iota((cw, cw), 1) >> grp_shift)
        ab, coef = _zoh(lamc_ref[0, 0], lamc_ref[0, 1], lamc_ref[0, 2])
        bb = _cmul(coef, (bt_ref[0, 0], bt_ref[0, 1]))
        ct = (ct_ref[0, 0], ct_ref[0, 1])
        pw = [(jnp.ones_like(ab[0]), jnp.zeros_like(ab[0]))]
        for _ in range(L):
            pw.append(_cmul(pw[-1], ab))
        for t in range(L):
            w = _cmul(pw[L - 1 - t], bb)
            for h in range(2):
                tiled = jnp.concatenate([w[h]] * (ns // LANES), axis=1)
                w_ref[t * cw:(t + 1) * cw, h * ns:(h + 1) * ns] = (
                    jnp.where(wmask, tiled, 0.0).astype(BF16))
            km = 0.5 * (nt_dot(w[0], ct[0]) - nt_dot(w[1], ct[1]))
            k_ref[t * cw:(t + 1) * cw, :] = jnp.where(kmask, km, 0.0).astype(BF16)
            q = _cmul(ct, pw[t + 1])
            for h, qh in enumerate((q[0], -q[1])):
                piece = qh.T[:SSM_STATE, :]
                tiled = jnp.concatenate([piece] * (ns // SSM_STATE), axis=0)
                p_ref[t, h * ns:(h + 1) * ns, :] = jnp.where(pmask, tiled, 0.0).astype(BF16)
        a, _ = _zoh(lamr_ref[0, 0:1], lamr_ref[0, 1:2], lamr_ref[0, 2:3])
        al = a
        for _ in range(L - 1):
            al = _cmul(al, a)
        q = al
        for r in range(8):
            apr_ref[r:r + 1, :] = q[0]
            api_ref[r:r + 1, :] = q[1]
            q = _cmul(q, al)

    @pl.when(pl.program_id(2) == 0)
    def _():
        sr_ref[...] = jnp.zeros_like(sr_ref)
        si_ref[...] = jnp.zeros_like(si_ref)

    rows = v_ref.shape[0]
    half = rows // 2
    apr = apr_ref[...]
    api = api_ref[...]
    row = lax.broadcasted_iota(jnp.int32, (8, ns), 0)

    def shift(z, k, fill):
        return jnp.where(row >= k, pltpu.roll(z, k, 0), fill)

    def scan_block(i, carry):
        sr, si = carry
        zr = v_ref[8 * i:8 * i + 8, 0:ns]
        zi = v_ref[8 * i:8 * i + 8, ns:2 * ns]
        for k in (1, 2, 4):
            ar, ai = apr[k - 1:k], api[k - 1:k]
            hr, hi = shift(zr, k, 0.0), shift(zi, k, 0.0)
            zr, zi = zr + ar * hr - ai * hi, zi + ar * hi + ai * hr
        fr = zr + apr * sr - api * si
        fi = zi + apr * si + api * sr
        v_ref[8 * i:8 * i + 8, 0:ns] = shift(fr, 1, sr)
        v_ref[8 * i:8 * i + 8, ns:2 * ns] = shift(fi, 1, si)
        return fr[7:8], fi[7:8]

    def outputs(lo, t):
        xs = u_ref[0, lo:lo + half, :]
        acc = jnp.dot(xs[:, :(t + 1) * cw], k_ref[(L - 1 - t) * cw:, :], preferred_element_type=F32)
        acc = acc + jnp.dot(sp_ref[lo:lo + half, :], p_ref[t], preferred_element_type=F32)
        y = acc + d_ref[0] * xs[:, t * cw:(t + 1) * cw].astype(F32)
        o_ref[0, lo:lo + half, t * cw:(t + 1) * cw] = jax.nn.gelu(y).astype(BF16)

    nblk_half = half // 8
    pieces = 4
    ncols = 2 * ns // pieces
    v_ref[:half, :] = jnp.dot(u_ref[0, :half, :], w_ref[...], preferred_element_type=F32)
    carry = (sr_ref[...], si_ref[...])
    for n in range(pieces):
        v_ref[half:, n * ncols:(n + 1) * ncols] = jnp.dot(
            u_ref[0, half:, :], w_ref[:, n * ncols:(n + 1) * ncols], preferred_element_type=F32)
        for i in range(n * nblk_half // pieces, (n + 1) * nblk_half // pieces):
            carry = scan_block(i, carry)
    sp_ref[:half, :] = v_ref[:half, :].astype(BF16)
    for t in range(L):
        outputs(0, t)
        for i in range(t * nblk_half // L, (t + 1) * nblk_half // L):
            carry = scan_block(nblk_half + i, carry)
    sr_ref[...], si_ref[...] = carry
    sp_ref[half:, :] = v_ref[half:, :].astype(BF16)
    for t in range(L):
        outputs(half, t)


def _s5_core(u_v, ops, d_skip, *, batch, seq):
    lamc, lamr, bt, ct = ops
    L = SCAN_CHUNK
    cw = MXU_WIDTH
    ncb = u_v.shape[0]
    rows_per_b = seq // L
    halves = rows_per_b // SCAN_ROWS
    ns = lamr.shape[2]
    blk = pl.BlockSpec((1, SCAN_ROWS, L * cw), lambda cb, b, h: (cb, b * halves + h, 0))

    def per_block(a):
        zeros = (0,) * (a.ndim - 1)
        return pl.BlockSpec((1,) + a.shape[1:], lambda cb, b, h: (cb,) + zeros)

    return pl.pallas_call(
        _s5_kernel,
        grid=(ncb, batch, halves),
        in_specs=[blk, per_block(lamc), per_block(lamr), per_block(bt), per_block(ct),
                  pl.BlockSpec((1, 1, cw), lambda cb, b, h: (cb, 0, 0))],
        out_specs=blk,
        out_shape=jax.ShapeDtypeStruct(u_v.shape, BF16),
        scratch_shapes=[pltpu.VMEM((L * cw, 2 * ns), BF16),
                        pltpu.VMEM((L * cw, cw), BF16),
                        pltpu.VMEM((L, 2 * ns, cw), BF16),
                        pltpu.VMEM((8, ns), F32),
                        pltpu.VMEM((8, ns), F32),
                        pltpu.VMEM((SCAN_ROWS, 2 * ns), F32),
                        pltpu.VMEM((SCAN_ROWS, 2 * ns), BF16),
                        pltpu.VMEM((1, ns), F32),
                        pltpu.VMEM((1, ns), F32)],
        compiler_params=_params("arbitrary", "arbitrary", "arbitrary"),
        name="s5_core",
    )(u_v, lamc, lamr, bt, ct, d_skip.reshape(ncb, 1, cw))


def _s5_param_layout(a_re, a_im, log_dt, b_re, b_im, c_re, c_im):
    groups, p = a_re.shape
    gl = MXU_WIDTH // SSM_GROUP
    ncb = groups // gl
    lam = jnp.stack([a_re, a_im, jnp.broadcast_to(log_dt[:, None], (groups, p))]).astype(F32)
    lamc = jnp.broadcast_to(lam[:, :, None, :], (3, groups, SSM_GROUP, p)).reshape(3, ncb, MXU_WIDTH, p)
    lamr = jnp.pad(lam.reshape(3, ncb, gl * p), ((0, 5), (0, 0), (0, 0)))
    bt = jnp.stack([b_re, b_im]).astype(F32).transpose(0, 1, 3, 2).reshape(2, ncb, MXU_WIDTH, p)
    ct = jnp.stack([c_re, c_im]).astype(F32).reshape(2, ncb, MXU_WIDTH, p)

    def twice(a):
        return jnp.concatenate([a] * (LANES // p), axis=-1).transpose(1, 0, 2, 3)

    return twice(lamc), lamr.transpose(1, 0, 2), twice(bt), twice(ct)


def _bucket_tiles(t):
    i = np.arange(t)[:, None]
    j = np.arange(t)[None, :]
    tiles = []
    for off in (0, t):
        rel = j + off - i
        n = np.maximum(rel, 0)
        max_exact = NUM_BUCKETS // 2
        large = max_exact + (np.log(np.maximum(n, max_exact).astype(np.float32) / max_exact)
                             / math.log(MAX_DISTANCE / max_exact)
                             * (NUM_BUCKETS - max_exact)).astype(np.int32)
        large = np.minimum(large, NUM_BUCKETS - 1)
        b = np.where(n < max_exact, n, large)
        tiles.append(np.where(rel >= 0, b, -1).astype(np.int32))
    return np.stack(tiles)


def _bias_kernel(bkt_ref, tab_ref, o_ref):
    h = pl.program_id(0)
    far = tab_ref[NUM_BUCKETS - 1, h]
    for typ in range(2):
        b = bkt_ref[typ]
        acc = jnp.zeros(b.shape, F32)
        for i in range(NUM_BUCKETS - 1):
            acc = jnp.where(b == i, (tab_ref[i, h] - far) * LOG2E, acc)
        o_ref[0, typ] = jnp.where(b < 0, NEG_INF, acc)


def _bias_tiles(rel_bias, t):
    heads = rel_bias.shape[1]
    return pl.pallas_call(
        _bias_kernel,
        grid=(heads,),
        in_specs=[_const_spec((2, t, t)), pl.BlockSpec(memory_space=pltpu.SMEM)],
        out_specs=pl.BlockSpec((1, 2, t, t), lambda h: (h, 0, 0, 0)),
        out_shape=jax.ShapeDtypeStruct((heads, 2, t, t), F32),
        compiler_params=_params("parallel"),
        name="rel_bias_tiles",
    )(jnp.asarray(_bucket_tiles(t)), rel_bias.astype(F32))


def _flash_kernel(q_ref, k_ref, vt_ref, bias_ref, lam_ref, hn_ref, o_ref,
                  qcat_ref, sa_ref, sb_ref, m_ref, l_ref, acc_ref, *, lam_init):
    tq = q_ref.shape[2]
    tk = bias_ref.shape[2]
    cw = MXU_WIDTH
    per_branch = tq // cw
    ncol = 2 * per_branch
    per_key = tk // cw
    inner = tq // tk
    assert inner % 2 == 0, "the far-tile loop below relies on an odd number of far tiles"
    qi = pl.program_id(2)
    qt = q_ref[0]
    row = lax.broadcasted_iota(jnp.int32, qt.shape, 0)
    zero = jnp.zeros_like(qt)
    qcat_ref[:, :tq] = jnp.where(row < HEAD_DIM, qt, zero)
    qcat_ref[:, tq:] = jnp.where(row >= HEAD_DIM, qt, zero)

    m_ref[...] = jnp.full(m_ref.shape, NEG_INF, F32)
    l_ref[...] = jnp.zeros_like(l_ref)
    acc_ref[...] = jnp.zeros_like(acc_ref)

    def bias_of(kind, c):
        if kind is None:
            return None
        cq = c % per_branch
        dist = cq // per_key - kind
        if dist < 0:
            return "skip"
        if dist > 1:
            return None
        return dist, (cq % per_key) * cw

    def scores_col(kblk, dst_ref, kind, c):
        spec = bias_of(kind, c)
        if spec == "skip":
            return None
        s = jnp.dot(kblk, qcat_ref[:, c * cw:(c + 1) * cw], preferred_element_type=F32)
        if spec is not None:
            s = s + bias_ref[0, spec[0], :, spec[1]:spec[1] + cw]
        dst_ref[c] = s
        return jnp.max(s, axis=0, keepdims=True)

    def softmax_pv_col(vblk, src_ref, cm, kind, c):
        if bias_of(kind, c) == "skip":
            return
        s = src_ref[c]
        m_old = m_ref[c]
        m_new = jnp.maximum(m_old, cm)
        alpha = jnp.exp2(m_old - m_new)
        p = jnp.exp2(s - m_new)
        l_ref[c] = alpha * l_ref[c] + jnp.sum(p, axis=0, keepdims=True)
        m_ref[c] = m_new
        acc_ref[c] = alpha * acc_ref[c] + jnp.dot(vblk, p.astype(BF16), preferred_element_type=F32)

    def key_tile(j):
        return k_ref[0, pl.ds(pl.multiple_of(j * tk, tk), tk), :]

    def scores(j, dst_ref, kind):
        kblk = key_tile(j)
        return tuple(scores_col(kblk, dst_ref, kind, c) for c in range(ncol))

    def step(nxt, cur, cmax):
        j, kind, src_ref = cur
        vblk = vt_ref[0, 0, j]
        kblk = key_tile(nxt[0]) if nxt is not None else None
        cm_next = []
        for c in range(ncol):
            if nxt is not None:
                cm_next.append(scores_col(kblk, nxt[2], nxt[1], c))
            softmax_pv_col(vblk, src_ref, cmax[c], kind, c)
        return tuple(cm_next)

    def run(tiles, first_max):
        bufs = (sa_ref, sb_ref)
        cmax = first_max
        for n, (j, kind) in enumerate(tiles):
            nxt = None
            if n + 1 < len(tiles):
                nxt = tiles[n + 1] + (bufs[(n + 1) % 2],)
            cmax = step(nxt, (j, kind, bufs[n % 2]), cmax)

    first_diag = qi * inner
    diag = [(first_diag + r, r) for r in range(inner)]

    @pl.when(qi == 0)
    def _():
        run(diag, scores(0, sa_ref, 0))

    @pl.when(qi >= 1)
    def _():
        n_far = first_diag - 1

        def pair(i, cm_a):
            j = 2 * i
            cm_b = step((j + 1, None, sb_ref), (j, None, sa_ref), cm_a)
            return step((j + 2, None, sa_ref), (j + 1, None, sb_ref), cm_b)

        cm_a = lax.fori_loop(0, (n_far - 1) // 2, pair, scores(0, sa_ref, None))
        run([(n_far - 1, None), (n_far, -1)] + diag, cm_a)

    lv = lam_ref[...]
    lam = (jnp.exp(jnp.sum(lv[0:1] * lv[1:2], axis=1, keepdims=True))
           - jnp.exp(jnp.sum(lv[2:3] * lv[3:4], axis=1, keepdims=True)) + lam_init)
    for c in range(per_branch):
        o1 = acc_ref[c] * (1.0 / l_ref[c])
        o2 = acc_ref[per_branch + c] * (1.0 / l_ref[per_branch + c])
        o = o1 - lam * o2
        o = o * lax.rsqrt(jnp.mean(o * o, axis=0, keepdims=True) + NORM_EPS)
        o = o * hn_ref[...] * (1.0 - lam_init)
        o_ref[0, c * cw:(c + 1) * cw, :] = o.T.astype(BF16)


def _diff_attention(qt, k, vt, bias, lam_vecs, head_norm, *, lam_init):
    batch, d, seq = qt.shape
    heads = d // V_DIM
    t = ATTN_TILE
    tq = ATTN_QTILE
    nblk = seq // t
    ncol = 2 * tq // MXU_WIDTH
    lam_pad = jnp.zeros((8, V_DIM), F32).at[:4, :HEAD_DIM].set(lam_vecs.astype(F32))
    return pl.pallas_call(
        functools.partial(_flash_kernel, lam_init=lam_init),
        grid=(batch, heads, seq // tq),
        in_specs=[pl.BlockSpec((1, V_DIM, tq), lambda b, h, i: (b, h, i)),
                  pl.BlockSpec((1, seq, V_DIM), lambda b, h, i: (b, 0, h)),
                  pl.BlockSpec((1, 1, nblk, V_DIM, t), lambda b, h, i: (b, h, 0, 0, 0)),
                  pl.BlockSpec((1, 2, t, t), lambda b, h, i: (h, 0, 0, 0)),
                  _const_spec((8, V_DIM)),
                  _const_spec((V_DIM, 1))],
        out_specs=pl.BlockSpec((1, tq, V_DIM), lambda b, h, i: (b, i, h)),
        out_shape=jax.ShapeDtypeStruct((batch, seq, d), BF16),
        scratch_shapes=[pltpu.VMEM((V_DIM, 2 * tq), BF16),
                        pltpu.VMEM((ncol, t, MXU_WIDTH), F32),
                        pltpu.VMEM((ncol, t, MXU_WIDTH), F32),
                        pltpu.VMEM((ncol, 1, MXU_WIDTH), F32),
                        pltpu.VMEM((ncol, 1, MXU_WIDTH), F32),
                        pltpu.VMEM((ncol, V_DIM, MXU_WIDTH), F32)],
        compiler_params=_params("parallel", "parallel", "arbitrary"),
        name="diff_attention",
    )(qt, k.reshape(batch, seq, d), vt, bias, lam_pad, head_norm.astype(F32).reshape(V_DIM, 1))


def kernel(x, norm_mixer_pre, norm_mixer_post, norm_mlp_pre, norm_mlp_post, mlp_w_up, mlp_w_down,
           ssm_w_in, ssm_a_re, ssm_a_im, ssm_log_dt, ssm_b_re, ssm_b_im, ssm_c_re, ssm_c_im, ssm_d,
           ssm_w_glu, kv_norm, w_kv, attn_w_q, attn_lambda_q1, attn_lambda_k1, attn_lambda_q2,
           attn_lambda_k2, attn_head_norm, attn_w_o, rel_bias):
    batch, seq, d = x.shape
    depth = norm_mixer_pre.shape[0]
    n_ssm = ssm_w_in.shape[0]
    assert 0 < n_ssm < depth
    xs = x.reshape(batch * seq, d).astype(F32)
    q_scale = HEAD_DIM ** -0.5 * LOG2E
    w_up, w_down, w_in, w_glu, w_q, w_o = (
        w.astype(BF16) for w in (mlp_w_up, mlp_w_down, ssm_w_in, ssm_w_glu, attn_w_q, attn_w_o))

    def mixer_projections(layer):
        if layer >= depth:
            return []
        if layer < n_ssm:
            return [("colblock", norm_mixer_pre[layer], (w_in, layer), 1.0)]
        projs = []
        if layer == n_ssm:
            projs.append(("kv", kv_norm, w_kv.astype(BF16), 1.0))
        projs.append(("qT", norm_mixer_pre[layer], (w_q, layer - n_ssm), q_scale))
        return projs

    feeds = [_norm_proj(xs, norm_mixer_pre[0], w_in[0], mode="colblock", batch=batch, seq=seq)]
    bias = _bias_tiles(rel_bias, ATTN_TILE)
    k = vt = None
    for layer in range(depth):
        if layer < n_ssm:
            i = layer
            ops = _s5_param_layout(ssm_a_re[i], ssm_a_im[i], ssm_log_dt[i], ssm_b_re[i], ssm_b_im[i],
                                ssm_c_re[i], ssm_c_im[i])
            y = _s5_core(feeds[0], ops, ssm_d[i].astype(F32), batch=batch, seq=seq)
            w_out = (w_glu, i)
        else:
            if layer == n_ssm:
                k, vt = feeds[0], feeds[1]
            j = layer - n_ssm
            lam_init = 0.8 - 0.6 * math.exp(-0.3 * layer)
            lam_vecs = jnp.stack([attn_lambda_q1[j], attn_lambda_k1[j],
                                  attn_lambda_q2[j], attn_lambda_k2[j]])
            y = _diff_attention(feeds[-1], k, vt, bias, lam_vecs, attn_head_norm[j], lam_init=lam_init)
            y = y.reshape(batch * seq, d)
            w_out = (w_o, j)
        xs, feeds = _layer_tail(y, w_out, xs, norm_mixer_post[layer], norm_mlp_pre[layer],
                                (w_up, layer), (w_down, layer),
                                norm_mlp_post[layer], mixer_projections(layer + 1),
                                glu=layer < n_ssm, chunked=layer < n_ssm, batch=batch, seq=seq)
    return xs.reshape(batch, seq, d).astype(x.dtype)
```

```python
import functools
import math

import numpy as np
import jax
import jax.numpy as jnp
from jax import lax
from jax.experimental import pallas as pl
from jax.experimental.pallas import tpu as pltpu

F32 = jnp.float32
BF16 = jnp.bfloat16

SSM_GROUP = 16
SSM_STATE = 64
HEAD_DIM = 64
V_DIM = 2 * HEAD_DIM
NUM_BUCKETS = 32
MAX_DISTANCE = 128
NORM_EPS = 1e-6
NEG_INF = -1e30
LOG2E = math.log2(math.e)

MXU_WIDTH = 256
LANES = 128
SUBLANES = 8
ROW_TILE = 512
SCAN_CHUNK = 8
SCAN_ROWS = 512
ATTN_TILE = 512
ATTN_QTILE = 2048
FF_CHUNK = 1024
TAIL_SPLIT = 2
VMEM_LIMIT = 56 * 1024 * 1024


def _params(*sem):
    return pltpu.CompilerParams(dimension_semantics=sem, vmem_limit_bytes=VMEM_LIMIT)


def _rms(xf, gain):
    return xf * lax.rsqrt(jnp.mean(xf * xf, axis=-1, keepdims=True) + NORM_EPS) * gain


def _const_spec(shape):
    zeros = (0,) * len(shape)
    return pl.BlockSpec(shape, lambda *_: zeros, pipeline_mode=pl.Buffered(1))


def _project(x, g_ref, w_ref, out_refs, slab_ref, mode, scale, r0=0):
    hn = _rms(x, g_ref[...]).astype(BF16)
    acc = jnp.dot(hn, w_ref[...], preferred_element_type=F32)
    rows = acc.shape[0]
    if mode == "colblock":
        (o_ref,) = out_refs
        L = SCAN_CHUNK
        for sl in range(slab_ref.shape[0]):
            slab_ref[sl, r0:r0 + rows, :] = acc[:, sl * LANES:(sl + 1) * LANES]
        for cb in range(o_ref.shape[0]):
            for t in range(L):
                for hf in range(MXU_WIDTH // LANES):
                    lo = t * MXU_WIDTH + hf * LANES
                    piece = slab_ref[cb * (MXU_WIDTH // LANES) + hf, pl.ds(r0 + t, rows // L, stride=L), :]
                    o_ref[cb, r0 // L:(r0 + rows) // L, lo:lo + LANES] = piece.astype(BF16)
    elif mode == "qT":
        (o_ref,) = out_refs
        o_ref[0, :, r0:r0 + rows] = (acc * scale).T.astype(BF16)
    else:
        k_ref, vt_ref = out_refs
        d = k_ref.shape[1]
        k_ref[r0:r0 + rows, :] = acc[:, :d].astype(BF16)
        vt = acc[:, d:].T.astype(BF16)
        tk = vt_ref.shape[4]
        step = min(rows, tk)
        for n in range(rows // step):
            at = r0 + n * step
            vt_ref[0, :, at // tk, :, at % tk:at % tk + step] = (
                vt[:, n * step:(n + 1) * step].reshape(d // V_DIM, V_DIM, step))


def _proj_outputs(mode, n, d, nout, batch, seq):
    tm = ROW_TILE
    per_b = seq // tm
    if mode == "colblock":
        ncb = nout // MXU_WIDTH
        L = SCAN_CHUNK
        return ([jax.ShapeDtypeStruct((ncb, n // L, L * MXU_WIDTH), BF16)],
                [pl.BlockSpec((ncb, tm // L, L * MXU_WIDTH), lambda i: (0, i, 0))])
    if mode == "qT":
        return ([jax.ShapeDtypeStruct((batch, nout, seq), BF16)],
                [pl.BlockSpec((1, nout, tm), lambda i: (i // per_b, 0, i % per_b))])
    heads = d // V_DIM
    tk = ATTN_TILE
    assert mode == "kv" and tm % tk == 0
    return ([jax.ShapeDtypeStruct((n, d), BF16),
             jax.ShapeDtypeStruct((batch, heads, seq // tk, V_DIM, tk), BF16)],
            [pl.BlockSpec((tm, d), lambda i: (i, 0)),
             pl.BlockSpec((1, heads, tm // tk, V_DIM, tk), lambda i: (i // per_b, 0, i % per_b, 0, 0))])


def _slab_scratch(d):
    return [pltpu.VMEM((d // LANES, ROW_TILE, LANES), F32), pltpu.VMEM((ROW_TILE, d), BF16)]


def _proj_kernel(x_ref, g_ref, w_ref, *refs, mode, scale):
    slab_ref = refs[-2] if mode == "colblock" else None
    nout = 2 if mode == "kv" else 1
    _project(x_ref[...], g_ref, w_ref, refs[:nout], slab_ref, mode, scale)


def _norm_proj(x, gain, w, *, mode, batch, seq, scale=1.0):
    n, d = x.shape
    nout = w.shape[1]
    tm = ROW_TILE
    out_shape, out_specs = _proj_outputs(mode, n, d, nout, batch, seq)
    out = pl.pallas_call(
        functools.partial(_proj_kernel, mode=mode, scale=scale),
        grid=(n // tm,),
        in_specs=[pl.BlockSpec((tm, d), lambda i: (i, 0)),
                  _const_spec((1, d)),
                  _const_spec((d, nout))],
        out_specs=out_specs,
        out_shape=out_shape,
        scratch_shapes=_slab_scratch(nout) if mode == "colblock" else [],
        compiler_params=_params("parallel"),
        name="norm_proj_" + mode,
    )(x, gain.reshape(1, d), w)
    return out[0] if len(out) == 1 else out


def _chunk_rows_to_tokens(y_ref, slab_ref, tok_ref):
    L = SCAN_CHUNK
    nchunk = y_ref.shape[1]
    for cb in range(y_ref.shape[0]):
        for t in range(L):
            for hf in range(MXU_WIDTH // LANES):
                lo = t * MXU_WIDTH + hf * LANES
                slab_ref[cb * (MXU_WIDTH // LANES) + hf, pl.ds(t, nchunk, stride=L), :] = (
                    y_ref[cb, :, lo:lo + LANES].astype(F32))
    for sl in range(slab_ref.shape[0]):
        tok_ref[:, sl * LANES:(sl + 1) * LANES] = slab_ref[sl].astype(BF16)


def _tail_kernel(*refs, glu, chunked, projs, use_slab):
    n_in = 8 + 2 * len(projs)
    y_ref, wo_ref, x_ref, gpost_ref, gpre_ref, wu_ref, wd_ref, gmlp_ref = refs[:8]
    proj_in = refs[8:n_in]
    n_out = 1 + sum(2 if mode == "kv" else 1 for mode, _ in projs)
    o_ref = refs[n_in]
    proj_out = refs[n_in + 1:n_in + n_out]
    slab_ref, tok_ref = refs[n_in + n_out:] if use_slab else (None, None)
    tm, d = o_ref.shape
    rows = tm // TAIL_SPLIT
    if chunked:
        _chunk_rows_to_tokens(y_ref, slab_ref, tok_ref)
    y_tok = tok_ref if chunked else y_ref

    def stages(r0):
        acc = jnp.dot(y_tok[r0:r0 + rows, :], wo_ref[...], preferred_element_type=F32)
        if glu:
            acc = acc[:, :d] * jax.nn.sigmoid(acc[:, d:])
        yield
        x = x_ref[r0:r0 + rows, :] + _rms(acc, gpost_ref[...])
        hn = _rms(x, gpre_ref[...]).astype(BF16)
        yield
        acc = None
        for c in range(wu_ref.shape[1] // FF_CHUNK):
            a = jnp.dot(hn, wu_ref[:, c * FF_CHUNK:(c + 1) * FF_CHUNK], preferred_element_type=F32)
            a = jnp.maximum(a, 0.0)
            a = (a * a).astype(BF16)
            part = jnp.dot(a, wd_ref[c * FF_CHUNK:(c + 1) * FF_CHUNK, :], preferred_element_type=F32)
            acc = part if acc is None else acc + part
            yield
        x = x + _rms(acc, gmlp_ref[...])
        o_ref[r0:r0 + rows, :] = x
        k = 0
        for i, (mode, scale) in enumerate(projs):
            yield
            cnt = 2 if mode == "kv" else 1
            _project(x, proj_in[2 * i], proj_in[2 * i + 1], proj_out[k:k + cnt], slab_ref, mode, scale, r0)
            k += cnt

    active = [stages(g * rows) for g in range(TAIL_SPLIT)]
    while active:
        for gen in list(active):
            if next(gen, "done") == "done":
                active.remove(gen)


def _layer_weight(w):
    if isinstance(w, tuple):
        stack, layer = w
        return stack, pl.BlockSpec((None,) + stack.shape[1:], lambda *_: (layer, 0, 0),
                                   pipeline_mode=pl.Buffered(1))
    return w, _const_spec(w.shape)


def _layer_tail(y, w_out, x, g_post, g_pre, w_up, w_down, g_mlp, projs, *, glu, chunked, batch, seq):
    n, d = x.shape
    tm = ROW_TILE
    row = pl.BlockSpec((tm, d), lambda i: (i, 0))
    if chunked:
        ncb, _, width = y.shape
        y_spec = pl.BlockSpec((ncb, tm // SCAN_CHUNK, width), lambda i: (0, i, 0))
    else:
        y_spec = pl.BlockSpec((tm, y.shape[1]), lambda i: (i, 0))
    (w_out, wo_spec), (w_up, wu_spec), (w_down, wd_spec) = map(_layer_weight, (w_out, w_up, w_down))
    in_specs = [y_spec, wo_spec, row, _const_spec((1, d)), _const_spec((1, d)),
                wu_spec, wd_spec, _const_spec((1, d))]
    args = [y, w_out, x, g_post.reshape(1, d), g_pre.reshape(1, d), w_up, w_down, g_mlp.reshape(1, d)]
    out_shape = [jax.ShapeDtypeStruct((n, d), F32)]
    out_specs = [row]
    for mode, gain, w, _ in projs:
        w, w_spec = _layer_weight(w)
        in_specs += [_const_spec((1, d)), w_spec]
        args += [gain.reshape(1, d), w]
        shapes, specs = _proj_outputs(mode, n, d, w.shape[-1], batch, seq)
        out_shape += shapes
        out_specs += specs
    use_slab = chunked or any(mode == "colblock" for mode, *_ in projs)
    out = pl.pallas_call(
        functools.partial(_tail_kernel, glu=glu, chunked=chunked,
                          projs=tuple((mode, scale) for mode, _, _, scale in projs), use_slab=use_slab),
        grid=(n // tm,),
        in_specs=in_specs,
        out_specs=out_specs,
        out_shape=out_shape,
        scratch_shapes=_slab_scratch(d) if use_slab else [],
        compiler_params=_params("parallel"),
        name="layer_tail",
    )(*args)
    return out[0], list(out[1:])


def _cmul(a, b):
    return a[0] * b[0] - a[1] * b[1], a[0] * b[1] + a[1] * b[0]


def _zoh(lr, li, log_dt):
    dt = jnp.exp(log_dt)
    mag = jnp.exp(lr * dt)
    ab = (mag * jnp.cos(li * dt), mag * jnp.sin(li * dt))
    den = lr * lr + li * li
    coef = (((ab[0] - 1.0) * lr + ab[1] * li) / den, (ab[1] * lr - (ab[0] - 1.0) * li) / den)
    return ab, coef


def _s5_kernel(u_ref, lamc_ref, lamr_ref, bt_ref, ct_ref, d_ref, o_ref,
               w_ref, k_ref, p_ref, apr_ref, api_ref, v_ref, sp_ref, sr_ref, si_ref):
    cw = MXU_WIDTH
    ns = v_ref.shape[1] // 2
    L = u_ref.shape[2] // cw
    grp_shift = SSM_GROUP.bit_length() - 1
    st_shift = SSM_STATE.bit_length() - 1

    @pl.when(jnp.logical_and(pl.program_id(1) == 0, pl.program_id(2) == 0))
    def _():
        def iota(shape, dim):
            return lax.broadcasted_iota(jnp.int32, shape, dim)

        def nt_dot(a, b):
            return lax.dot_general(a, b, (((1,), (1,)), ((), ())), precision=lax.Precision.HIGHEST,
                                   preferred_element_type=F32)

        wmask = (iota((cw, ns), 0) >> grp_shift) == (iota((cw, ns), 1) >> st_shift)
        pmask = (iota((ns, cw), 0) >> st_shift) == (iota((ns, cw), 1) >> grp_shift)
        kmask = (iota((cw, cw), 0) >> grp_shift) == (iota((cw, cw), 1) >> grp_shift)
        ab, coef = _zoh(lamc_ref[0, 0], lamc_ref[0, 1], lamc_ref[0, 2])
        bb = _cmul(coef, (bt_ref[0, 0], bt_ref[0, 1]))
        ct = (ct_ref[0, 0], ct_ref[0, 1])
        pw = [(jnp.ones_like(ab[0]), jnp.zeros_like(ab[0]))]
        for _ in range(L):
            pw.append(_cmul(pw[-1], ab))
        for t in range(L):
            w = _cmul(pw[L - 1 - t], bb)
            for h in range(2):
                tiled = jnp.concatenate([w[h]] * (ns // LANES), axis=1)
                w_ref[t * cw:(t + 1) * cw, h * ns:(h + 1) * ns] = (
                    jnp.where(wmask, tiled, 0.0).astype(BF16))
            km = 0.5 * (nt_dot(w[0], ct[0]) - nt_dot(w[1], ct[1]))
            k_ref[t * cw:(t + 1) * cw, :] = jnp.where(kmask, km, 0.0).astype(BF16)
            q = _cmul(ct, pw[t + 1])
            for h, qh in enumerate((q[0], -q[1])):
                piece = qh.T[:SSM_STATE, :]
                tiled = jnp.concatenate([piece] * (ns // SSM_STATE), axis=0)
                p_ref[t, h * ns:(h + 1) * ns, :] = jnp.where(pmask, tiled, 0.0).astype(BF16)
        a, _ = _zoh(lamr_ref[0, 0:1], lamr_ref[0, 1:2], lamr_ref[0, 2:3])
        al = a
        for _ in range(L - 1):
            al = _cmul(al, a)
        q = al
        for r in range(SUBLANES):
            apr_ref[r:r + 1, :] = q[0]
            api_ref[r:r + 1, :] = q[1]
            q = _cmul(q, al)

    @pl.when(pl.program_id(2) == 0)
    def _():
        sr_ref[...] = jnp.zeros_like(sr_ref)
        si_ref[...] = jnp.zeros_like(si_ref)

    rows = v_ref.shape[0]
    half = rows // 2
    apr = apr_ref[...]
    api = api_ref[...]
    row = lax.broadcasted_iota(jnp.int32, (SUBLANES, ns), 0)

    def shift(z, k, fill):
        return jnp.where(row >= k, pltpu.roll(z, k, 0), fill)

    def scan_block(i, carry):
        sr, si = carry
        lo, hi_row = SUBLANES * i, SUBLANES * (i + 1)
        zr = v_ref[lo:hi_row, 0:ns]
        zi = v_ref[lo:hi_row, ns:2 * ns]
        k = 1
        while k < SUBLANES:
            ar, ai = apr[k - 1:k], api[k - 1:k]
            hr, hi = shift(zr, k, 0.0), shift(zi, k, 0.0)
            zr, zi = zr + ar * hr - ai * hi, zi + ar * hi + ai * hr
            k *= 2
        fr = zr + apr * sr - api * si
        fi = zi + apr * si + api * sr
        v_ref[lo:hi_row, 0:ns] = shift(fr, 1, sr)
        v_ref[lo:hi_row, ns:2 * ns] = shift(fi, 1, si)
        return fr[SUBLANES - 1:], fi[SUBLANES - 1:]

    def outputs(lo, t):
        xs = u_ref[0, lo:lo + half, :]
        acc = jnp.dot(xs[:, :(t + 1) * cw], k_ref[(L - 1 - t) * cw:, :], preferred_element_type=F32)
        acc = acc + jnp.dot(sp_ref[lo:lo + half, :], p_ref[t], preferred_element_type=F32)
        y = acc + d_ref[0] * xs[:, t * cw:(t + 1) * cw].astype(F32)
        o_ref[0, lo:lo + half, t * cw:(t + 1) * cw] = jax.nn.gelu(y).astype(BF16)

    nblk_half = half // SUBLANES
    pieces = 4
    ncols = 2 * ns // pieces
    v_ref[:half, :] = jnp.dot(u_ref[0, :half, :], w_ref[...], preferred_element_type=F32)
    carry = (sr_ref[...], si_ref[...])
    for n in range(pieces):
        v_ref[half:, n * ncols:(n + 1) * ncols] = jnp.dot(
            u_ref[0, half:, :], w_ref[:, n * ncols:(n + 1) * ncols], preferred_element_type=F32)
        for i in range(n * nblk_half // pieces, (n + 1) * nblk_half // pieces):
            carry = scan_block(i, carry)
    sp_ref[:half, :] = v_ref[:half, :].astype(BF16)
    for t in range(L):
        outputs(0, t)
        for i in range(t * nblk_half // L, (t + 1) * nblk_half // L):
            carry = scan_block(nblk_half + i, carry)
    sr_ref[...], si_ref[...] = carry
    sp_ref[half:, :] = v_ref[half:, :].astype(BF16)
    for t in range(L):
        outputs(half, t)


def _s5_core(u_v, ops, d_skip, *, batch, seq):
    lamc, lamr, bt, ct = ops
    L = SCAN_CHUNK
    cw = MXU_WIDTH
    ncb = u_v.shape[0]
    rows_per_b = seq // L
    halves = rows_per_b // SCAN_ROWS
    ns = lamr.shape[2]
    blk = pl.BlockSpec((1, SCAN_ROWS, L * cw), lambda cb, b, h: (cb, b * halves + h, 0))

    def per_block(a):
        zeros = (0,) * (a.ndim - 1)
        return pl.BlockSpec((1,) + a.shape[1:], lambda cb, b, h: (cb,) + zeros)

    return pl.pallas_call(
        _s5_kernel,
        grid=(ncb, batch, halves),
        in_specs=[blk, per_block(lamc), per_block(lamr), per_block(bt), per_block(ct),
                  pl.BlockSpec((1, 1, cw), lambda cb, b, h: (cb, 0, 0))],
        out_specs=blk,
        out_shape=jax.ShapeDtypeStruct(u_v.shape, BF16),
        scratch_shapes=[pltpu.VMEM((L * cw, 2 * ns), BF16),
                        pltpu.VMEM((L * cw, cw), BF16),
                        pltpu.VMEM((L, 2 * ns, cw), BF16),
                        pltpu.VMEM((SUBLANES, ns), F32),
                        pltpu.VMEM((SUBLANES, ns), F32),
                        pltpu.VMEM((SCAN_ROWS, 2 * ns), F32),
                        pltpu.VMEM((SCAN_ROWS, 2 * ns), BF16),
                        pltpu.VMEM((1, ns), F32),
                        pltpu.VMEM((1, ns), F32)],
        compiler_params=_params("arbitrary", "arbitrary", "arbitrary"),
        name="s5_core",
    )(u_v, lamc, lamr, bt, ct, d_skip.reshape(ncb, 1, cw))


def _s5_param_layout(a_re, a_im, log_dt, b_re, b_im, c_re, c_im):
    groups, p = a_re.shape
    gl = MXU_WIDTH // SSM_GROUP
    ncb = groups // gl
    lam = jnp.stack([a_re, a_im, jnp.broadcast_to(log_dt[:, None], (groups, p))]).astype(F32)
    lamc = jnp.broadcast_to(lam[:, :, None, :], (3, groups, SSM_GROUP, p)).reshape(3, ncb, MXU_WIDTH, p)
    lamr = jnp.pad(lam.reshape(3, ncb, gl * p), ((0, SUBLANES - 3), (0, 0), (0, 0)))
    bt = jnp.stack([b_re, b_im]).astype(F32).transpose(0, 1, 3, 2).reshape(2, ncb, MXU_WIDTH, p)
    ct = jnp.stack([c_re, c_im]).astype(F32).reshape(2, ncb, MXU_WIDTH, p)

    def twice(a):
        return jnp.concatenate([a] * (LANES // p), axis=-1).transpose(1, 0, 2, 3)

    return twice(lamc), lamr.transpose(1, 0, 2), twice(bt), twice(ct)


def _bucket_tiles(t):
    i = np.arange(t)[:, None]
    j = np.arange(t)[None, :]
    tiles = []
    for off in (0, t):
        rel = j + off - i
        n = np.maximum(rel, 0)
        max_exact = NUM_BUCKETS // 2
        large = max_exact + (np.log(np.maximum(n, max_exact).astype(np.float32) / max_exact)
                             / math.log(MAX_DISTANCE / max_exact)
                             * (NUM_BUCKETS - max_exact)).astype(np.int32)
        large = np.minimum(large, NUM_BUCKETS - 1)
        b = np.where(n < max_exact, n, large)
        tiles.append(np.where(rel >= 0, b, -1).astype(np.int32))
    return np.stack(tiles)


def _bias_kernel(bkt_ref, tab_ref, o_ref):
    h = pl.program_id(0)
    far = tab_ref[NUM_BUCKETS - 1, h]
    for typ in range(2):
        b = bkt_ref[typ]
        acc = jnp.zeros(b.shape, F32)
        for i in range(NUM_BUCKETS - 1):
            acc = jnp.where(b == i, (tab_ref[i, h] - far) * LOG2E, acc)
        o_ref[0, typ] = jnp.where(b < 0, NEG_INF, acc)


def _bias_tiles(rel_bias, t):
    heads = rel_bias.shape[1]
    return pl.pallas_call(
        _bias_kernel,
        grid=(heads,),
        in_specs=[_const_spec((2, t, t)), pl.BlockSpec(memory_space=pltpu.SMEM)],
        out_specs=pl.BlockSpec((1, 2, t, t), lambda h: (h, 0, 0, 0)),
        out_shape=jax.ShapeDtypeStruct((heads, 2, t, t), F32),
        compiler_params=_params("parallel"),
        name="rel_bias_tiles",
    )(jnp.asarray(_bucket_tiles(t)), rel_bias.astype(F32))


def _flash_kernel(q_ref, k_ref, vt_ref, bias_ref, lam_ref, hn_ref, o_ref,
                  qcat_ref, sa_ref, sb_ref, m_ref, l_ref, acc_ref, *, lam_init):
    tq = q_ref.shape[2]
    tk = bias_ref.shape[2]
    cw = MXU_WIDTH
    per_branch = tq // cw
    ncol = 2 * per_branch
    per_key = tk // cw
    inner = tq // tk
    assert inner % 2 == 0, "the far-tile loop below relies on an odd number of far tiles"
    qi = pl.program_id(2)
    qt = q_ref[0]
    row = lax.broadcasted_iota(jnp.int32, qt.shape, 0)
    zero = jnp.zeros_like(qt)
    qcat_ref[:, :tq] = jnp.where(row < HEAD_DIM, qt, zero)
    qcat_ref[:, tq:] = jnp.where(row >= HEAD_DIM, qt, zero)

    m_ref[...] = jnp.full(m_ref.shape, NEG_INF, F32)
    l_ref[...] = jnp.zeros_like(l_ref)
    acc_ref[...] = jnp.zeros_like(acc_ref)

    def bias_of(kind, c):
        if kind is None:
            return None
        cq = c % per_branch
        dist = cq // per_key - kind
        if dist < 0:
            return "skip"
        if dist > 1:
            return None
        return dist, (cq % per_key) * cw

    def scores_col(kblk, dst_ref, kind, c):
        spec = bias_of(kind, c)
        if spec == "skip":
            return None
        s = jnp.dot(kblk, qcat_ref[:, c * cw:(c + 1) * cw], preferred_element_type=F32)
        if spec is not None:
            s = s + bias_ref[0, spec[0], :, spec[1]:spec[1] + cw]
        dst_ref[c] = s
        return jnp.max(s, axis=0, keepdims=True)

    def softmax_pv_col(vblk, src_ref, cm, kind, c):
        if bias_of(kind, c) == "skip":
            return
        s = src_ref[c]
        m_old = m_ref[c]
        m_new = jnp.maximum(m_old, cm)
        alpha = jnp.exp2(m_old - m_new)
        p = jnp.exp2(s - m_new)
        l_ref[c] = alpha * l_ref[c] + jnp.sum(p, axis=0, keepdims=True)
        m_ref[c] = m_new
        acc_ref[c] = alpha * acc_ref[c] + jnp.dot(vblk, p.astype(BF16), preferred_element_type=F32)

    def key_tile(j):
        return k_ref[0, pl.ds(pl.multiple_of(j * tk, tk), tk), :]

    def scores(j, dst_ref, kind):
        kblk = key_tile(j)
        return tuple(scores_col(kblk, dst_ref, kind, c) for c in range(ncol))

    def step(nxt, cur, cmax):
        j, kind, src_ref = cur
        vblk = vt_ref[0, 0, j]
        kblk = key_tile(nxt[0]) if nxt is not None else None
        cm_next = []
        for c in range(ncol):
            if nxt is not None:
                cm_next.append(scores_col(kblk, nxt[2], nxt[1], c))
            softmax_pv_col(vblk, src_ref, cmax[c], kind, c)
        return tuple(cm_next)

    def run(tiles, first_max):
        bufs = (sa_ref, sb_ref)
        cmax = first_max
        for n, (j, kind) in enumerate(tiles):
            nxt = None
            if n + 1 < len(tiles):
                nxt = tiles[n + 1] + (bufs[(n + 1) % 2],)
            cmax = step(nxt, (j, kind, bufs[n % 2]), cmax)

    first_diag = qi * inner
    diag = [(first_diag + r, r) for r in range(inner)]

    @pl.when(qi == 0)
    def _():
        run(diag, scores(0, sa_ref, 0))

    @pl.when(qi >= 1)
    def _():
        n_far = first_diag - 1

        def pair(i, cm_a):
            j = 2 * i
            cm_b = step((j + 1, None, sb_ref), (j, None, sa_ref), cm_a)
            return step((j + 2, None, sa_ref), (j + 1, None, sb_ref), cm_b)

        cm_a = lax.fori_loop(0, (n_far - 1) // 2, pair, scores(0, sa_ref, None))
        run([(n_far - 1, None), (n_far, -1)] + diag, cm_a)

    lv = lam_ref[...]
    lam = (jnp.exp(jnp.sum(lv[0:1] * lv[1:2], axis=1, keepdims=True))
           - jnp.exp(jnp.sum(lv[2:3] * lv[3:4], axis=1, keepdims=True)) + lam_init)
    for c in range(per_branch):
        o1 = acc_ref[c] * (1.0 / l_ref[c])
        o2 = acc_ref[per_branch + c] * (1.0 / l_ref[per_branch + c])
        o = o1 - lam * o2
        o = o * lax.rsqrt(jnp.mean(o * o, axis=0, keepdims=True) + NORM_EPS)
        o = o * hn_ref[...] * (1.0 - lam_init)
        o_ref[0, c * cw:(c + 1) * cw, :] = o.T.astype(BF16)


def _diff_attention(qt, k, vt, bias, lam_vecs, head_norm, *, lam_init):
    batch, d, seq = qt.shape
    heads = d // V_DIM
    t = ATTN_TILE
    tq = ATTN_QTILE
    nblk = seq // t
    ncol = 2 * tq // MXU_WIDTH
    lam_pad = jnp.zeros((SUBLANES, V_DIM), F32).at[:4, :HEAD_DIM].set(lam_vecs.astype(F32))
    return pl.pallas_call(
        functools.partial(_flash_kernel, lam_init=lam_init),
        grid=(batch, heads, seq // tq),
        in_specs=[pl.BlockSpec((1, V_DIM, tq), lambda b, h, i: (b, h, i)),
                  pl.BlockSpec((1, seq, V_DIM), lambda b, h, i: (b, 0, h)),
                  pl.BlockSpec((1, 1, nblk, V_DIM, t), lambda b, h, i: (b, h, 0, 0, 0)),
                  pl.BlockSpec((1, 2, t, t), lambda b, h, i: (h, 0, 0, 0)),
                  _const_spec((SUBLANES, V_DIM)),
                  _const_spec((V_DIM, 1))],
        out_specs=pl.BlockSpec((1, tq, V_DIM), lambda b, h, i: (b, i, h)),
        out_shape=jax.ShapeDtypeStruct((batch, seq, d), BF16),
        scratch_shapes=[pltpu.VMEM((V_DIM, 2 * tq), BF16),
                        pltpu.VMEM((ncol, t, MXU_WIDTH), F32),
                        pltpu.VMEM((ncol, t, MXU_WIDTH), F32),
                        pltpu.VMEM((ncol, 1, MXU_WIDTH), F32),
                        pltpu.VMEM((ncol, 1, MXU_WIDTH), F32),
                        pltpu.VMEM((ncol, V_DIM, MXU_WIDTH), F32)],
        compiler_params=_params("parallel", "parallel", "arbitrary"),
        name="diff_attention",
    )(qt, k.reshape(batch, seq, d), vt, bias, lam_pad, head_norm.astype(F32).reshape(V_DIM, 1))


def kernel(x, norm_mixer_pre, norm_mixer_post, norm_mlp_pre, norm_mlp_post, mlp_w_up, mlp_w_down,
           ssm_w_in, ssm_a_re, ssm_a_im, ssm_log_dt, ssm_b_re, ssm_b_im, ssm_c_re, ssm_c_im, ssm_d,
           ssm_w_glu, kv_norm, w_kv, attn_w_q, attn_lambda_q1, attn_lambda_k1, attn_lambda_q2,
           attn_lambda_k2, attn_head_norm, attn_w_o, rel_bias):
    batch, seq, d = x.shape
    depth = norm_mixer_pre.shape[0]
    n_ssm = ssm_w_in.shape[0]
    assert 0 < n_ssm < depth
    xs = x.reshape(batch * seq, d).astype(F32)
    q_scale = HEAD_DIM ** -0.5 * LOG2E
    w_up, w_down, w_in, w_glu, w_q, w_o = (
        w.astype(BF16) for w in (mlp_w_up, mlp_w_down, ssm_w_in, ssm_w_glu, attn_w_q, attn_w_o))

    def mixer_projections(layer):
        if layer >= depth:
            return []
        if layer < n_ssm:
            return [("colblock", norm_mixer_pre[layer], (w_in, layer), 1.0)]
        projs = []
        if layer == n_ssm:
            projs.append(("kv", kv_norm, w_kv.astype(BF16), 1.0))
        projs.append(("qT", norm_mixer_pre[layer], (w_q, layer - n_ssm), q_scale))
        return projs

    feeds = [_norm_proj(xs, norm_mixer_pre[0], w_in[0], mode="colblock", batch=batch, seq=seq)]
    bias = _bias_tiles(rel_bias, ATTN_TILE)
    k = vt = None
    for layer in range(depth):
        if layer < n_ssm:
            i = layer
            ops = _s5_param_layout(ssm_a_re[i], ssm_a_im[i], ssm_log_dt[i], ssm_b_re[i], ssm_b_im[i],
                                ssm_c_re[i], ssm_c_im[i])
            y = _s5_core(feeds[0], ops, ssm_d[i].astype(F32), batch=batch, seq=seq)
            w_out = (w_glu, i)
        else:
            if layer == n_ssm:
                k, vt = feeds[0], feeds[1]
            j = layer - n_ssm
            lam_init = 0.8 - 0.6 * math.exp(-0.3 * layer)
            lam_vecs = jnp.stack([attn_lambda_q1[j], attn_lambda_k1[j],
                                  attn_lambda_q2[j], attn_lambda_k2[j]])
            y = _diff_attention(feeds[-1], k, vt, bias, lam_vecs, attn_head_norm[j], lam_init=lam_init)
            y = y.reshape(batch * seq, d)
            w_out = (w_o, j)
        xs, feeds = _layer_tail(y, w_out, xs, norm_mixer_post[layer], norm_mlp_pre[layer],
                                (w_up, layer), (w_down, layer),
                                norm_mlp_post[layer], mixer_projections(layer + 1),
                                glu=layer < n_ssm, chunked=layer < n_ssm, batch=batch, seq=seq)
    return xs.reshape(batch, seq, d).astype(x.dtype)
```

```python
import functools
import math

import numpy as np
import jax
import jax.numpy as jnp
from jax import lax
from jax.experimental import pallas as pl
from jax.experimental.pallas import tpu as pltpu

F32 = jnp.float32
BF16 = jnp.bfloat16

SSM_GROUP = 16
SSM_STATE = 64
HEAD_DIM = 64
V_DIM = 2 * HEAD_DIM
NUM_BUCKETS = 32
MAX_DISTANCE = 128
NORM_EPS = 1e-6
NEG_INF = -1e30
LOG2E = math.log2(math.e)

MXU_WIDTH = 256
LANES = 128
SUBLANES = 8
ROW_TILE = 512
SCAN_CHUNK = 8
SCAN_ROWS = 512
ATTN_TILE = 512
ATTN_QTILE = 2048
FF_CHUNK = 1024
TAIL_SPLIT = 2
VMEM_LIMIT = 56 * 1024 * 1024


def _params(*sem):
    return pltpu.CompilerParams(dimension_semantics=sem, vmem_limit_bytes=VMEM_LIMIT)


def _rms(xf, gain):
    return xf * lax.rsqrt(jnp.mean(xf * xf, axis=-1, keepdims=True) + NORM_EPS) * gain


def _const_spec(shape):
    zeros = (0,) * len(shape)
    return pl.BlockSpec(shape, lambda *_: zeros, pipeline_mode=pl.Buffered(1))


def _project(x, g_ref, w_ref, out_refs, slab_ref, mode, scale, r0=0):
    hn = _rms(x, g_ref[...]).astype(BF16)
    acc = jnp.dot(hn, w_ref[...], preferred_element_type=F32)
    rows = acc.shape[0]
    if mode == "colblock":
        (o_ref,) = out_refs
        L = SCAN_CHUNK
        for sl in range(slab_ref.shape[0]):
            slab_ref[sl, r0:r0 + rows, :] = acc[:, sl * LANES:(sl + 1) * LANES]
        for cb in range(o_ref.shape[0]):
            for t in range(L):
                for hf in range(MXU_WIDTH // LANES):
                    lo = t * MXU_WIDTH + hf * LANES
                    piece = slab_ref[cb * (MXU_WIDTH // LANES) + hf, pl.ds(r0 + t, rows // L, stride=L), :]
                    o_ref[cb, r0 // L:(r0 + rows) // L, lo:lo + LANES] = piece.astype(BF16)
    elif mode == "qT":
        (o_ref,) = out_refs
        o_ref[0, 0, :, r0:r0 + rows] = (acc * scale).T.astype(BF16)
    else:
        k_ref, vt_ref = out_refs
        d = k_ref.shape[1]
        k_ref[r0:r0 + rows, :] = acc[:, :d].astype(BF16)
        vt = acc[:, d:].T.astype(BF16)
        tk = vt_ref.shape[4]
        step = min(rows, tk)
        for n in range(rows // step):
            at = r0 + n * step
            vt_ref[0, :, at // tk, :, at % tk:at % tk + step] = (
                vt[:, n * step:(n + 1) * step].reshape(d // V_DIM, V_DIM, step))


def _proj_outputs(mode, n, d, nout, batch, seq):
    tm = ROW_TILE
    per_b = seq // tm
    if mode == "colblock":
        ncb = nout // MXU_WIDTH
        L = SCAN_CHUNK
        return ([jax.ShapeDtypeStruct((ncb, n // L, L * MXU_WIDTH), BF16)],
                [pl.BlockSpec((ncb, tm // L, L * MXU_WIDTH), lambda i: (0, i, 0))])
    if mode == "qT":
        tq = ATTN_QTILE
        per_q = tq // tm
        return ([jax.ShapeDtypeStruct((batch, seq // tq, nout, tq), BF16)],
                [pl.BlockSpec((1, 1, nout, tm),
                              lambda i: (i // per_b, (i % per_b) // per_q, 0, (i % per_b) % per_q))])
    heads = d // V_DIM
    tk = ATTN_TILE
    assert mode == "kv" and tm % tk == 0
    return ([jax.ShapeDtypeStruct((n, d), BF16),
             jax.ShapeDtypeStruct((batch, heads, seq // tk, V_DIM, tk), BF16)],
            [pl.BlockSpec((tm, d), lambda i: (i, 0)),
             pl.BlockSpec((1, heads, tm // tk, V_DIM, tk), lambda i: (i // per_b, 0, i % per_b, 0, 0))])


def _slab_scratch(d):
    return [pltpu.VMEM((d // LANES, ROW_TILE, LANES), F32), pltpu.VMEM((ROW_TILE, d), BF16)]


def _proj_kernel(x_ref, g_ref, w_ref, *refs, mode, scale):
    slab_ref = refs[-2] if mode == "colblock" else None
    nout = 2 if mode == "kv" else 1
    _project(x_ref[...], g_ref, w_ref, refs[:nout], slab_ref, mode, scale)


def _norm_proj(x, gain, w, *, mode, batch, seq, scale=1.0):
    n, d = x.shape
    nout = w.shape[1]
    tm = ROW_TILE
    out_shape, out_specs = _proj_outputs(mode, n, d, nout, batch, seq)
    out = pl.pallas_call(
        functools.partial(_proj_kernel, mode=mode, scale=scale),
        grid=(n // tm,),
        in_specs=[pl.BlockSpec((tm, d), lambda i: (i, 0)),
                  _const_spec((1, d)),
                  _const_spec((d, nout))],
        out_specs=out_specs,
        out_shape=out_shape,
        scratch_shapes=_slab_scratch(nout) if mode == "colblock" else [],
        compiler_params=_params("parallel"),
        name="norm_proj_" + mode,
    )(x, gain.reshape(1, d), w)
    return out[0] if len(out) == 1 else out


def _chunk_rows_to_tokens(y_ref, slab_ref, tok_ref):
    L = SCAN_CHUNK
    nchunk = y_ref.shape[1]
    for cb in range(y_ref.shape[0]):
        for t in range(L):
            for hf in range(MXU_WIDTH // LANES):
                lo = t * MXU_WIDTH + hf * LANES
                slab_ref[cb * (MXU_WIDTH // LANES) + hf, pl.ds(t, nchunk, stride=L), :] = (
                    y_ref[cb, :, lo:lo + LANES].astype(F32))
    for sl in range(slab_ref.shape[0]):
        tok_ref[:, sl * LANES:(sl + 1) * LANES] = slab_ref[sl].astype(BF16)


def _tail_kernel(*refs, glu, chunked, projs, use_slab):
    n_in = 8 + 2 * len(projs)
    y_ref, wo_ref, x_ref, gpost_ref, gpre_ref, wu_ref, wd_ref, gmlp_ref = refs[:8]
    proj_in = refs[8:n_in]
    n_out = 1 + sum(2 if mode == "kv" else 1 for mode, _ in projs)
    o_ref = refs[n_in]
    proj_out = refs[n_in + 1:n_in + n_out]
    slab_ref, tok_ref = refs[n_in + n_out:] if use_slab else (None, None)
    tm, d = o_ref.shape
    rows = tm // TAIL_SPLIT
    if chunked:
        _chunk_rows_to_tokens(y_ref, slab_ref, tok_ref)
    y_tok = tok_ref if chunked else y_ref

    def stages(r0):
        acc = jnp.dot(y_tok[r0:r0 + rows, :], wo_ref[...], preferred_element_type=F32)
        if glu:
            acc = acc[:, :d] * jax.nn.sigmoid(acc[:, d:])
        yield
        x = x_ref[r0:r0 + rows, :] + _rms(acc, gpost_ref[...])
        hn = _rms(x, gpre_ref[...]).astype(BF16)
        yield
        acc = None
        for c in range(wu_ref.shape[1] // FF_CHUNK):
            a = jnp.dot(hn, wu_ref[:, c * FF_CHUNK:(c + 1) * FF_CHUNK], preferred_element_type=F32)
            a = jnp.maximum(a, 0.0)
            a = (a * a).astype(BF16)
            part = jnp.dot(a, wd_ref[c * FF_CHUNK:(c + 1) * FF_CHUNK, :], preferred_element_type=F32)
            acc = part if acc is None else acc + part
            yield
        x = x + _rms(acc, gmlp_ref[...])
        o_ref[r0:r0 + rows, :] = x
        k = 0
        for i, (mode, scale) in enumerate(projs):
            yield
            cnt = 2 if mode == "kv" else 1
            _project(x, proj_in[2 * i], proj_in[2 * i + 1], proj_out[k:k + cnt], slab_ref, mode, scale, r0)
            k += cnt

    active = [stages(g * rows) for g in range(TAIL_SPLIT)]
    while active:
        for gen in list(active):
            if next(gen, "done") == "done":
                active.remove(gen)


def _layer_weight(w):
    if isinstance(w, tuple):
        stack, layer = w
        return stack, pl.BlockSpec((None,) + stack.shape[1:], lambda *_: (layer, 0, 0),
                                   pipeline_mode=pl.Buffered(1))
    return w, _const_spec(w.shape)


def _layer_tail(y, w_out, x, g_post, g_pre, w_up, w_down, g_mlp, projs, *, glu, chunked, batch, seq):
    n, d = x.shape
    tm = ROW_TILE
    row = pl.BlockSpec((tm, d), lambda i: (i, 0))
    if chunked:
        ncb, _, width = y.shape
        y_spec = pl.BlockSpec((ncb, tm // SCAN_CHUNK, width), lambda i: (0, i, 0))
    else:
        y_spec = pl.BlockSpec((tm, y.shape[1]), lambda i: (i, 0))
    (w_out, wo_spec), (w_up, wu_spec), (w_down, wd_spec) = map(_layer_weight, (w_out, w_up, w_down))
    in_specs = [y_spec, wo_spec, row, _const_spec((1, d)), _const_spec((1, d)),
                wu_spec, wd_spec, _const_spec((1, d))]
    args = [y, w_out, x, g_post.reshape(1, d), g_pre.reshape(1, d), w_up, w_down, g_mlp.reshape(1, d)]
    out_shape = [jax.ShapeDtypeStruct((n, d), F32)]
    out_specs = [row]
    for mode, gain, w, _ in projs:
        w, w_spec = _layer_weight(w)
        in_specs += [_const_spec((1, d)), w_spec]
        args += [gain.reshape(1, d), w]
        shapes, specs = _proj_outputs(mode, n, d, w.shape[-1], batch, seq)
        out_shape += shapes
        out_specs += specs
    use_slab = chunked or any(mode == "colblock" for mode, *_ in projs)
    out = pl.pallas_call(
        functools.partial(_tail_kernel, glu=glu, chunked=chunked,
                          projs=tuple((mode, scale) for mode, _, _, scale in projs), use_slab=use_slab),
        grid=(n // tm,),
        in_specs=in_specs,
        out_specs=out_specs,
        out_shape=out_shape,
        scratch_shapes=_slab_scratch(d) if use_slab else [],
        compiler_params=_params("parallel"),
        name="layer_tail",
    )(*args)
    return out[0], list(out[1:])


def _cmul(a, b):
    return a[0] * b[0] - a[1] * b[1], a[0] * b[1] + a[1] * b[0]


def _zoh(lr, li, log_dt):
    dt = jnp.exp(log_dt)
    mag = jnp.exp(lr * dt)
    ab = (mag * jnp.cos(li * dt), mag * jnp.sin(li * dt))
    den = lr * lr + li * li
    coef = (((ab[0] - 1.0) * lr + ab[1] * li) / den, (ab[1] * lr - (ab[0] - 1.0) * li) / den)
    return ab, coef


def _s5_kernel(u_ref, lamc_ref, lamr_ref, bt_ref, ct_ref, d_ref, o_ref,
               w_ref, k_ref, p_ref, apr_ref, api_ref, v_ref, sp_ref, sr_ref, si_ref):
    cw = MXU_WIDTH
    ns = v_ref.shape[1] // 2
    L = u_ref.shape[2] // cw
    grp_shift = SSM_GROUP.bit_length() - 1
    st_shift = SSM_STATE.bit_length() - 1

    @pl.when(jnp.logical_and(pl.program_id(1) == 0, pl.program_id(2) == 0))
    def _():
        def iota(shape, dim):
            return lax.broadcasted_iota(jnp.int32, shape, dim)

        def nt_dot(a, b):
            return lax.dot_general(a, b, (((1,), (1,)), ((), ())), precision=lax.Precision.HIGHEST,
                                   preferred_element_type=F32)

        wmask = (iota((cw, ns), 0) >> grp_shift) == (iota((cw, ns), 1) >> st_shift)
        pmask = (iota((ns, cw), 0) >> st_shift) == (iota((ns, cw), 1) >> grp_shift)
        kmask = (iota((cw, cw), 0) >> grp_shift) == (iota((cw, cw), 1) >> grp_shift)
        ab, coef = _zoh(lamc_ref[0, 0], lamc_ref[0, 1], lamc_ref[0, 2])
        bb = _cmul(coef, (bt_ref[0, 0], bt_ref[0, 1]))
        ct = (ct_ref[0, 0], ct_ref[0, 1])
        pw = [(jnp.ones_like(ab[0]), jnp.zeros_like(ab[0]))]
        for _ in range(L):
            pw.append(_cmul(pw[-1], ab))
        for t in range(L):
            w = _cmul(pw[L - 1 - t], bb)
            for h in range(2):
                tiled = jnp.concatenate([w[h]] * (ns // LANES), axis=1)
                w_ref[t * cw:(t + 1) * cw, h * ns:(h + 1) * ns] = (
                    jnp.where(wmask, tiled, 0.0).astype(BF16))
            km = 0.5 * (nt_dot(w[0], ct[0]) - nt_dot(w[1], ct[1]))
            k_ref[t * cw:(t + 1) * cw, :] = jnp.where(kmask, km, 0.0).astype(BF16)
            q = _cmul(ct, pw[t + 1])
            for h, qh in enumerate((q[0], -q[1])):
                piece = qh.T[:SSM_STATE, :]
                tiled = jnp.concatenate([piece] * (ns // SSM_STATE), axis=0)
                p_ref[t, h * ns:(h + 1) * ns, :] = jnp.where(pmask, tiled, 0.0).astype(BF16)
        a, _ = _zoh(lamr_ref[0, 0:1], lamr_ref[0, 1:2], lamr_ref[0, 2:3])
        al = a
        for _ in range(L - 1):
            al = _cmul(al, a)
        q = al
        for r in range(SUBLANES):
            apr_ref[r:r + 1, :] = q[0]
            api_ref[r:r + 1, :] = q[1]
            q = _cmul(q, al)

    @pl.when(pl.program_id(2) == 0)
    def _():
        sr_ref[...] = jnp.zeros_like(sr_ref)
        si_ref[...] = jnp.zeros_like(si_ref)

    rows = v_ref.shape[0]
    half = rows // 2
    apr = apr_ref[...]
    api = api_ref[...]
    row = lax.broadcasted_iota(jnp.int32, (SUBLANES, ns), 0)

    def shift(z, k, fill):
        return jnp.where(row >= k, pltpu.roll(z, k, 0), fill)

    def scan_block(i, carry):
        sr, si = carry
        lo, hi_row = SUBLANES * i, SUBLANES * (i + 1)
        zr = v_ref[lo:hi_row, 0:ns]
        zi = v_ref[lo:hi_row, ns:2 * ns]
        k = 1
        while k < SUBLANES:
            ar, ai = apr[k - 1:k], api[k - 1:k]
            hr, hi = shift(zr, k, 0.0), shift(zi, k, 0.0)
            zr, zi = zr + ar * hr - ai * hi, zi + ar * hi + ai * hr
            k *= 2
        fr = zr + apr * sr - api * si
        fi = zi + apr * si + api * sr
        v_ref[lo:hi_row, 0:ns] = shift(fr, 1, sr)
        v_ref[lo:hi_row, ns:2 * ns] = shift(fi, 1, si)
        return fr[SUBLANES - 1:], fi[SUBLANES - 1:]

    def outputs(lo, t):
        xs = u_ref[0, lo:lo + half, :]
        acc = jnp.dot(xs[:, :(t + 1) * cw], k_ref[(L - 1 - t) * cw:, :], preferred_element_type=F32)
        acc = acc + jnp.dot(sp_ref[lo:lo + half, :], p_ref[t], preferred_element_type=F32)
        y = acc + d_ref[0] * xs[:, t * cw:(t + 1) * cw].astype(F32)
        o_ref[0, lo:lo + half, t * cw:(t + 1) * cw] = jax.nn.gelu(y).astype(BF16)

    nblk_half = half // SUBLANES
    pieces = 4
    ncols = 2 * ns // pieces
    v_ref[:half, :] = jnp.dot(u_ref[0, :half, :], w_ref[...], preferred_element_type=F32)
    carry = (sr_ref[...], si_ref[...])
    for n in range(pieces):
        v_ref[half:, n * ncols:(n + 1) * ncols] = jnp.dot(
            u_ref[0, half:, :], w_ref[:, n * ncols:(n + 1) * ncols], preferred_element_type=F32)
        for i in range(n * nblk_half // pieces, (n + 1) * nblk_half // pieces):
            carry = scan_block(i, carry)
    sp_ref[:half, :] = v_ref[:half, :].astype(BF16)
    for t in range(L):
        outputs(0, t)
        for i in range(t * nblk_half // L, (t + 1) * nblk_half // L):
            carry = scan_block(nblk_half + i, carry)
    sr_ref[...], si_ref[...] = carry
    sp_ref[half:, :] = v_ref[half:, :].astype(BF16)
    for t in range(L):
        outputs(half, t)


def _s5_core(u_v, ops, d_skip, *, batch, seq):
    lamc, lamr, bt, ct = ops
    L = SCAN_CHUNK
    cw = MXU_WIDTH
    ncb = u_v.shape[0]
    rows_per_b = seq // L
    halves = rows_per_b // SCAN_ROWS
    ns = lamr.shape[2]
    blk = pl.BlockSpec((1, SCAN_ROWS, L * cw), lambda cb, b, h: (cb, b * halves + h, 0))

    def per_block(a):
        zeros = (0,) * (a.ndim - 1)
        return pl.BlockSpec((1,) + a.shape[1:], lambda cb, b, h: (cb,) + zeros)

    return pl.pallas_call(
        _s5_kernel,
        grid=(ncb, batch, halves),
        in_specs=[blk, per_block(lamc), per_block(lamr), per_block(bt), per_block(ct),
                  pl.BlockSpec((1, 1, cw), lambda cb, b, h: (cb, 0, 0))],
        out_specs=blk,
        out_shape=jax.ShapeDtypeStruct(u_v.shape, BF16),
        scratch_shapes=[pltpu.VMEM((L * cw, 2 * ns), BF16),
                        pltpu.VMEM((L * cw, cw), BF16),
                        pltpu.VMEM((L, 2 * ns, cw), BF16),
                        pltpu.VMEM((SUBLANES, ns), F32),
                        pltpu.VMEM((SUBLANES, ns), F32),
                        pltpu.VMEM((SCAN_ROWS, 2 * ns), F32),
                        pltpu.VMEM((SCAN_ROWS, 2 * ns), BF16),
                        pltpu.VMEM((1, ns), F32),
                        pltpu.VMEM((1, ns), F32)],
        compiler_params=_params("arbitrary", "arbitrary", "arbitrary"),
        name="s5_core",
    )(u_v, lamc, lamr, bt, ct, d_skip.reshape(ncb, 1, cw))


def _s5_param_layout(a_re, a_im, log_dt, b_re, b_im, c_re, c_im):
    groups, p = a_re.shape
    gl = MXU_WIDTH // SSM_GROUP
    ncb = groups // gl
    lam = jnp.stack([a_re, a_im, jnp.broadcast_to(log_dt[:, None], (groups, p))]).astype(F32)
    lamc = jnp.broadcast_to(lam[:, :, None, :], (3, groups, SSM_GROUP, p)).reshape(3, ncb, MXU_WIDTH, p)
    lamr = jnp.pad(lam.reshape(3, ncb, gl * p), ((0, SUBLANES - 3), (0, 0), (0, 0)))
    bt = jnp.stack([b_re, b_im]).astype(F32).transpose(0, 1, 3, 2).reshape(2, ncb, MXU_WIDTH, p)
    ct = jnp.stack([c_re, c_im]).astype(F32).reshape(2, ncb, MXU_WIDTH, p)

    def twice(a):
        return jnp.concatenate([a] * (LANES // p), axis=-1).transpose(1, 0, 2, 3)

    return twice(lamc), lamr.transpose(1, 0, 2), twice(bt), twice(ct)


def _bucket_tiles(t):
    i = np.arange(t)[:, None]
    j = np.arange(t)[None, :]
    tiles = []
    for off in (0, t):
        rel = j + off - i
        n = np.maximum(rel, 0)
        max_exact = NUM_BUCKETS // 2
        large = max_exact + (np.log(np.maximum(n, max_exact).astype(np.float32) / max_exact)
                             / math.log(MAX_DISTANCE / max_exact)
                             * (NUM_BUCKETS - max_exact)).astype(np.int32)
        large = np.minimum(large, NUM_BUCKETS - 1)
        b = np.where(n < max_exact, n, large)
        tiles.append(np.where(rel >= 0, b, -1).astype(np.int32))
    return np.stack(tiles)


def _bias_kernel(bkt_ref, tab_ref, o_ref):
    h = pl.program_id(0)
    far = tab_ref[NUM_BUCKETS - 1, h]
    for typ in range(2):
        b = bkt_ref[typ]
        acc = jnp.zeros(b.shape, F32)
        for i in range(NUM_BUCKETS - 1):
            acc = jnp.where(b == i, (tab_ref[i, h] - far) * LOG2E, acc)
        o_ref[0, typ] = jnp.where(b < 0, NEG_INF, acc)


def _bias_tiles(rel_bias, t):
    heads = rel_bias.shape[1]
    return pl.pallas_call(
        _bias_kernel,
        grid=(heads,),
        in_specs=[_const_spec((2, t, t)), pl.BlockSpec(memory_space=pltpu.SMEM)],
        out_specs=pl.BlockSpec((1, 2, t, t), lambda h: (h, 0, 0, 0)),
        out_shape=jax.ShapeDtypeStruct((heads, 2, t, t), F32),
        compiler_params=_params("parallel"),
        name="rel_bias_tiles",
    )(jnp.asarray(_bucket_tiles(t)), rel_bias.astype(F32))


def _flash_kernel(q_ref, *refs, lam_init):
    def query_tile(qi, carry):
        _flash_tile(qi, q_ref, *refs, lam_init=lam_init)
        return carry

    lax.fori_loop(0, q_ref.shape[1], query_tile, 0)


def _flash_tile(qi, q_ref, k_ref, vt_ref, bias_ref, lam_ref, hn_ref, o_ref,
                qcat_ref, sa_ref, sb_ref, m_ref, l_ref, acc_ref, *, lam_init):
    tq = q_ref.shape[3]
    tk = bias_ref.shape[2]
    cw = MXU_WIDTH
    per_branch = tq // cw
    ncol = 2 * per_branch
    per_key = tk // cw
    inner = tq // tk
    assert inner % 2 == 0, "the far-tile loop below relies on an odd number of far tiles"
    qt = q_ref[0, qi]
    row = lax.broadcasted_iota(jnp.int32, qt.shape, 0)
    zero = jnp.zeros_like(qt)
    qcat_ref[:, :tq] = jnp.where(row < HEAD_DIM, qt, zero)
    qcat_ref[:, tq:] = jnp.where(row >= HEAD_DIM, qt, zero)

    m_ref[...] = jnp.full(m_ref.shape, NEG_INF, F32)
    l_ref[...] = jnp.zeros_like(l_ref)
    acc_ref[...] = jnp.zeros_like(acc_ref)

    def bias_of(kind, c):
        if kind is None:
            return None
        cq = c % per_branch
        dist = cq // per_key - kind
        if dist < 0:
            return "skip"
        if dist > 1:
            return None
        return dist, (cq % per_key) * cw

    def scores_col(kblk, dst_ref, kind, c):
        spec = bias_of(kind, c)
        if spec == "skip":
            return None
        s = jnp.dot(kblk, qcat_ref[:, c * cw:(c + 1) * cw], preferred_element_type=F32)
        if spec is not None:
            s = s + bias_ref[0, spec[0], :, spec[1]:spec[1] + cw]
        dst_ref[c] = s
        return jnp.max(s, axis=0, keepdims=True)

    def softmax_pv_col(vblk, src_ref, cm, kind, c):
        if bias_of(kind, c) == "skip":
            return
        s = src_ref[c]
        m_old = m_ref[c]
        m_new = jnp.maximum(m_old, cm)
        alpha = jnp.exp2(m_old - m_new)
        p = jnp.exp2(s - m_new)
        l_ref[c] = alpha * l_ref[c] + jnp.sum(p, axis=0, keepdims=True)
        m_ref[c] = m_new
        acc_ref[c] = alpha * acc_ref[c] + jnp.dot(vblk, p.astype(BF16), preferred_element_type=F32)

    def key_tile(j):
        return k_ref[0, pl.ds(pl.multiple_of(j * tk, tk), tk), :]

    def scores(j, dst_ref, kind):
        kblk = key_tile(j)
        return tuple(scores_col(kblk, dst_ref, kind, c) for c in range(ncol))

    def step(nxt, cur, cmax):
        j, kind, src_ref = cur
        vblk = vt_ref[0, 0, j]
        kblk = key_tile(nxt[0]) if nxt is not None else None
        cm_next = []
        for c in range(ncol):
            if nxt is not None:
                cm_next.append(scores_col(kblk, nxt[2], nxt[1], c))
            softmax_pv_col(vblk, src_ref, cmax[c], kind, c)
        return tuple(cm_next)

    def run(tiles, first_max):
        bufs = (sa_ref, sb_ref)
        cmax = first_max
        for n, (j, kind) in enumerate(tiles):
            nxt = None
            if n + 1 < len(tiles):
                nxt = tiles[n + 1] + (bufs[(n + 1) % 2],)
            cmax = step(nxt, (j, kind, bufs[n % 2]), cmax)

    first_diag = qi * inner
    diag = [(first_diag + r, r) for r in range(inner)]

    @pl.when(qi == 0)
    def _():
        run(diag, scores(0, sa_ref, 0))

    @pl.when(qi >= 1)
    def _():
        n_far = first_diag - 1

        def pair(i, cm_a):
            j = 2 * i
            cm_b = step((j + 1, None, sb_ref), (j, None, sa_ref), cm_a)
            return step((j + 2, None, sa_ref), (j + 1, None, sb_ref), cm_b)

        cm_a = lax.fori_loop(0, (n_far - 1) // 2, pair, scores(0, sa_ref, None))
        run([(n_far - 1, None), (n_far, -1)] + diag, cm_a)

    lv = lam_ref[...]
    lam = (jnp.exp(jnp.sum(lv[0:1] * lv[1:2], axis=1, keepdims=True))
           - jnp.exp(jnp.sum(lv[2:3] * lv[3:4], axis=1, keepdims=True)) + lam_init)
    for c in range(per_branch):
        o1 = acc_ref[c] * (1.0 / l_ref[c])
        o2 = acc_ref[per_branch + c] * (1.0 / l_ref[per_branch + c])
        o = o1 - lam * o2
        o = o * lax.rsqrt(jnp.mean(o * o, axis=0, keepdims=True) + NORM_EPS)
        o = o * hn_ref[...] * (1.0 - lam_init)
        o_ref[0, pl.ds(pl.multiple_of(qi * tq + c * cw, cw), cw), :] = o.T.astype(BF16)


def _diff_attention(qt, k, vt, bias, lam_vecs, head_norm, *, lam_init):
    batch, nq, d, tq = qt.shape
    seq = nq * tq
    heads = d // V_DIM
    t = ATTN_TILE
    nblk = seq // t
    ncol = 2 * tq // MXU_WIDTH
    lam_pad = jnp.zeros((SUBLANES, V_DIM), F32).at[:4, :HEAD_DIM].set(lam_vecs.astype(F32))
    return pl.pallas_call(
        functools.partial(_flash_kernel, lam_init=lam_init),
        grid=(batch, heads),
        in_specs=[pl.BlockSpec((1, nq, V_DIM, tq), lambda b, h: (b, 0, h, 0)),
                  pl.BlockSpec((1, seq, V_DIM), lambda b, h: (b, 0, h)),
                  pl.BlockSpec((1, 1, nblk, V_DIM, t), lambda b, h: (b, h, 0, 0, 0)),
                  pl.BlockSpec((1, 2, t, t), lambda b, h: (h, 0, 0, 0)),
                  _const_spec((SUBLANES, V_DIM)),
                  _const_spec((V_DIM, 1))],
        out_specs=pl.BlockSpec((1, seq, V_DIM), lambda b, h: (b, 0, h)),
        out_shape=jax.ShapeDtypeStruct((batch, seq, d), BF16),
        scratch_shapes=[pltpu.VMEM((V_DIM, 2 * tq), BF16),
                        pltpu.VMEM((ncol, t, MXU_WIDTH), F32),
                        pltpu.VMEM((ncol, t, MXU_WIDTH), F32),
                        pltpu.VMEM((ncol, 1, MXU_WIDTH), F32),
                        pltpu.VMEM((ncol, 1, MXU_WIDTH), F32),
                        pltpu.VMEM((ncol, V_DIM, MXU_WIDTH), F32)],
        compiler_params=_params("parallel", "parallel"),
        name="diff_attention",
    )(qt, k.reshape(batch, seq, d), vt, bias, lam_pad, head_norm.astype(F32).reshape(V_DIM, 1))


def kernel(x, norm_mixer_pre, norm_mixer_post, norm_mlp_pre, norm_mlp_post, mlp_w_up, mlp_w_down,
           ssm_w_in, ssm_a_re, ssm_a_im, ssm_log_dt, ssm_b_re, ssm_b_im, ssm_c_re, ssm_c_im, ssm_d,
           ssm_w_glu, kv_norm, w_kv, attn_w_q, attn_lambda_q1, attn_lambda_k1, attn_lambda_q2,
           attn_lambda_k2, attn_head_norm, attn_w_o, rel_bias):
    batch, seq, d = x.shape
    depth = norm_mixer_pre.shape[0]
    n_ssm = ssm_w_in.shape[0]
    assert 0 < n_ssm < depth
    xs = x.reshape(batch * seq, d).astype(F32)
    q_scale = HEAD_DIM ** -0.5 * LOG2E
    w_up, w_down, w_in, w_glu, w_q, w_o = (
        w.astype(BF16) for w in (mlp_w_up, mlp_w_down, ssm_w_in, ssm_w_glu, attn_w_q, attn_w_o))

    def mixer_projections(layer):
        if layer >= depth:
            return []
        if layer < n_ssm:
            return [("colblock", norm_mixer_pre[layer], (w_in, layer), 1.0)]
        projs = []
        if layer == n_ssm:
            projs.append(("kv", kv_norm, w_kv.astype(BF16), 1.0))
        projs.append(("qT", norm_mixer_pre[layer], (w_q, layer - n_ssm), q_scale))
        return projs

    feeds = [_norm_proj(xs, norm_mixer_pre[0], w_in[0], mode="colblock", batch=batch, seq=seq)]
    bias = _bias_tiles(rel_bias, ATTN_TILE)
    k = vt = None
    for layer in range(depth):
        if layer < n_ssm:
            i = layer
            ops = _s5_param_layout(ssm_a_re[i], ssm_a_im[i], ssm_log_dt[i], ssm_b_re[i], ssm_b_im[i],
                                ssm_c_re[i], ssm_c_im[i])
            y = _s5_core(feeds[0], ops, ssm_d[i].astype(F32), batch=batch, seq=seq)
            w_out = (w_glu, i)
        else:
            if layer == n_ssm:
                k, vt = feeds[0], feeds[1]
            j = layer - n_ssm
            lam_init = 0.8 - 0.6 * math.exp(-0.3 * layer)
            lam_vecs = jnp.stack([attn_lambda_q1[j], attn_lambda_k1[j],
                                  attn_lambda_q2[j], attn_lambda_k2[j]])
            y = _diff_attention(feeds[-1], k, vt, bias, lam_vecs, attn_head_norm[j], lam_init=lam_init)
            y = y.reshape(batch * seq, d)
            w_out = (w_o, j)
        xs, feeds = _layer_tail(y, w_out, xs, norm_mixer_post[layer], norm_mlp_pre[layer],
                                (w_up, layer), (w_down, layer),
                                norm_mlp_post[layer], mixer_projections(layer + 1),
                                glu=layer < n_ssm, chunked=layer < n_ssm, batch=batch, seq=seq)
    return xs.reshape(batch, seq, d).astype(x.dtype)
```

```python
import functools
import math

import numpy as np
import jax
import jax.numpy as jnp
from jax import lax
from jax.experimental import pallas as pl
from jax.experimental.pallas import tpu as pltpu

F32 = jnp.float32
BF16 = jnp.bfloat16

SSM_GROUP = 16
SSM_STATE = 64
HEAD_DIM = 64
V_DIM = 2 * HEAD_DIM
NUM_BUCKETS = 32
MAX_DISTANCE = 128
NORM_EPS = 1e-6
NEG_INF = -1e30
LOG2E = math.log2(math.e)

MXU_WIDTH = 256
LANES = 128
SUBLANES = 8
ROW_TILE = 512
SCAN_CHUNK = 8
SCAN_ROWS = 1024
ATTN_TILE = 512
ATTN_QTILE = 2048
FF_CHUNK = 1024
TAIL_SPLIT = 2
VMEM_LIMIT = 56 * 1024 * 1024


def _params(*sem):
    return pltpu.CompilerParams(dimension_semantics=sem, vmem_limit_bytes=VMEM_LIMIT)


def _rms(xf, gain):
    return xf * lax.rsqrt(jnp.mean(xf * xf, axis=-1, keepdims=True) + NORM_EPS) * gain


def _const_spec(shape):
    zeros = (0,) * len(shape)
    return pl.BlockSpec(shape, lambda *_: zeros, pipeline_mode=pl.Buffered(1))


def _project(x, g_ref, w_ref, out_refs, slab_ref, mode, scale, r0=0):
    hn = _rms(x, g_ref[...]).astype(BF16)
    acc = jnp.dot(hn, w_ref[...], preferred_element_type=F32)
    rows = acc.shape[0]
    if mode == "colblock":
        (o_ref,) = out_refs
        L = SCAN_CHUNK
        for sl in range(slab_ref.shape[0]):
            slab_ref[sl, r0:r0 + rows, :] = acc[:, sl * LANES:(sl + 1) * LANES]
        for cb in range(o_ref.shape[0]):
            for t in range(L):
                for hf in range(MXU_WIDTH // LANES):
                    lo = t * MXU_WIDTH + hf * LANES
                    piece = slab_ref[cb * (MXU_WIDTH // LANES) + hf, pl.ds(r0 + t, rows // L, stride=L), :]
                    o_ref[cb, r0 // L:(r0 + rows) // L, lo:lo + LANES] = piece.astype(BF16)
    elif mode == "qT":
        (o_ref,) = out_refs
        o_ref[0, 0, :, r0:r0 + rows] = (acc * scale).T.astype(BF16)
    else:
        k_ref, vt_ref = out_refs
        d = k_ref.shape[1]
        k_ref[r0:r0 + rows, :] = acc[:, :d].astype(BF16)
        vt = acc[:, d:].T.astype(BF16)
        tk = vt_ref.shape[4]
        step = min(rows, tk)
        for n in range(rows // step):
            at = r0 + n * step
            vt_ref[0, :, at // tk, :, at % tk:at % tk + step] = (
                vt[:, n * step:(n + 1) * step].reshape(d // V_DIM, V_DIM, step))


def _proj_outputs(mode, n, d, nout, batch, seq):
    tm = ROW_TILE
    per_b = seq // tm
    if mode == "colblock":
        ncb = nout // MXU_WIDTH
        L = SCAN_CHUNK
        return ([jax.ShapeDtypeStruct((ncb, n // L, L * MXU_WIDTH), BF16)],
                [pl.BlockSpec((ncb, tm // L, L * MXU_WIDTH), lambda i: (0, i, 0))])
    if mode == "qT":
        tq = ATTN_QTILE
        per_q = tq // tm
        return ([jax.ShapeDtypeStruct((batch, seq // tq, nout, tq), BF16)],
                [pl.BlockSpec((1, 1, nout, tm),
                              lambda i: (i // per_b, (i % per_b) // per_q, 0, (i % per_b) % per_q))])
    heads = d // V_DIM
    tk = ATTN_TILE
    assert mode == "kv" and tm % tk == 0
    return ([jax.ShapeDtypeStruct((n, d), BF16),
             jax.ShapeDtypeStruct((batch, heads, seq // tk, V_DIM, tk), BF16)],
            [pl.BlockSpec((tm, d), lambda i: (i, 0)),
             pl.BlockSpec((1, heads, tm // tk, V_DIM, tk), lambda i: (i // per_b, 0, i % per_b, 0, 0))])


def _slab_scratch(d):
    return [pltpu.VMEM((d // LANES, ROW_TILE, LANES), F32), pltpu.VMEM((ROW_TILE, d), BF16)]


def _proj_kernel(x_ref, g_ref, w_ref, *refs, mode, scale):
    slab_ref = refs[-2] if mode == "colblock" else None
    nout = 2 if mode == "kv" else 1
    _project(x_ref[...], g_ref, w_ref, refs[:nout], slab_ref, mode, scale)


def _norm_proj(x, gain, w, *, mode, batch, seq, scale=1.0):
    n, d = x.shape
    nout = w.shape[1]
    tm = ROW_TILE
    out_shape, out_specs = _proj_outputs(mode, n, d, nout, batch, seq)
    out = pl.pallas_call(
        functools.partial(_proj_kernel, mode=mode, scale=scale),
        grid=(n // tm,),
        in_specs=[pl.BlockSpec((tm, d), lambda i: (i, 0)),
                  _const_spec((1, d)),
                  _const_spec((d, nout))],
        out_specs=out_specs,
        out_shape=out_shape,
        scratch_shapes=_slab_scratch(nout) if mode == "colblock" else [],
        compiler_params=_params("parallel"),
        name="norm_proj_" + mode,
    )(x, gain.reshape(1, d), w)
    return out[0] if len(out) == 1 else out


def _chunk_rows_to_tokens(y_ref, slab_ref, tok_ref):
    L = SCAN_CHUNK
    nchunk = y_ref.shape[1]
    for cb in range(y_ref.shape[0]):
        for t in range(L):
            for hf in range(MXU_WIDTH // LANES):
                lo = t * MXU_WIDTH + hf * LANES
                slab_ref[cb * (MXU_WIDTH // LANES) + hf, pl.ds(t, nchunk, stride=L), :] = (
                    y_ref[cb, :, lo:lo + LANES].astype(F32))
    for sl in range(slab_ref.shape[0]):
        tok_ref[:, sl * LANES:(sl + 1) * LANES] = slab_ref[sl].astype(BF16)


def _tail_kernel(*refs, glu, chunked, projs, use_slab):
    n_in = 8 + 2 * len(projs)
    y_ref, wo_ref, x_ref, gpost_ref, gpre_ref, wu_ref, wd_ref, gmlp_ref = refs[:8]
    proj_in = refs[8:n_in]
    n_out = 1 + sum(2 if mode == "kv" else 1 for mode, _ in projs)
    o_ref = refs[n_in]
    proj_out = refs[n_in + 1:n_in + n_out]
    slab_ref, tok_ref = refs[n_in + n_out:] if use_slab else (None, None)
    tm, d = o_ref.shape
    rows = tm // TAIL_SPLIT
    if chunked:
        _chunk_rows_to_tokens(y_ref, slab_ref, tok_ref)
    y_tok = tok_ref if chunked else y_ref

    def stages(r0):
        acc = jnp.dot(y_tok[r0:r0 + rows, :], wo_ref[...], preferred_element_type=F32)
        if glu:
            acc = acc[:, :d] * jax.nn.sigmoid(acc[:, d:])
        yield
        x = x_ref[r0:r0 + rows, :] + _rms(acc, gpost_ref[...])
        hn = _rms(x, gpre_ref[...]).astype(BF16)
        yield
        acc = None
        for c in range(wu_ref.shape[1] // FF_CHUNK):
            a = jnp.dot(hn, wu_ref[:, c * FF_CHUNK:(c + 1) * FF_CHUNK], preferred_element_type=F32)
            a = jnp.maximum(a, 0.0)
            a = (a * a).astype(BF16)
            part = jnp.dot(a, wd_ref[c * FF_CHUNK:(c + 1) * FF_CHUNK, :], preferred_element_type=F32)
            acc = part if acc is None else acc + part
            yield
        x = x + _rms(acc, gmlp_ref[...])
        o_ref[r0:r0 + rows, :] = x
        k = 0
        for i, (mode, scale) in enumerate(projs):
            yield
            cnt = 2 if mode == "kv" else 1
            _project(x, proj_in[2 * i], proj_in[2 * i + 1], proj_out[k:k + cnt], slab_ref, mode, scale, r0)
            k += cnt

    active = [stages(g * rows) for g in range(TAIL_SPLIT)]
    while active:
        for gen in list(active):
            if next(gen, "done") == "done":
                active.remove(gen)


def _layer_weight(w):
    if isinstance(w, tuple):
        stack, layer = w
        return stack, pl.BlockSpec((None,) + stack.shape[1:], lambda *_: (layer, 0, 0),
                                   pipeline_mode=pl.Buffered(1))
    return w, _const_spec(w.shape)


def _layer_tail(y, w_out, x, g_post, g_pre, w_up, w_down, g_mlp, projs, *, glu, chunked, batch, seq):
    n, d = x.shape
    tm = ROW_TILE
    row = pl.BlockSpec((tm, d), lambda i: (i, 0))
    if chunked:
        ncb, _, width = y.shape
        y_spec = pl.BlockSpec((ncb, tm // SCAN_CHUNK, width), lambda i: (0, i, 0))
    else:
        y_spec = pl.BlockSpec((tm, y.shape[1]), lambda i: (i, 0))
    (w_out, wo_spec), (w_up, wu_spec), (w_down, wd_spec) = map(_layer_weight, (w_out, w_up, w_down))
    in_specs = [y_spec, wo_spec, row, _const_spec((1, d)), _const_spec((1, d)),
                wu_spec, wd_spec, _const_spec((1, d))]
    args = [y, w_out, x, g_post.reshape(1, d), g_pre.reshape(1, d), w_up, w_down, g_mlp.reshape(1, d)]
    out_shape = [jax.ShapeDtypeStruct((n, d), F32)]
    out_specs = [row]
    for mode, gain, w, _ in projs:
        w, w_spec = _layer_weight(w)
        in_specs += [_const_spec((1, d)), w_spec]
        args += [gain.reshape(1, d), w]
        shapes, specs = _proj_outputs(mode, n, d, w.shape[-1], batch, seq)
        out_shape += shapes
        out_specs += specs
    use_slab = chunked or any(mode == "colblock" for mode, *_ in projs)
    out = pl.pallas_call(
        functools.partial(_tail_kernel, glu=glu, chunked=chunked,
                          projs=tuple((mode, scale) for mode, _, _, scale in projs), use_slab=use_slab),
        grid=(n // tm,),
        in_specs=in_specs,
        out_specs=out_specs,
        out_shape=out_shape,
        scratch_shapes=_slab_scratch(d) if use_slab else [],
        compiler_params=_params("parallel"),
        name="layer_tail",
    )(*args)
    return out[0], list(out[1:])


def _cmul(a, b):
    return a[0] * b[0] - a[1] * b[1], a[0] * b[1] + a[1] * b[0]


def _zoh(lr, li, log_dt):
    dt = jnp.exp(log_dt)
    mag = jnp.exp(lr * dt)
    ab = (mag * jnp.cos(li * dt), mag * jnp.sin(li * dt))
    den = lr * lr + li * li
    coef = (((ab[0] - 1.0) * lr + ab[1] * li) / den, (ab[1] * lr - (ab[0] - 1.0) * li) / den)
    return ab, coef


def _s5_kernel(u_ref, lamc_ref, lamr_ref, bt_ref, ct_ref, d_ref, o_ref,
               w_ref, k_ref, p_ref, apr_ref, api_ref, v_ref, sp_ref, sr_ref, si_ref):
    cw = MXU_WIDTH
    ns = v_ref.shape[1] // 2
    L = u_ref.shape[2] // cw
    grp_shift = SSM_GROUP.bit_length() - 1
    st_shift = SSM_STATE.bit_length() - 1

    @pl.when(jnp.logical_and(pl.program_id(1) == 0, pl.program_id(2) == 0))
    def _():
        def iota(shape, dim):
            return lax.broadcasted_iota(jnp.int32, shape, dim)

        def nt_dot(a, b):
            return lax.dot_general(a, b, (((1,), (1,)), ((), ())), precision=lax.Precision.HIGHEST,
                                   preferred_element_type=F32)

        wmask = (iota((cw, ns), 0) >> grp_shift) == (iota((cw, ns), 1) >> st_shift)
        pmask = (iota((ns, cw), 0) >> st_shift) == (iota((ns, cw), 1) >> grp_shift)
        kmask = (iota((cw, cw), 0) >> grp_shift) == (iota((cw, cw), 1) >> grp_shift)
        ab, coef = _zoh(lamc_ref[0, 0], lamc_ref[0, 1], lamc_ref[0, 2])
        bb = _cmul(coef, (bt_ref[0, 0], bt_ref[0, 1]))
        ct = (ct_ref[0, 0], ct_ref[0, 1])
        pw = [(jnp.ones_like(ab[0]), jnp.zeros_like(ab[0]))]
        for _ in range(L):
            pw.append(_cmul(pw[-1], ab))
        for t in range(L):
            w = _cmul(pw[L - 1 - t], bb)
            for h in range(2):
                tiled = jnp.concatenate([w[h]] * (ns // LANES), axis=1)
                w_ref[t * cw:(t + 1) * cw, h * ns:(h + 1) * ns] = (
                    jnp.where(wmask, tiled, 0.0).astype(BF16))
            km = 0.5 * (nt_dot(w[0], ct[0]) - nt_dot(w[1], ct[1]))
            k_ref[t * cw:(t + 1) * cw, :] = jnp.where(kmask, km, 0.0).astype(BF16)
            q = _cmul(ct, pw[t + 1])
            for h, qh in enumerate((q[0], -q[1])):
                piece = qh.T[:SSM_STATE, :]
                tiled = jnp.concatenate([piece] * (ns // SSM_STATE), axis=0)
                p_ref[t, h * ns:(h + 1) * ns, :] = jnp.where(pmask, tiled, 0.0).astype(BF16)
        a, _ = _zoh(lamr_ref[0, 0:1], lamr_ref[0, 1:2], lamr_ref[0, 2:3])
        al = a
        for _ in range(L - 1):
            al = _cmul(al, a)
        q = al
        for r in range(SUBLANES):
            apr_ref[r:r + 1, :] = q[0]
            api_ref[r:r + 1, :] = q[1]
            q = _cmul(q, al)

    @pl.when(pl.program_id(2) == 0)
    def _():
        sr_ref[...] = jnp.zeros_like(sr_ref)
        si_ref[...] = jnp.zeros_like(si_ref)

    rows = v_ref.shape[0]
    half = rows // 2
    apr = apr_ref[...]
    api = api_ref[...]
    row = lax.broadcasted_iota(jnp.int32, (SUBLANES, ns), 0)

    def shift(z, k, fill):
        return jnp.where(row >= k, pltpu.roll(z, k, 0), fill)

    def scan_block(i, carry):
        sr, si = carry
        lo, hi_row = SUBLANES * i, SUBLANES * (i + 1)
        zr = v_ref[lo:hi_row, 0:ns]
        zi = v_ref[lo:hi_row, ns:2 * ns]
        k = 1
        while k < SUBLANES:
            ar, ai = apr[k - 1:k], api[k - 1:k]
            hr, hi = shift(zr, k, 0.0), shift(zi, k, 0.0)
            zr, zi = zr + ar * hr - ai * hi, zi + ar * hi + ai * hr
            k *= 2
        fr = zr + apr * sr - api * si
        fi = zi + apr * si + api * sr
        v_ref[lo:hi_row, 0:ns] = shift(fr, 1, sr)
        v_ref[lo:hi_row, ns:2 * ns] = shift(fi, 1, si)
        return fr[SUBLANES - 1:], fi[SUBLANES - 1:]

    def outputs(lo, t):
        xs = u_ref[0, lo:lo + half, :]
        acc = jnp.dot(xs[:, :(t + 1) * cw], k_ref[(L - 1 - t) * cw:, :], preferred_element_type=F32)
        acc = acc + jnp.dot(sp_ref[lo:lo + half, :], p_ref[t], preferred_element_type=F32)
        y = acc + d_ref[0] * xs[:, t * cw:(t + 1) * cw].astype(F32)
        o_ref[0, lo:lo + half, t * cw:(t + 1) * cw] = jax.nn.gelu(y).astype(BF16)

    nblk_half = half // SUBLANES
    pieces = 4
    ncols = 2 * ns // pieces
    v_ref[:half, :] = jnp.dot(u_ref[0, :half, :], w_ref[...], preferred_element_type=F32)
    carry = (sr_ref[...], si_ref[...])
    for n in range(pieces):
        v_ref[half:, n * ncols:(n + 1) * ncols] = jnp.dot(
            u_ref[0, half:, :], w_ref[:, n * ncols:(n + 1) * ncols], preferred_element_type=F32)
        for i in range(n * nblk_half // pieces, (n + 1) * nblk_half // pieces):
            carry = scan_block(i, carry)
    sp_ref[:half, :] = v_ref[:half, :].astype(BF16)
    for t in range(L):
        outputs(0, t)
        for i in range(t * nblk_half // L, (t + 1) * nblk_half // L):
            carry = scan_block(nblk_half + i, carry)
    sr_ref[...], si_ref[...] = carry
    sp_ref[half:, :] = v_ref[half:, :].astype(BF16)
    for t in range(L):
        outputs(half, t)


def _s5_core(u_v, ops, d_skip, *, batch, seq):
    lamc, lamr, bt, ct = ops
    L = SCAN_CHUNK
    cw = MXU_WIDTH
    ncb = u_v.shape[0]
    rows_per_b = seq // L
    halves = rows_per_b // SCAN_ROWS
    ns = lamr.shape[2]
    blk = pl.BlockSpec((1, SCAN_ROWS, L * cw), lambda cb, b, h: (cb, b * halves + h, 0))

    def per_block(a):
        zeros = (0,) * (a.ndim - 1)
        return pl.BlockSpec((1,) + a.shape[1:], lambda cb, b, h: (cb,) + zeros)

    return pl.pallas_call(
        _s5_kernel,
        grid=(ncb, batch, halves),
        in_specs=[blk, per_block(lamc), per_block(lamr), per_block(bt), per_block(ct),
                  pl.BlockSpec((1, 1, cw), lambda cb, b, h: (cb, 0, 0))],
        out_specs=blk,
        out_shape=jax.ShapeDtypeStruct(u_v.shape, BF16),
        scratch_shapes=[pltpu.VMEM((L * cw, 2 * ns), BF16),
                        pltpu.VMEM((L * cw, cw), BF16),
                        pltpu.VMEM((L, 2 * ns, cw), BF16),
                        pltpu.VMEM((SUBLANES, ns), F32),
                        pltpu.VMEM((SUBLANES, ns), F32),
                        pltpu.VMEM((SCAN_ROWS, 2 * ns), F32),
                        pltpu.VMEM((SCAN_ROWS, 2 * ns), BF16),
                        pltpu.VMEM((1, ns), F32),
                        pltpu.VMEM((1, ns), F32)],
        compiler_params=_params("arbitrary", "arbitrary", "arbitrary"),
        name="s5_core",
    )(u_v, lamc, lamr, bt, ct, d_skip.reshape(ncb, 1, cw))


def _s5_param_layout(a_re, a_im, log_dt, b_re, b_im, c_re, c_im):
    groups, p = a_re.shape
    gl = MXU_WIDTH // SSM_GROUP
    ncb = groups // gl
    lam = jnp.stack([a_re, a_im, jnp.broadcast_to(log_dt[:, None], (groups, p))]).astype(F32)
    lamc = jnp.broadcast_to(lam[:, :, None, :], (3, groups, SSM_GROUP, p)).reshape(3, ncb, MXU_WIDTH, p)
    lamr = jnp.pad(lam.reshape(3, ncb, gl * p), ((0, SUBLANES - 3), (0, 0), (0, 0)))
    bt = jnp.stack([b_re, b_im]).astype(F32).transpose(0, 1, 3, 2).reshape(2, ncb, MXU_WIDTH, p)
    ct = jnp.stack([c_re, c_im]).astype(F32).reshape(2, ncb, MXU_WIDTH, p)

    def twice(a):
        return jnp.concatenate([a] * (LANES // p), axis=-1).transpose(1, 0, 2, 3)

    return twice(lamc), lamr.transpose(1, 0, 2), twice(bt), twice(ct)


def _bucket_tiles(t):
    i = np.arange(t)[:, None]
    j = np.arange(t)[None, :]
    tiles = []
    for off in (0, t):
        rel = j + off - i
        n = np.maximum(rel, 0)
        max_exact = NUM_BUCKETS // 2
        large = max_exact + (np.log(np.maximum(n, max_exact).astype(np.float32) / max_exact)
                             / math.log(MAX_DISTANCE / max_exact)
                             * (NUM_BUCKETS - max_exact)).astype(np.int32)
        large = np.minimum(large, NUM_BUCKETS - 1)
        b = np.where(n < max_exact, n, large)
        tiles.append(np.where(rel >= 0, b, -1).astype(np.int32))
    return np.stack(tiles)


def _bias_kernel(bkt_ref, tab_ref, o_ref):
    h = pl.program_id(0)
    far = tab_ref[NUM_BUCKETS - 1, h]
    for typ in range(2):
        b = bkt_ref[typ]
        acc = jnp.zeros(b.shape, F32)
        for i in range(NUM_BUCKETS - 1):
            acc = jnp.where(b == i, (tab_ref[i, h] - far) * LOG2E, acc)
        o_ref[0, typ] = jnp.where(b < 0, NEG_INF, acc)


def _bias_tiles(rel_bias, t):
    heads = rel_bias.shape[1]
    return pl.pallas_call(
        _bias_kernel,
        grid=(heads,),
        in_specs=[_const_spec((2, t, t)), pl.BlockSpec(memory_space=pltpu.SMEM)],
        out_specs=pl.BlockSpec((1, 2, t, t), lambda h: (h, 0, 0, 0)),
        out_shape=jax.ShapeDtypeStruct((heads, 2, t, t), F32),
        compiler_params=_params("parallel"),
        name="rel_bias_tiles",
    )(jnp.asarray(_bucket_tiles(t)), rel_bias.astype(F32))


def _flash_kernel(q_ref, *refs, lam_init):
    def query_tile(qi, carry):
        _flash_tile(qi, q_ref, *refs, lam_init=lam_init)
        return carry

    lax.fori_loop(0, q_ref.shape[1], query_tile, 0)


def _flash_tile(qi, q_ref, k_ref, vt_ref, bias_ref, lam_ref, hn_ref, o_ref,
                qcat_ref, sa_ref, sb_ref, m_ref, l_ref, acc_ref, *, lam_init):
    tq = q_ref.shape[3]
    tk = bias_ref.shape[2]
    cw = MXU_WIDTH
    per_branch = tq // cw
    ncol = 2 * per_branch
    per_key = tk // cw
    inner = tq // tk
    assert inner % 2 == 0, "the far-tile loop below relies on an odd number of far tiles"
    qt = q_ref[0, qi]
    row = lax.broadcasted_iota(jnp.int32, qt.shape, 0)
    zero = jnp.zeros_like(qt)
    qcat_ref[:, :tq] = jnp.where(row < HEAD_DIM, qt, zero)
    qcat_ref[:, tq:] = jnp.where(row >= HEAD_DIM, qt, zero)

    m_ref[...] = jnp.full(m_ref.shape, NEG_INF, F32)
    l_ref[...] = jnp.zeros_like(l_ref)
    acc_ref[...] = jnp.zeros_like(acc_ref)

    def bias_of(kind, c):
        if kind is None:
            return None
        cq = c % per_branch
        dist = cq // per_key - kind
        if dist < 0:
            return "skip"
        if dist > 1:
            return None
        return dist, (cq % per_key) * cw

    def scores_col(kblk, dst_ref, kind, c):
        spec = bias_of(kind, c)
        if spec == "skip":
            return None
        s = jnp.dot(kblk, qcat_ref[:, c * cw:(c + 1) * cw], preferred_element_type=F32)
        if spec is not None:
            s = s + bias_ref[0, spec[0], :, spec[1]:spec[1] + cw]
        dst_ref[c] = s
        return jnp.max(s, axis=0, keepdims=True)

    def softmax_pv_col(vblk, src_ref, cm, kind, c):
        if bias_of(kind, c) == "skip":
            return
        s = src_ref[c]
        m_old = m_ref[c]
        m_new = jnp.maximum(m_old, cm)
        alpha = jnp.exp2(m_old - m_new)
        p = jnp.exp2(s - m_new)
        l_ref[c] = alpha * l_ref[c] + jnp.sum(p, axis=0, keepdims=True)
        m_ref[c] = m_new
        acc_ref[c] = alpha * acc_ref[c] + jnp.dot(vblk, p.astype(BF16), preferred_element_type=F32)

    def key_tile(j):
        return k_ref[0, pl.ds(pl.multiple_of(j * tk, tk), tk), :]

    def scores(j, dst_ref, kind):
        kblk = key_tile(j)
        return tuple(scores_col(kblk, dst_ref, kind, c) for c in range(ncol))

    def step(nxt, cur, cmax):
        j, kind, src_ref = cur
        vblk = vt_ref[0, 0, j]
        kblk = key_tile(nxt[0]) if nxt is not None else None
        cm_next = []
        for c in range(ncol):
            if nxt is not None:
                cm_next.append(scores_col(kblk, nxt[2], nxt[1], c))
            softmax_pv_col(vblk, src_ref, cmax[c], kind, c)
        return tuple(cm_next)

    def run(tiles, first_max):
        bufs = (sa_ref, sb_ref)
        cmax = first_max
        for n, (j, kind) in enumerate(tiles):
            nxt = None
            if n + 1 < len(tiles):
                nxt = tiles[n + 1] + (bufs[(n + 1) % 2],)
            cmax = step(nxt, (j, kind, bufs[n % 2]), cmax)

    first_diag = qi * inner
    diag = [(first_diag + r, r) for r in range(inner)]

    @pl.when(qi == 0)
    def _():
        run(diag, scores(0, sa_ref, 0))

    @pl.when(qi >= 1)
    def _():
        n_far = first_diag - 1

        def pair(i, cm_a):
            j = 2 * i
            cm_b = step((j + 1, None, sb_ref), (j, None, sa_ref), cm_a)
            return step((j + 2, None, sa_ref), (j + 1, None, sb_ref), cm_b)

        cm_a = lax.fori_loop(0, (n_far - 1) // 2, pair, scores(0, sa_ref, None))
        run([(n_far - 1, None), (n_far, -1)] + diag, cm_a)

    lv = lam_ref[...]
    lam = (jnp.exp(jnp.sum(lv[0:1] * lv[1:2], axis=1, keepdims=True))
           - jnp.exp(jnp.sum(lv[2:3] * lv[3:4], axis=1, keepdims=True)) + lam_init)
    for c in range(per_branch):
        o1 = acc_ref[c] * (1.0 / l_ref[c])
        o2 = acc_ref[per_branch + c] * (1.0 / l_ref[per_branch + c])
        o = o1 - lam * o2
        o = o * lax.rsqrt(jnp.mean(o * o, axis=0, keepdims=True) + NORM_EPS)
        o = o * hn_ref[...] * (1.0 - lam_init)
        o_ref[0, pl.ds(pl.multiple_of(qi * tq + c * cw, cw), cw), :] = o.T.astype(BF16)


def _diff_attention(qt, k, vt, bias, lam_vecs, head_norm, *, lam_init):
    batch, nq, d, tq = qt.shape
    seq = nq * tq
    heads = d // V_DIM
    t = ATTN_TILE
    nblk = seq // t
    ncol = 2 * tq // MXU_WIDTH
    lam_pad = jnp.zeros((SUBLANES, V_DIM), F32).at[:4, :HEAD_DIM].set(lam_vecs.astype(F32))
    return pl.pallas_call(
        functools.partial(_flash_kernel, lam_init=lam_init),
        grid=(batch, heads),
        in_specs=[pl.BlockSpec((1, nq, V_DIM, tq), lambda b, h: (b, 0, h, 0)),
                  pl.BlockSpec((1, seq, V_DIM), lambda b, h: (b, 0, h)),
                  pl.BlockSpec((1, 1, nblk, V_DIM, t), lambda b, h: (b, h, 0, 0, 0)),
                  pl.BlockSpec((1, 2, t, t), lambda b, h: (h, 0, 0, 0)),
                  _const_spec((SUBLANES, V_DIM)),
                  _const_spec((V_DIM, 1))],
        out_specs=pl.BlockSpec((1, seq, V_DIM), lambda b, h: (b, 0, h)),
        out_shape=jax.ShapeDtypeStruct((batch, seq, d), BF16),
        scratch_shapes=[pltpu.VMEM((V_DIM, 2 * tq), BF16),
                        pltpu.VMEM((ncol, t, MXU_WIDTH), F32),
                        pltpu.VMEM((ncol, t, MXU_WIDTH), F32),
                        pltpu.VMEM((ncol, 1, MXU_WIDTH), F32),
                        pltpu.VMEM((ncol, 1, MXU_WIDTH), F32),
                        pltpu.VMEM((ncol, V_DIM, MXU_WIDTH), F32)],
        compiler_params=_params("parallel", "parallel"),
        name="diff_attention",
    )(qt, k.reshape(batch, seq, d), vt, bias, lam_pad, head_norm.astype(F32).reshape(V_DIM, 1))


def kernel(x, norm_mixer_pre, norm_mixer_post, norm_mlp_pre, norm_mlp_post, mlp_w_up, mlp_w_down,
           ssm_w_in, ssm_a_re, ssm_a_im, ssm_log_dt, ssm_b_re, ssm_b_im, ssm_c_re, ssm_c_im, ssm_d,
           ssm_w_glu, kv_norm, w_kv, attn_w_q, attn_lambda_q1, attn_lambda_k1, attn_lambda_q2,
           attn_lambda_k2, attn_head_norm, attn_w_o, rel_bias):
    batch, seq, d = x.shape
    depth = norm_mixer_pre.shape[0]
    n_ssm = ssm_w_in.shape[0]
    assert 0 < n_ssm < depth
    xs = x.reshape(batch * seq, d).astype(F32)
    q_scale = HEAD_DIM ** -0.5 * LOG2E
    w_up, w_down, w_in, w_glu, w_q, w_o = (
        w.astype(BF16) for w in (mlp_w_up, mlp_w_down, ssm_w_in, ssm_w_glu, attn_w_q, attn_w_o))

    def mixer_projections(layer):
        if layer >= depth:
            return []
        if layer < n_ssm:
            return [("colblock", norm_mixer_pre[layer], (w_in, layer), 1.0)]
        projs = []
        if layer == n_ssm:
            projs.append(("kv", kv_norm, w_kv.astype(BF16), 1.0))
        projs.append(("qT", norm_mixer_pre[layer], (w_q, layer - n_ssm), q_scale))
        return projs

    feeds = [_norm_proj(xs, norm_mixer_pre[0], w_in[0], mode="colblock", batch=batch, seq=seq)]
    bias = _bias_tiles(rel_bias, ATTN_TILE)
    k = vt = None
    for layer in range(depth):
        if layer < n_ssm:
            i = layer
            ops = _s5_param_layout(ssm_a_re[i], ssm_a_im[i], ssm_log_dt[i], ssm_b_re[i], ssm_b_im[i],
                                ssm_c_re[i], ssm_c_im[i])
            y = _s5_core(feeds[0], ops, ssm_d[i].astype(F32), batch=batch, seq=seq)
            w_out = (w_glu, i)
        else:
            if layer == n_ssm:
                k, vt = feeds[0], feeds[1]
            j = layer - n_ssm
            lam_init = 0.8 - 0.6 * math.exp(-0.3 * layer)
            lam_vecs = jnp.stack([attn_lambda_q1[j], attn_lambda_k1[j],
                                  attn_lambda_q2[j], attn_lambda_k2[j]])
            y = _diff_attention(feeds[-1], k, vt, bias, lam_vecs, attn_head_norm[j], lam_init=lam_init)
            y = y.reshape(batch * seq, d)
            w_out = (w_o, j)
        xs, feeds = _layer_tail(y, w_out, xs, norm_mixer_post[layer], norm_mlp_pre[layer],
                                (w_up, layer), (w_down, layer),
                                norm_mlp_post[layer], mixer_projections(layer + 1),
                                glu=layer < n_ssm, chunked=layer < n_ssm, batch=batch, seq=seq)
    return xs.reshape(batch, seq, d).astype(x.dtype)
```

```python
import functools
import math

import numpy as np
import jax
import jax.numpy as jnp
from jax import lax
from jax.experimental import pallas as pl
from jax.experimental.pallas import tpu as pltpu

F32 = jnp.float32
BF16 = jnp.bfloat16

SSM_GROUP = 16
SSM_STATE = 64
HEAD_DIM = 64
V_DIM = 2 * HEAD_DIM
NUM_BUCKETS = 32
MAX_DISTANCE = 128
NORM_EPS = 1e-6
NEG_INF = -1e30
LOG2E = math.log2(math.e)

MXU_WIDTH = 256
LANES = 128
SUBLANES = 8
ROW_TILE = 512
SCAN_CHUNK = 8
SCAN_ROWS = 512
ATTN_TILE = 512
ATTN_QTILE = 2048
FF_CHUNK = 1024
TAIL_SPLIT = 2
VMEM_LIMIT = 56 * 1024 * 1024


def _params(*sem):
    return pltpu.CompilerParams(dimension_semantics=sem, vmem_limit_bytes=VMEM_LIMIT)


def _rms(xf, gain):
    return xf * lax.rsqrt(jnp.mean(xf * xf, axis=-1, keepdims=True) + NORM_EPS) * gain


def _const_spec(shape):
    zeros = (0,) * len(shape)
    return pl.BlockSpec(shape, lambda *_: zeros, pipeline_mode=pl.Buffered(1))


def _project(x, g_ref, w_ref, out_refs, slab_ref, mode, scale, r0=0):
    hn = _rms(x, g_ref[...]).astype(BF16)
    acc = jnp.dot(hn, w_ref[...], preferred_element_type=F32)
    rows = acc.shape[0]
    if mode == "colblock":
        (o_ref,) = out_refs
        L = SCAN_CHUNK
        for sl in range(slab_ref.shape[0]):
            slab_ref[sl, r0:r0 + rows, :] = acc[:, sl * LANES:(sl + 1) * LANES]
        for cb in range(o_ref.shape[0]):
            for t in range(L):
                for hf in range(MXU_WIDTH // LANES):
                    lo = t * MXU_WIDTH + hf * LANES
                    piece = slab_ref[cb * (MXU_WIDTH // LANES) + hf, pl.ds(r0 + t, rows // L, stride=L), :]
                    o_ref[cb, r0 // L:(r0 + rows) // L, lo:lo + LANES] = piece.astype(BF16)
    elif mode == "qT":
        (o_ref,) = out_refs
        o_ref[0, 0, :, r0:r0 + rows] = (acc * scale).T.astype(BF16)
    else:
        k_ref, vt_ref = out_refs
        d = k_ref.shape[1]
        k_ref[r0:r0 + rows, :] = acc[:, :d].astype(BF16)
        vt = acc[:, d:].T.astype(BF16)
        tk = vt_ref.shape[4]
        step = min(rows, tk)
        for n in range(rows // step):
            at = r0 + n * step
            vt_ref[0, :, at // tk, :, at % tk:at % tk + step] = (
                vt[:, n * step:(n + 1) * step].reshape(d // V_DIM, V_DIM, step))


def _proj_outputs(mode, n, d, nout, batch, seq):
    tm = ROW_TILE
    per_b = seq // tm
    if mode == "colblock":
        ncb = nout // MXU_WIDTH
        L = SCAN_CHUNK
        return ([jax.ShapeDtypeStruct((ncb, n // L, L * MXU_WIDTH), BF16)],
                [pl.BlockSpec((ncb, tm // L, L * MXU_WIDTH), lambda i: (0, i, 0))])
    if mode == "qT":
        tq = ATTN_QTILE
        per_q = tq // tm
        return ([jax.ShapeDtypeStruct((batch, seq // tq, nout, tq), BF16)],
                [pl.BlockSpec((1, 1, nout, tm),
                              lambda i: (i // per_b, (i % per_b) // per_q, 0, (i % per_b) % per_q))])
    heads = d // V_DIM
    tk = ATTN_TILE
    assert mode == "kv" and tm % tk == 0
    return ([jax.ShapeDtypeStruct((n, d), BF16),
             jax.ShapeDtypeStruct((batch, heads, seq // tk, V_DIM, tk), BF16)],
            [pl.BlockSpec((tm, d), lambda i: (i, 0)),
             pl.BlockSpec((1, heads, tm // tk, V_DIM, tk), lambda i: (i // per_b, 0, i % per_b, 0, 0))])


def _slab_scratch(d):
    return [pltpu.VMEM((d // LANES, ROW_TILE, LANES), F32), pltpu.VMEM((ROW_TILE, d), BF16)]


def _proj_kernel(x_ref, g_ref, w_ref, *refs, mode, scale):
    slab_ref = refs[-2] if mode == "colblock" else None
    nout = 2 if mode == "kv" else 1
    _project(x_ref[...], g_ref, w_ref, refs[:nout], slab_ref, mode, scale)


def _norm_proj(x, gain, w, *, mode, batch, seq, scale=1.0):
    n, d = x.shape
    nout = w.shape[1]
    tm = ROW_TILE
    out_shape, out_specs = _proj_outputs(mode, n, d, nout, batch, seq)
    out = pl.pallas_call(
        functools.partial(_proj_kernel, mode=mode, scale=scale),
        grid=(n // tm,),
        in_specs=[pl.BlockSpec((tm, d), lambda i: (i, 0)),
                  _const_spec((1, d)),
                  _const_spec((d, nout))],
        out_specs=out_specs,
        out_shape=out_shape,
        scratch_shapes=_slab_scratch(nout) if mode == "colblock" else [],
        compiler_params=_params("parallel"),
        name="norm_proj_" + mode,
    )(x, gain.reshape(1, d), w)
    return out[0] if len(out) == 1 else out


def _chunk_rows_to_tokens(y_ref, slab_ref, tok_ref):
    L = SCAN_CHUNK
    nchunk = y_ref.shape[1]
    for cb in range(y_ref.shape[0]):
        for t in range(L):
            for hf in range(MXU_WIDTH // LANES):
                lo = t * MXU_WIDTH + hf * LANES
                slab_ref[cb * (MXU_WIDTH // LANES) + hf, pl.ds(t, nchunk, stride=L), :] = (
                    y_ref[cb, :, lo:lo + LANES].astype(F32))
    for sl in range(slab_ref.shape[0]):
        tok_ref[:, sl * LANES:(sl + 1) * LANES] = slab_ref[sl].astype(BF16)


def _tail_kernel(*refs, glu, chunked, projs, use_slab):
    n_in = 8 + 2 * len(projs)
    y_ref, wo_ref, x_ref, gpost_ref, gpre_ref, wu_ref, wd_ref, gmlp_ref = refs[:8]
    proj_in = refs[8:n_in]
    n_out = 1 + sum(2 if mode == "kv" else 1 for mode, _ in projs)
    o_ref = refs[n_in]
    proj_out = refs[n_in + 1:n_in + n_out]
    slab_ref, tok_ref = refs[n_in + n_out:] if use_slab else (None, None)
    tm, d = o_ref.shape
    rows = tm // TAIL_SPLIT
    if chunked:
        _chunk_rows_to_tokens(y_ref, slab_ref, tok_ref)
    y_tok = tok_ref if chunked else y_ref

    def stages(r0):
        acc = jnp.dot(y_tok[r0:r0 + rows, :], wo_ref[...], preferred_element_type=F32)
        if glu:
            acc = acc[:, :d] * jax.nn.sigmoid(acc[:, d:])
        yield
        x = x_ref[r0:r0 + rows, :] + _rms(acc, gpost_ref[...])
        hn = _rms(x, gpre_ref[...]).astype(BF16)
        yield
        acc = None
        for c in range(wu_ref.shape[1] // FF_CHUNK):
            a = jnp.dot(hn, wu_ref[:, c * FF_CHUNK:(c + 1) * FF_CHUNK], preferred_element_type=F32)
            a = jnp.maximum(a, 0.0)
            a = (a * a).astype(BF16)
            part = jnp.dot(a, wd_ref[c * FF_CHUNK:(c + 1) * FF_CHUNK, :], preferred_element_type=F32)
            acc = part if acc is None else acc + part
            yield
        x = x + _rms(acc, gmlp_ref[...])
        o_ref[r0:r0 + rows, :] = x
        k = 0
        for i, (mode, scale) in enumerate(projs):
            yield
            cnt = 2 if mode == "kv" else 1
            _project(x, proj_in[2 * i], proj_in[2 * i + 1], proj_out[k:k + cnt], slab_ref, mode, scale, r0)
            k += cnt

    active = [stages(g * rows) for g in range(TAIL_SPLIT)]
    while active:
        for gen in list(active):
            if next(gen, "done") == "done":
                active.remove(gen)


def _layer_weight(w):
    if isinstance(w, tuple):
        stack, layer = w
        return stack, pl.BlockSpec((None,) + stack.shape[1:], lambda *_: (layer, 0, 0),
                                   pipeline_mode=pl.Buffered(1))
    return w, _const_spec(w.shape)


def _layer_tail(y, w_out, x, g_post, g_pre, w_up, w_down, g_mlp, projs, *, glu, chunked, batch, seq):
    n, d = x.shape
    tm = ROW_TILE
    row = pl.BlockSpec((tm, d), lambda i: (i, 0))
    if chunked:
        ncb, _, width = y.shape
        y_spec = pl.BlockSpec((ncb, tm // SCAN_CHUNK, width), lambda i: (0, i, 0))
    else:
        y_spec = pl.BlockSpec((tm, y.shape[1]), lambda i: (i, 0))
    (w_out, wo_spec), (w_up, wu_spec), (w_down, wd_spec) = map(_layer_weight, (w_out, w_up, w_down))
    in_specs = [y_spec, wo_spec, row, _const_spec((1, d)), _const_spec((1, d)),
                wu_spec, wd_spec, _const_spec((1, d))]
    args = [y, w_out, x, g_post.reshape(1, d), g_pre.reshape(1, d), w_up, w_down, g_mlp.reshape(1, d)]
    out_shape = [jax.ShapeDtypeStruct((n, d), F32)]
    out_specs = [row]
    for mode, gain, w, _ in projs:
        w, w_spec = _layer_weight(w)
        in_specs += [_const_spec((1, d)), w_spec]
        args += [gain.reshape(1, d), w]
        shapes, specs = _proj_outputs(mode, n, d, w.shape[-1], batch, seq)
        out_shape += shapes
        out_specs += specs
    use_slab = chunked or any(mode == "colblock" for mode, *_ in projs)
    out = pl.pallas_call(
        functools.partial(_tail_kernel, glu=glu, chunked=chunked,
                          projs=tuple((mode, scale) for mode, _, _, scale in projs), use_slab=use_slab),
        grid=(n // tm,),
        in_specs=in_specs,
        out_specs=out_specs,
        out_shape=out_shape,
        scratch_shapes=_slab_scratch(d) if use_slab else [],
        compiler_params=_params("parallel"),
        name="layer_tail",
    )(*args)
    return out[0], list(out[1:])


def _cmul(a, b):
    return a[0] * b[0] - a[1] * b[1], a[0] * b[1] + a[1] * b[0]


def _zoh(lr, li, log_dt):
    dt = jnp.exp(log_dt)
    mag = jnp.exp(lr * dt)
    ab = (mag * jnp.cos(li * dt), mag * jnp.sin(li * dt))
    den = lr * lr + li * li
    coef = (((ab[0] - 1.0) * lr + ab[1] * li) / den, (ab[1] * lr - (ab[0] - 1.0) * li) / den)
    return ab, coef


def _s5_kernel(u_ref, lamc_ref, lamr_ref, bt_ref, ct_ref, d_ref, o_ref,
               w_ref, k_ref, p_ref, apr_ref, api_ref, v_ref, sp_ref, sr_ref, si_ref):
    cw = MXU_WIDTH
    ns = v_ref.shape[1] // 2
    L = u_ref.shape[2] // cw
    grp_shift = SSM_GROUP.bit_length() - 1
    st_shift = SSM_STATE.bit_length() - 1

    @pl.when(jnp.logical_and(pl.program_id(1) == 0, pl.program_id(2) == 0))
    def _():
        def iota(shape, dim):
            return lax.broadcasted_iota(jnp.int32, shape, dim)

        def nt_dot(a, b):
            return lax.dot_general(a, b, (((1,), (1,)), ((), ())), precision=lax.Precision.HIGHEST,
                                   preferred_element_type=F32)

        wmask = (iota((cw, ns), 0) >> grp_shift) == (iota((cw, ns), 1) >> st_shift)
        pmask = (iota((ns, cw), 0) >> st_shift) == (iota((ns, cw), 1) >> grp_shift)
        kmask = (iota((cw, cw), 0) >> grp_shift) == (iota((cw, cw), 1) >> grp_shift)
        ab, coef = _zoh(lamc_ref[0, 0], lamc_ref[0, 1], lamc_ref[0, 2])
        bb = _cmul(coef, (bt_ref[0, 0], bt_ref[0, 1]))
        ct = (ct_ref[0, 0], ct_ref[0, 1])
        pw = [(jnp.ones_like(ab[0]), jnp.zeros_like(ab[0]))]
        for _ in range(L):
            pw.append(_cmul(pw[-1], ab))
        for t in range(L):
            w = _cmul(pw[L - 1 - t], bb)
            for h in range(2):
                tiled = jnp.concatenate([w[h]] * (ns // LANES), axis=1)
                w_ref[t * cw:(t + 1) * cw, h * ns:(h + 1) * ns] = (
                    jnp.where(wmask, tiled, 0.0).astype(BF16))
            km = 0.5 * (nt_dot(w[0], ct[0]) - nt_dot(w[1], ct[1]))
            k_ref[t * cw:(t + 1) * cw, :] = jnp.where(kmask, km, 0.0).astype(BF16)
            q = _cmul(ct, pw[t + 1])
            for h, qh in enumerate((q[0], -q[1])):
                piece = qh.T[:SSM_STATE, :]
                tiled = jnp.concatenate([piece] * (ns // SSM_STATE), axis=0)
                p_ref[t, h * ns:(h + 1) * ns, :] = jnp.where(pmask, tiled, 0.0).astype(BF16)
        a, _ = _zoh(lamr_ref[0, 0:1], lamr_ref[0, 1:2], lamr_ref[0, 2:3])
        al = a
        for _ in range(L - 1):
            al = _cmul(al, a)
        q = al
        for r in range(SUBLANES):
            apr_ref[r:r + 1, :] = q[0]
            api_ref[r:r + 1, :] = q[1]
            q = _cmul(q, al)

    @pl.when(pl.program_id(2) == 0)
    def _():
        sr_ref[...] = jnp.zeros_like(sr_ref)
        si_ref[...] = jnp.zeros_like(si_ref)

    rows = v_ref.shape[0]
    half = rows // 2
    apr = apr_ref[...]
    api = api_ref[...]
    row = lax.broadcasted_iota(jnp.int32, (SUBLANES, ns), 0)

    def shift(z, k, fill):
        return jnp.where(row >= k, pltpu.roll(z, k, 0), fill)

    def scan_block(i, carry):
        sr, si = carry
        lo, hi_row = SUBLANES * i, SUBLANES * (i + 1)
        zr = v_ref[lo:hi_row, 0:ns]
        zi = v_ref[lo:hi_row, ns:2 * ns]
        k = 1
        while k < SUBLANES:
            ar, ai = apr[k - 1:k], api[k - 1:k]
            hr, hi = shift(zr, k, 0.0), shift(zi, k, 0.0)
            zr, zi = zr + ar * hr - ai * hi, zi + ar * hi + ai * hr
            k *= 2
        fr = zr + apr * sr - api * si
        fi = zi + apr * si + api * sr
        v_ref[lo:hi_row, 0:ns] = shift(fr, 1, sr)
        v_ref[lo:hi_row, ns:2 * ns] = shift(fi, 1, si)
        return fr[SUBLANES - 1:], fi[SUBLANES - 1:]

    def outputs(lo, t):
        xs = u_ref[0, lo:lo + half, :]
        acc = jnp.dot(xs[:, :(t + 1) * cw], k_ref[(L - 1 - t) * cw:, :], preferred_element_type=F32)
        acc = acc + jnp.dot(sp_ref[lo:lo + half, :], p_ref[t], preferred_element_type=F32)
        y = acc + d_ref[0] * xs[:, t * cw:(t + 1) * cw].astype(F32)
        o_ref[0, lo:lo + half, t * cw:(t + 1) * cw] = jax.nn.gelu(y).astype(BF16)

    nblk_half = half // SUBLANES
    pieces = 8
    ncols = 2 * ns // pieces
    v_ref[:half, :] = jnp.dot(u_ref[0, :half, :], w_ref[...], preferred_element_type=F32)
    carry = (sr_ref[...], si_ref[...])
    for n in range(pieces):
        v_ref[half:, n * ncols:(n + 1) * ncols] = jnp.dot(
            u_ref[0, half:, :], w_ref[:, n * ncols:(n + 1) * ncols], preferred_element_type=F32)
        for i in range(n * nblk_half // pieces, (n + 1) * nblk_half // pieces):
            carry = scan_block(i, carry)
    sp_ref[:half, :] = v_ref[:half, :].astype(BF16)
    for t in range(L):
        outputs(0, t)
        for i in range(t * nblk_half // L, (t + 1) * nblk_half // L):
            carry = scan_block(nblk_half + i, carry)
    sr_ref[...], si_ref[...] = carry
    sp_ref[half:, :] = v_ref[half:, :].astype(BF16)
    for t in range(L):
        outputs(half, t)


def _s5_core(u_v, ops, d_skip, *, batch, seq):
    lamc, lamr, bt, ct = ops
    L = SCAN_CHUNK
    cw = MXU_WIDTH
    ncb = u_v.shape[0]
    rows_per_b = seq // L
    halves = rows_per_b // SCAN_ROWS
    ns = lamr.shape[2]
    blk = pl.BlockSpec((1, SCAN_ROWS, L * cw), lambda cb, b, h: (cb, b * halves + h, 0))

    def per_block(a):
        zeros = (0,) * (a.ndim - 1)
        return pl.BlockSpec((1,) + a.shape[1:], lambda cb, b, h: (cb,) + zeros)

    return pl.pallas_call(
        _s5_kernel,
        grid=(ncb, batch, halves),
        in_specs=[blk, per_block(lamc), per_block(lamr), per_block(bt), per_block(ct),
                  pl.BlockSpec((1, 1, cw), lambda cb, b, h: (cb, 0, 0))],
        out_specs=blk,
        out_shape=jax.ShapeDtypeStruct(u_v.shape, BF16),
        scratch_shapes=[pltpu.VMEM((L * cw, 2 * ns), BF16),
                        pltpu.VMEM((L * cw, cw), BF16),
                        pltpu.VMEM((L, 2 * ns, cw), BF16),
                        pltpu.VMEM((SUBLANES, ns), F32),
                        pltpu.VMEM((SUBLANES, ns), F32),
                        pltpu.VMEM((SCAN_ROWS, 2 * ns), F32),
                        pltpu.VMEM((SCAN_ROWS, 2 * ns), BF16),
                        pltpu.VMEM((1, ns), F32),
                        pltpu.VMEM((1, ns), F32)],
        compiler_params=_params("arbitrary", "arbitrary", "arbitrary"),
        name="s5_core",
    )(u_v, lamc, lamr, bt, ct, d_skip.reshape(ncb, 1, cw))


def _s5_param_layout(a_re, a_im, log_dt, b_re, b_im, c_re, c_im):
    groups, p = a_re.shape
    gl = MXU_WIDTH // SSM_GROUP
    ncb = groups // gl
    lam = jnp.stack([a_re, a_im, jnp.broadcast_to(log_dt[:, None], (groups, p))]).astype(F32)
    lamc = jnp.broadcast_to(lam[:, :, None, :], (3, groups, SSM_GROUP, p)).reshape(3, ncb, MXU_WIDTH, p)
    lamr = jnp.pad(lam.reshape(3, ncb, gl * p), ((0, SUBLANES - 3), (0, 0), (0, 0)))
    bt = jnp.stack([b_re, b_im]).astype(F32).transpose(0, 1, 3, 2).reshape(2, ncb, MXU_WIDTH, p)
    ct = jnp.stack([c_re, c_im]).astype(F32).reshape(2, ncb, MXU_WIDTH, p)

    def twice(a):
        return jnp.concatenate([a] * (LANES // p), axis=-1).transpose(1, 0, 2, 3)

    return twice(lamc), lamr.transpose(1, 0, 2), twice(bt), twice(ct)


def _bucket_tiles(t):
    i = np.arange(t)[:, None]
    j = np.arange(t)[None, :]
    tiles = []
    for off in (0, t):
        rel = j + off - i
        n = np.maximum(rel, 0)
        max_exact = NUM_BUCKETS // 2
        large = max_exact + (np.log(np.maximum(n, max_exact).astype(np.float32) / max_exact)
                             / math.log(MAX_DISTANCE / max_exact)
                             * (NUM_BUCKETS - max_exact)).astype(np.int32)
        large = np.minimum(large, NUM_BUCKETS - 1)
        b = np.where(n < max_exact, n, large)
        tiles.append(np.where(rel >= 0, b, -1).astype(np.int32))
    return np.stack(tiles)


def _bias_kernel(bkt_ref, tab_ref, o_ref):
    h = pl.program_id(0)
    far = tab_ref[NUM_BUCKETS - 1, h]
    for typ in range(2):
        b = bkt_ref[typ]
        acc = jnp.zeros(b.shape, F32)
        for i in range(NUM_BUCKETS - 1):
            acc = jnp.where(b == i, (tab_ref[i, h] - far) * LOG2E, acc)
        o_ref[0, typ] = jnp.where(b < 0, NEG_INF, acc)


def _bias_tiles(rel_bias, t):
    heads = rel_bias.shape[1]
    return pl.pallas_call(
        _bias_kernel,
        grid=(heads,),
        in_specs=[_const_spec((2, t, t)), pl.BlockSpec(memory_space=pltpu.SMEM)],
        out_specs=pl.BlockSpec((1, 2, t, t), lambda h: (h, 0, 0, 0)),
        out_shape=jax.ShapeDtypeStruct((heads, 2, t, t), F32),
        compiler_params=_params("parallel"),
        name="rel_bias_tiles",
    )(jnp.asarray(_bucket_tiles(t)), rel_bias.astype(F32))


def _flash_kernel(q_ref, *refs, lam_init):
    def query_tile(qi, carry):
        _flash_tile(qi, q_ref, *refs, lam_init=lam_init)
        return carry

    lax.fori_loop(0, q_ref.shape[1], query_tile, 0)


def _flash_tile(qi, q_ref, k_ref, vt_ref, bias_ref, lam_ref, hn_ref, o_ref,
                qcat_ref, sa_ref, sb_ref, m_ref, l_ref, acc_ref, *, lam_init):
    tq = q_ref.shape[3]
    tk = bias_ref.shape[2]
    cw = MXU_WIDTH
    per_branch = tq // cw
    ncol = 2 * per_branch
    per_key = tk // cw
    inner = tq // tk
    assert inner % 2 == 0, "the far-tile loop below relies on an odd number of far tiles"
    qt = q_ref[0, qi]
    row = lax.broadcasted_iota(jnp.int32, qt.shape, 0)
    zero = jnp.zeros_like(qt)
    qcat_ref[:, :tq] = jnp.where(row < HEAD_DIM, qt, zero)
    qcat_ref[:, tq:] = jnp.where(row >= HEAD_DIM, qt, zero)

    m_ref[...] = jnp.full(m_ref.shape, NEG_INF, F32)
    l_ref[...] = jnp.zeros_like(l_ref)
    acc_ref[...] = jnp.zeros_like(acc_ref)

    def bias_of(kind, c):
        if kind is None:
            return None
        cq = c % per_branch
        dist = cq // per_key - kind
        if dist < 0:
            return "skip"
        if dist > 1:
            return None
        return dist, (cq % per_key) * cw

    def scores_col(kblk, dst_ref, kind, c):
        spec = bias_of(kind, c)
        if spec == "skip":
            return None
        s = jnp.dot(kblk, qcat_ref[:, c * cw:(c + 1) * cw], preferred_element_type=F32)
        if spec is not None:
            s = s + bias_ref[0, spec[0], :, spec[1]:spec[1] + cw]
        dst_ref[c] = s
        return jnp.max(s, axis=0, keepdims=True)

    def softmax_pv_col(vblk, src_ref, cm, kind, c):
        if bias_of(kind, c) == "skip":
            return
        s = src_ref[c]
        m_old = m_ref[c]
        m_new = jnp.maximum(m_old, cm)
        alpha = jnp.exp2(m_old - m_new)
        p = jnp.exp2(s - m_new)
        l_ref[c] = alpha * l_ref[c] + jnp.sum(p, axis=0, keepdims=True)
        m_ref[c] = m_new
        acc_ref[c] = alpha * acc_ref[c] + jnp.dot(vblk, p.astype(BF16), preferred_element_type=F32)

    def key_tile(j):
        return k_ref[0, pl.ds(pl.multiple_of(j * tk, tk), tk), :]

    def scores(j, dst_ref, kind):
        kblk = key_tile(j)
        return tuple(scores_col(kblk, dst_ref, kind, c) for c in range(ncol))

    def step(nxt, cur, cmax):
        j, kind, src_ref = cur
        vblk = vt_ref[0, 0, j]
        kblk = key_tile(nxt[0]) if nxt is not None else None
        cm_next = []
        for c in range(ncol):
            if nxt is not None:
                cm_next.append(scores_col(kblk, nxt[2], nxt[1], c))
            softmax_pv_col(vblk, src_ref, cmax[c], kind, c)
        return tuple(cm_next)

    def run(tiles, first_max):
        bufs = (sa_ref, sb_ref)
        cmax = first_max
        for n, (j, kind) in enumerate(tiles):
            nxt = None
            if n + 1 < len(tiles):
                nxt = tiles[n + 1] + (bufs[(n + 1) % 2],)
            cmax = step(nxt, (j, kind, bufs[n % 2]), cmax)

    first_diag = qi * inner
    diag = [(first_diag + r, r) for r in range(inner)]

    @pl.when(qi == 0)
    def _():
        run(diag, scores(0, sa_ref, 0))

    @pl.when(qi >= 1)
    def _():
        n_far = first_diag - 1

        def pair(i, cm_a):
            j = 2 * i
            cm_b = step((j + 1, None, sb_ref), (j, None, sa_ref), cm_a)
            return step((j + 2, None, sa_ref), (j + 1, None, sb_ref), cm_b)

        cm_a = lax.fori_loop(0, (n_far - 1) // 2, pair, scores(0, sa_ref, None))
        run([(n_far - 1, None), (n_far, -1)] + diag, cm_a)

    lv = lam_ref[...]
    lam = (jnp.exp(jnp.sum(lv[0:1] * lv[1:2], axis=1, keepdims=True))
           - jnp.exp(jnp.sum(lv[2:3] * lv[3:4], axis=1, keepdims=True)) + lam_init)
    for c in range(per_branch):
        o1 = acc_ref[c] * (1.0 / l_ref[c])
        o2 = acc_ref[per_branch + c] * (1.0 / l_ref[per_branch + c])
        o = o1 - lam * o2
        o = o * lax.rsqrt(jnp.mean(o * o, axis=0, keepdims=True) + NORM_EPS)
        o = o * hn_ref[...] * (1.0 - lam_init)
        o_ref[0, pl.ds(pl.multiple_of(qi * tq + c * cw, cw), cw), :] = o.T.astype(BF16)


def _diff_attention(qt, k, vt, bias, lam_vecs, head_norm, *, lam_init):
    batch, nq, d, tq = qt.shape
    seq = nq * tq
    heads = d // V_DIM
    t = ATTN_TILE
    nblk = seq // t
    ncol = 2 * tq // MXU_WIDTH
    lam_pad = jnp.zeros((SUBLANES, V_DIM), F32).at[:4, :HEAD_DIM].set(lam_vecs.astype(F32))
    return pl.pallas_call(
        functools.partial(_flash_kernel, lam_init=lam_init),
        grid=(batch, heads),
        in_specs=[pl.BlockSpec((1, nq, V_DIM, tq), lambda b, h: (b, 0, h, 0)),
                  pl.BlockSpec((1, seq, V_DIM), lambda b, h: (b, 0, h)),
                  pl.BlockSpec((1, 1, nblk, V_DIM, t), lambda b, h: (b, h, 0, 0, 0)),
                  pl.BlockSpec((1, 2, t, t), lambda b, h: (h, 0, 0, 0)),
                  _const_spec((SUBLANES, V_DIM)),
                  _const_spec((V_DIM, 1))],
        out_specs=pl.BlockSpec((1, seq, V_DIM), lambda b, h: (b, 0, h)),
        out_shape=jax.ShapeDtypeStruct((batch, seq, d), BF16),
        scratch_shapes=[pltpu.VMEM((V_DIM, 2 * tq), BF16),
                        pltpu.VMEM((ncol, t, MXU_WIDTH), F32),
                        pltpu.VMEM((ncol, t, MXU_WIDTH), F32),
                        pltpu.VMEM((ncol, 1, MXU_WIDTH), F32),
                        pltpu.VMEM((ncol, 1, MXU_WIDTH), F32),
                        pltpu.VMEM((ncol, V_DIM, MXU_WIDTH), F32)],
        compiler_params=_params("parallel", "parallel"),
        name="diff_attention",
    )(qt, k.reshape(batch, seq, d), vt, bias, lam_pad, head_norm.astype(F32).reshape(V_DIM, 1))


def kernel(x, norm_mixer_pre, norm_mixer_post, norm_mlp_pre, norm_mlp_post, mlp_w_up, mlp_w_down,
           ssm_w_in, ssm_a_re, ssm_a_im, ssm_log_dt, ssm_b_re, ssm_b_im, ssm_c_re, ssm_c_im, ssm_d,
           ssm_w_glu, kv_norm, w_kv, attn_w_q, attn_lambda_q1, attn_lambda_k1, attn_lambda_q2,
           attn_lambda_k2, attn_head_norm, attn_w_o, rel_bias):
    batch, seq, d = x.shape
    depth = norm_mixer_pre.shape[0]
    n_ssm = ssm_w_in.shape[0]
    assert 0 < n_ssm < depth
    xs = x.reshape(batch * seq, d).astype(F32)
    q_scale = HEAD_DIM ** -0.5 * LOG2E
    w_up, w_down, w_in, w_glu, w_q, w_o = (
        w.astype(BF16) for w in (mlp_w_up, mlp_w_down, ssm_w_in, ssm_w_glu, attn_w_q, attn_w_o))

    def mixer_projections(layer):
        if layer >= depth:
            return []
        if layer < n_ssm:
            return [("colblock", norm_mixer_pre[layer], (w_in, layer), 1.0)]
        projs = []
        if layer == n_ssm:
            projs.append(("kv", kv_norm, w_kv.astype(BF16), 1.0))
        projs.append(("qT", norm_mixer_pre[layer], (w_q, layer - n_ssm), q_scale))
        return projs

    feeds = [_norm_proj(xs, norm_mixer_pre[0], w_in[0], mode="colblock", batch=batch, seq=seq)]
    bias = _bias_tiles(rel_bias, ATTN_TILE)
    k = vt = None
    for layer in range(depth):
        if layer < n_ssm:
            i = layer
            ops = _s5_param_layout(ssm_a_re[i], ssm_a_im[i], ssm_log_dt[i], ssm_b_re[i], ssm_b_im[i],
                                ssm_c_re[i], ssm_c_im[i])
            y = _s5_core(feeds[0], ops, ssm_d[i].astype(F32), batch=batch, seq=seq)
            w_out = (w_glu, i)
        else:
            if layer == n_ssm:
                k, vt = feeds[0], feeds[1]
            j = layer - n_ssm
            lam_init = 0.8 - 0.6 * math.exp(-0.3 * layer)
            lam_vecs = jnp.stack([attn_lambda_q1[j], attn_lambda_k1[j],
                                  attn_lambda_q2[j], attn_lambda_k2[j]])
            y = _diff_attention(feeds[-1], k, vt, bias, lam_vecs, attn_head_norm[j], lam_init=lam_init)
            y = y.reshape(batch * seq, d)
            w_out = (w_o, j)
        xs, feeds = _layer_tail(y, w_out, xs, norm_mixer_post[layer], norm_mlp_pre[layer],
                                (w_up, layer), (w_down, layer),
                                norm_mlp_post[layer], mixer_projections(layer + 1),
                                glu=layer < n_ssm, chunked=layer < n_ssm, batch=batch, seq=seq)
    return xs.reshape(batch, seq, d).astype(x.dtype)
```
